```python
import math
import jax, jax.numpy as jnp
from jax import lax
import numpy as np

D_MODEL = 2048
BATCH = 2
SEQ = 4096
DEPTH = 1
DEC_BATCH = 32
DEC_SEQ = 4
PAST_LEN = 8192
PAGE_SIZE = 128

ATT_HEADS = 16
HEAD_DIM = D_MODEL // ATT_HEADS
KV_HEADS = 4
ATT_W = ATT_HEADS * HEAD_DIM
KV_W = KV_HEADS * HEAD_DIM
IDX_HEADS = 16
IDX_DIM = 64
IDX_SCALE = IDX_HEADS ** -0.5 * IDX_DIM ** -0.5
TOPK_MAX = 256
Q_BLOCK = 128
ROPE_THETA = 10000.0
D_INNER = 2 * D_MODEL
SSD_HEAD_DIM = 64
SSD_HEADS = D_INNER // SSD_HEAD_DIM
SSD_GROUPS = 8
D_STATE = 128
CONV_W = 4
CHUNK = 128
CONV_DIM = D_INNER + 2 * SSD_GROUPS * D_STATE
EPS = 1e-6

IN_SPLITS = (ATT_W, KV_W, KV_W, ATT_W, IDX_HEADS * IDX_DIM, IDX_DIM, IDX_HEADS, D_INNER, CONV_DIM, SSD_HEADS, D_MODEL, D_MODEL)
IN_W = sum(IN_SPLITS)
SPLIT_POINTS = tuple(int(s) for s in np.cumsum(IN_SPLITS)[:-1])
BRANCH_W = ATT_W + D_INNER

kernel_name = "dsa_ssd_parallel_gated_decoder_step"


def rmsnorm(x, g):
    xf = x.astype(jnp.float32)
    y = xf * lax.rsqrt(jnp.mean(xf * xf, axis=-1, keepdims=True) + EPS)
    return (y * g.astype(jnp.float32)).astype(x.dtype)


def layernorm(x, w, b):
    xf = x.astype(jnp.float32)
    mu = jnp.mean(xf, axis=-1, keepdims=True)
    xc = xf - mu
    y = xc * lax.rsqrt(jnp.mean(xc * xc, axis=-1, keepdims=True) + EPS)
    return (y * w.astype(jnp.float32) + b.astype(jnp.float32)).astype(x.dtype)


def group_rmsnorm(y, g):
    b, l, w = y.shape
    yg = y.reshape(b, l, SSD_GROUPS, w // SSD_GROUPS).astype(jnp.float32)
    yg = yg * lax.rsqrt(jnp.mean(yg * yg, axis=-1, keepdims=True) + EPS)
    return (yg.reshape(b, l, w) * g.astype(jnp.float32)).astype(y.dtype)


def rope(x, pos):
    d = x.shape[-1]
    inv = ROPE_THETA ** (-jnp.arange(0, d, 2, dtype=jnp.float32) / d)
    ang = pos.astype(jnp.float32)[:, None] * inv[None, :]
    cos = jnp.cos(ang)[:, None, :]
    sin = jnp.sin(ang)[:, None, :]
    x1, x2 = jnp.split(x.astype(jnp.float32), 2, axis=-1)
    return jnp.concatenate([x1 * cos - x2 * sin, x2 * cos + x1 * sin], axis=-1).astype(x.dtype)


def causal_conv(x, prefix, w, b):
    n = x.shape[1]
    xp = jnp.concatenate([prefix.astype(x.dtype), x], axis=1)
    acc = b
    for tap in range(CONV_W):
        acc = acc + xp[:, tap:tap + n] * w[tap]
    return jax.nn.silu(acc), xp[:, n:]


def gather_pages(pool, page_table):
    g = pool[page_table]
    return g.reshape(g.shape[0], g.shape[1] * g.shape[2], *g.shape[3:])


def dsa_attend(q, qi, wi, q_pos, k_all, v_all, ki_all, k_pos, top_k):
    b, t = q.shape[0], q.shape[1]
    causal = k_pos[None, :] <= q_pos[:, None]
    dots = jax.nn.relu(jnp.einsum("bthd,bsd->bths", qi, ki_all))
    score = jnp.einsum("bths,bth->bts", dots, wi).astype(jnp.float32)
    score = jnp.where(causal[None], score, -jnp.inf)
    top_vals, idx = lax.top_k(score, top_k)
    valid = jnp.isfinite(top_vals)
    k_sel = jax.vmap(lambda kk, ii: kk[ii])(k_all, idx)
    v_sel = jax.vmap(lambda vv, ii: vv[ii])(v_all, idx)
    qg = q.reshape(b, t, KV_HEADS, ATT_HEADS // KV_HEADS, HEAD_DIM)
    s = jnp.einsum("btgrd,btkgd->btgrk", qg, k_sel).astype(jnp.float32) * (HEAD_DIM ** -0.5)
    s = jnp.where(valid[:, :, None, None, :], s, -jnp.inf)
    p = jax.nn.softmax(s, axis=-1).astype(v_sel.dtype)
    o = jnp.einsum("btgrk,btkgd->btgrd", p, v_sel)
    return o.reshape(b, t, ATT_W)


def ssd_scan(x, dt, a, bm, cm, s0):
    b, l, h, p = x.shape
    g, n = bm.shape[2], bm.shape[3]
    r = h // g
    q = min(CHUNK, l)
    pad = (-l) % q
    if pad:
        padf = lambda t: jnp.pad(t, [(0, 0), (0, pad)] + [(0, 0)] * (t.ndim - 2))
        x, dt, bm, cm = padf(x), padf(dt), padf(bm), padf(cm)
    c = (l + pad) // q
    x = x.reshape(b, c, q, g, r, p)
    dt = dt.reshape(b, c, q, g, r)
    bc = bm.reshape(b, c, q, g, n)
    cc = cm.reshape(b, c, q, g, n)
    a_cs = jnp.cumsum(dt * a.reshape(g, r), axis=2)
    seg = a_cs[:, :, :, None] - a_cs[:, :, None, :]
    mask = jnp.tril(jnp.ones((q, q), dtype=bool))[None, None, :, :, None, None]
    lmat = jnp.where(mask, jnp.exp(jnp.where(mask, seg, 0.0)), 0.0)
    xdt = x * dt[..., None]
    cb = jnp.einsum("bcign,bcjgn->bcijg", cc, bc)
    y_diag = jnp.einsum("bcijg,bcijgr,bcjgrp->bcigrp", cb, lmat, xdt)
    decay_to_end = jnp.exp(a_cs[:, :, -1:] - a_cs)
    chunk_states = jnp.einsum("bcjgn,bcjgr,bcjgrp->bcgrpn", bc, decay_to_end, xdt)
    chunk_decay = jnp.exp(a_cs[:, :, -1])

    def step(s, inp):
        dec, st = inp
        return s * dec[..., None, None] + st, s

    s_final, s_prev = lax.scan(step, s0.reshape(b, g, r, p, n),
                               (jnp.moveaxis(chunk_decay, 1, 0), jnp.moveaxis(chunk_states, 1, 0)))
    s_prev = jnp.moveaxis(s_prev, 0, 1)
    y_off = jnp.einsum("bcign,bcigr,bcgrpn->bcigrp", cc, jnp.exp(a_cs), s_prev)
    y = (y_diag + y_off).reshape(b, c * q, h, p)[:, :l]
    return y, s_final.reshape(b, h, p, n)


def mixer(xn, past_k, past_v, past_idx_k, ssm0, conv_prefix,
          w_in, conv_w, conv_b, dt_bias, a_log, d_skip, ssd_norm_g, idx_ln_w, idx_ln_b, w_branch, w_out):
    f32 = jnp.float32
    bsz, n_new, _ = xn.shape
    n_past = past_k.shape[1]
    pos = n_past + jnp.arange(n_new, dtype=jnp.int32)
    key_pos = jnp.arange(n_past + n_new, dtype=jnp.int32)
    proj = jnp.einsum("bld,de->ble", xn, w_in)
    (q, k, v, z_att, qi, ki, wi, z_ssd, xbc, dt, g_att, g_ssd) = jnp.split(proj, SPLIT_POINTS, axis=-1)

    q = rope(q.reshape(bsz, n_new, ATT_HEADS, HEAD_DIM), pos)
    k = rope(k.reshape(bsz, n_new, KV_HEADS, HEAD_DIM), pos)
    v = v.reshape(bsz, n_new, KV_HEADS, HEAD_DIM)
    qi = rope(qi.reshape(bsz, n_new, IDX_HEADS, IDX_DIM), pos)
    ki = rope(layernorm(ki, idx_ln_w, idx_ln_b)[:, :, None, :], pos)[:, :, 0, :]
    wi = wi * IDX_SCALE
    k_all = jnp.concatenate([past_k.astype(k.dtype), k], axis=1)
    v_all = jnp.concatenate([past_v.astype(v.dtype), v], axis=1)
    ki_all = jnp.concatenate([past_idx_k.astype(ki.dtype), ki], axis=1)
    top_k = max(1, min(TOPK_MAX, (n_past + n_new) // 4))
    qb = min(Q_BLOCK, n_new)
    nb = n_new // qb
    to_blocks = lambda t: jnp.moveaxis(t.reshape(bsz, nb, qb, *t.shape[2:]), 1, 0)

    def attend_block(args):
        qq, qqi, wwi, pp = args
        return dsa_attend(qq, qqi, wwi, pp, k_all, v_all, ki_all, key_pos, top_k)

    o = lax.map(attend_block, (to_blocks(q), to_blocks(qi), to_blocks(wi), pos.reshape(nb, qb)))
    o = jnp.moveaxis(o, 0, 1).reshape(bsz, n_new, ATT_W)
    y_att = o * jax.nn.silu(z_att)

    xbc_act, conv_new = causal_conv(xbc, conv_prefix, conv_w, conv_b)
    xs, bm, cm = jnp.split(xbc_act, [D_INNER, D_INNER + SSD_GROUPS * D_STATE], axis=-1)
    dt = jax.nn.softplus(dt.astype(f32) + dt_bias.astype(f32))
    a = -jnp.exp(a_log.astype(f32))
    xs_h = xs.reshape(bsz, n_new, SSD_HEADS, SSD_HEAD_DIM).astype(f32)
    y_ssd, ssm_new = ssd_scan(xs_h, dt, a,
                              bm.reshape(bsz, n_new, SSD_GROUPS, D_STATE).astype(f32),
                              cm.reshape(bsz, n_new, SSD_GROUPS, D_STATE).astype(f32),
                              ssm0.astype(f32))
    y_ssd = y_ssd + d_skip.astype(f32)[:, None] * xs_h
    y_ssd = y_ssd.reshape(bsz, n_new, D_INNER).astype(xn.dtype) * jax.nn.silu(z_ssd)
    y_ssd = group_rmsnorm(y_ssd, ssd_norm_g)

    ya = jnp.einsum("blw,wd->bld", y_att, w_branch[:ATT_W])
    yb = jnp.einsum("blw,wd->bld", y_ssd, w_branch[ATT_W:])
    merged = jax.nn.sigmoid(g_att) * ya + jax.nn.sigmoid(g_ssd) * yb
    out = jnp.einsum("bld,de->ble", merged, w_out)
    return out, k, v, ki, ssm_new.astype(xn.dtype), conv_new


def setup_inputs(seed: int = 0) -> dict:
    key = jax.random.key(seed)
    ks = jax.random.split(key, 24)
    f32 = jnp.float32
    n_pages = PAST_LEN // PAGE_SIZE
    n_phys = (DEC_BATCH * n_pages * 5) // 4
    nrm = lambda k, shape, s: s * jax.random.normal(k, shape, f32)
    x_prompt = nrm(ks[0], (BATCH, SEQ, D_MODEL), 1.0)
    x_sample = nrm(ks[1], (DEC_BATCH, DEC_SEQ, D_MODEL), 1.0)
    cache_k = nrm(ks[2], (DEPTH, n_phys, PAGE_SIZE, KV_HEADS, HEAD_DIM), 1.0)
    cache_v = nrm(ks[3], (DEPTH, n_phys, PAGE_SIZE, KV_HEADS, HEAD_DIM), 1.0)
    cache_idx_k = nrm(ks[4], (DEPTH, n_phys, PAGE_SIZE, IDX_DIM), 1.0)
    state_ssm = nrm(ks[5], (DEPTH, DEC_BATCH, SSD_HEADS, SSD_HEAD_DIM, D_STATE), 0.1)
    state_conv = nrm(ks[6], (DEPTH, DEC_BATCH, CONV_W - 1, CONV_DIM), 1.0)
    page_table = jax.random.permutation(ks[7], n_phys)[: DEC_BATCH * n_pages].reshape(DEC_BATCH, n_pages).astype(jnp.int32)
    norm_g = 1.0 + nrm(ks[8], (DEPTH, D_MODEL), 0.02)
    w_in = nrm(ks[9], (DEPTH, D_MODEL, IN_W), D_MODEL ** -0.5)
    conv_w = nrm(ks[10], (DEPTH, CONV_W, CONV_DIM), CONV_W ** -0.5)
    conv_b = nrm(ks[11], (DEPTH, CONV_DIM), 0.01)
    dt0 = jnp.exp(jax.random.uniform(ks[12], (DEPTH, SSD_HEADS), f32, math.log(1e-3), math.log(1e-1)))
    dt_bias = dt0 + jnp.log(-jnp.expm1(-dt0))
    a_log = jnp.log(jax.random.uniform(ks[13], (DEPTH, SSD_HEADS), f32, 1.0, 16.0))
    d_skip = 1.0 + nrm(ks[14], (DEPTH, SSD_HEADS), 0.1)
    ssd_norm_g = 1.0 + nrm(ks[15], (DEPTH, D_INNER), 0.02)
    idx_ln_w = 1.0 + nrm(ks[16], (DEPTH, IDX_DIM), 0.02)
    idx_ln_b = nrm(ks[17], (DEPTH, IDX_DIM), 0.01)
    w_branch = jnp.concatenate([nrm(ks[18], (DEPTH, ATT_W, D_MODEL), ATT_W ** -0.5),
                                nrm(ks[19], (DEPTH, D_INNER, D_MODEL), D_INNER ** -0.5)], axis=1)
    w_out = nrm(ks[20], (DEPTH, D_MODEL, D_MODEL), D_MODEL ** -0.5)
    final_norm_g = 1.0 + nrm(ks[21], (D_MODEL,), 0.02)
    return {"x_prompt": x_prompt, "x_sample": x_sample, "cache_k": cache_k, "cache_v": cache_v,
            "cache_idx_k": cache_idx_k, "state_ssm": state_ssm, "state_conv": state_conv,
            "page_table": page_table, "norm_g": norm_g, "w_in": w_in, "conv_w": conv_w, "conv_b": conv_b,
            "dt_bias": dt_bias, "a_log": a_log, "d_skip": d_skip, "ssd_norm_g": ssd_norm_g,
            "idx_ln_w": idx_ln_w, "idx_ln_b": idx_ln_b, "w_branch": w_branch, "w_out": w_out,
            "final_norm_g": final_norm_g}


def reference(x_prompt, x_sample, cache_k, cache_v, cache_idx_k, state_ssm, state_conv, page_table,
              norm_g, w_in, conv_w, conv_b, dt_bias, a_log, d_skip, ssd_norm_g, idx_ln_w, idx_ln_b,
              w_branch, w_out, final_norm_g):
    hp, hs = x_prompt, x_sample
    bp = hp.shape[0]
    outs_p, outs_s = [], []
    for l in range(DEPTH):
        weights = (w_in[l], conv_w[l], conv_b[l], dt_bias[l], a_log[l], d_skip[l], ssd_norm_g[l],
                   idx_ln_w[l], idx_ln_b[l], w_branch[l], w_out[l])
        dp, *st_p = mixer(rmsnorm(hp, norm_g[l]),
                          jnp.zeros((bp, 0, KV_HEADS, HEAD_DIM), hp.dtype),
                          jnp.zeros((bp, 0, KV_HEADS, HEAD_DIM), hp.dtype),
                          jnp.zeros((bp, 0, IDX_DIM), hp.dtype),
                          jnp.zeros((bp, SSD_HEADS, SSD_HEAD_DIM, D_STATE), hp.dtype),
                          jnp.zeros((bp, CONV_W - 1, CONV_DIM), hp.dtype),
                          *weights)
        hp = hp + dp
        ds, *st_s = mixer(rmsnorm(hs, norm_g[l]),
                          gather_pages(cache_k[l], page_table),
                          gather_pages(cache_v[l], page_table),
                          gather_pages(cache_idx_k[l], page_table),
                          state_ssm[l], state_conv[l],
                          *weights)
        hs = hs + ds
        outs_p.append(st_p)
        outs_s.append(st_s)
    stack = lambda lst, i: jnp.stack([o[i] for o in lst], axis=0)
    y_prompt = rmsnorm(hp, final_norm_g)
    y_sample = rmsnorm(hs, final_norm_g)
    k_prompt, v_prompt, idx_k_prompt = stack(outs_p, 0), stack(outs_p, 1), stack(outs_p, 2)
    ssm_prompt, conv_prompt = stack(outs_p, 3), stack(outs_p, 4)
    k_sample, v_sample, idx_k_sample = stack(outs_s, 0), stack(outs_s, 1), stack(outs_s, 2)
    ssm_sample, conv_sample = stack(outs_s, 3), stack(outs_s, 4)
    return (y_prompt, y_sample, k_prompt, v_prompt, idx_k_prompt, ssm_prompt, conv_prompt,
            k_sample, v_sample, idx_k_sample, ssm_sample, conv_sample)
```

```python
import functools

import numpy as np
import jax
import jax.numpy as jnp
from jax import lax
from jax.experimental import pallas as pl
from jax.experimental.pallas import tpu as pltpu

F32, BF16, I32 = jnp.float32, jnp.bfloat16, jnp.int32

ATT_HEADS = 16
KV_HEADS = 4
HEAD_DIM = 128
IDX_HEADS = 16
IDX_DIM = 64
TOPK_MAX = 256
ROPE_THETA = 10000.0
SSD_HEAD_DIM = 64
SSD_GROUPS = 8
D_STATE = 128
CONV_W = 4
CHUNK = 128
EPS = 1e-6

LANES = 128
SUBLANES = 8
VMEM_LIMIT = 56 * 1024 * 1024
INT_MIN = -2 ** 31
MASKED = -1e30
PAGES_PER_STEP = 8


def _cparams(sem):
    return pltpu.CompilerParams(dimension_semantics=sem, vmem_limit_bytes=VMEM_LIMIT)


def _nt_dot(a, b):
    return lax.dot_general(a, b, (((1,), (1,)), ((), ())), preferred_element_type=F32)


def _tile_lanes(x, n):
    return x if n == 1 else jnp.concatenate([x] * n, axis=1)


def _split3(x):
    hi = x.astype(BF16)
    r1 = x - hi.astype(F32)
    mid = r1.astype(BF16)
    lo = (r1 - mid.astype(F32)).astype(BF16)
    return hi, mid, lo


def _dot_exact_rhs(x, e):
    hi, mid, lo = _split3(x)
    d = functools.partial(jnp.dot, preferred_element_type=F32)
    return d(hi, e) + d(mid, e) + d(lo, e)


def _dot_exact_lhs(e, x):
    hi, mid, lo = _split3(x)
    d = functools.partial(jnp.dot, preferred_element_type=F32)
    return d(e, hi) + d(e, mid) + d(e, lo)


def _sortable(score):
    score = jnp.where(score == 0.0, 0.0, score)
    bits = pltpu.bitcast(score, I32)
    return jnp.where(bits < 0, bits ^ jnp.int32(0x7FFFFFFF), bits)


def _norm_kernel(x_ref, g_ref, o_ref):
    x = x_ref[...]
    ms = jnp.mean(x * x, axis=-1, keepdims=True)
    o_ref[...] = (x * lax.rsqrt(ms + EPS) * g_ref[...]).astype(o_ref.dtype)


def _rmsnorm_bf16(x, g, tm):
    m, d = x.shape
    return pl.pallas_call(
        _norm_kernel, grid=(m // tm,),
        in_specs=[pl.BlockSpec((tm, d), lambda i: (i, 0)), pl.BlockSpec((1, d), lambda i: (0, 0))],
        out_specs=pl.BlockSpec((tm, d), lambda i: (i, 0)),
        out_shape=jax.ShapeDtypeStruct((m, d), BF16),
        compiler_params=_cparams(("parallel",)), name="rmsnorm")(x, g.reshape(1, d))


def _mm_kernel(x_ref, w_ref, *rest, epilogue, n_aux):
    acc = jnp.dot(x_ref[...], w_ref[...], preferred_element_type=F32)
    epilogue(acc, rest[:n_aux], rest[n_aux:])


def _matmul(x, w, epilogue, aux, aux_specs, out_shapes, out_specs, tm, tn, name):
    m, k = x.shape
    n = w.shape[1]
    kern = functools.partial(_mm_kernel, epilogue=epilogue, n_aux=len(aux))
    return pl.pallas_call(
        kern, grid=(m // tm, n // tn),
        in_specs=[pl.BlockSpec((tm, k), lambda i, j: (i, 0)),
                  pl.BlockSpec((k, tn), lambda i, j: (0, j))] + list(aux_specs),
        out_specs=out_specs, out_shape=out_shapes,
        compiler_params=_cparams(("parallel", "parallel")), name=name)(x, w, *aux)


def _rope_half(x, cos, sin_signed, half):
    if 2 * half == LANES:
        partner = pltpu.roll(x, half, 1)
    else:
        partner = jnp.concatenate([x[:, half:], x[:, :half]], axis=1)
    return x * cos + partner * sin_signed


def _epi_rope128(acc, aux, outs):
    cos, sin = aux[0][...], aux[1][...]
    for c in range(acc.shape[1] // HEAD_DIM):
        sl = slice(c * HEAD_DIM, (c + 1) * HEAD_DIM)
        r = _rope_half(acc[:, sl], cos, sin, HEAD_DIM // 2)
        for o in outs:
            o[:, sl] = r.astype(o.dtype)


def _epi_store(acc, aux, outs, fn=None):
    val = acc if fn is None else fn(acc)
    for o in outs:
        o[...] = val.astype(o.dtype)


def _epi_qi(acc, aux, outs):
    cos, sin = aux[0][...], aux[1][...]
    for hh in range(acc.shape[1] // IDX_DIM):
        x = acc[:, hh * IDX_DIM:(hh + 1) * IDX_DIM]
        outs[0][hh] = _rope_half(x, cos, sin, IDX_DIM // 2).astype(outs[0].dtype)


def _epi_kiwi(acc, aux, outs, idx_scale):
    lnw, lnb, cos, sin = (a[...] for a in aux)
    ki = acc[:, :IDX_DIM]
    mu = jnp.mean(ki, axis=-1, keepdims=True)
    kc = ki - mu
    y = kc * lax.rsqrt(jnp.mean(kc * kc, axis=-1, keepdims=True) + EPS) * lnw + lnb
    r = _rope_half(y, cos, sin, IDX_DIM // 2)
    outs[0][...] = r
    outs[1][...] = r.astype(BF16)
    outs[2][...] = acc[:, IDX_DIM:] * idx_scale


def _epi_softplus(acc, aux, outs):
    x = acc + aux[0][...]
    outs[0][...] = jnp.maximum(x, 0.0) + jnp.log1p(jnp.exp(-jnp.abs(x)))


def _silu(x):
    return x * jax.nn.sigmoid(x)


def _rope_tables(pos, d):
    inv = ROPE_THETA ** (-jnp.arange(0, d, 2, dtype=F32) / d)
    ang = pos.astype(F32)[:, None] * inv[None, :]
    cos, sin = jnp.cos(ang), jnp.sin(ang)
    return jnp.concatenate([cos, cos], axis=-1), jnp.concatenate([-sin, sin], axis=-1)


def _projections(xn, w_in, pos_rows, n_pos_blocks, tm, idx_ln_w, idx_ln_b, dt_bias):
    m, d = xn.shape
    att_w, kv_w = ATT_HEADS * HEAD_DIM, KV_HEADS * HEAD_DIM
    d_inner = 2 * d
    conv_dim = d_inner + 2 * SSD_GROUPS * D_STATE
    ssd_heads = d_inner // SSD_HEAD_DIM
    splits = (att_w, kv_w, kv_w, att_w, IDX_HEADS * IDX_DIM, IDX_DIM, IDX_HEADS, d_inner, conv_dim,
              ssd_heads, d, d)
    assert sum(splits) == w_in.shape[1]
    off = np.concatenate([[0], np.cumsum(splits)]).astype(int)
    seg = lambda a, b: w_in[:, off[a]:off[b]].astype(BF16)

    cos128, sin128 = _rope_tables(pos_rows, HEAD_DIM)
    cos64, sin64 = _rope_tables(pos_rows, IDX_DIM)
    pos_map = lambda i, j: (i % n_pos_blocks, 0)
    t128 = [pl.BlockSpec((tm, HEAD_DIM), pos_map)] * 2
    t64 = [pl.BlockSpec((tm, IDX_DIM), pos_map)] * 2
    tile = lambda tn: pl.BlockSpec((tm, tn), lambda i, j: (i, j))
    sds = lambda n, dt: jax.ShapeDtypeStruct((m, n), dt)
    tn = 512

    (q,) = _matmul(xn, seg(0, 1), _epi_rope128, [cos128, sin128], t128, [sds(att_w, BF16)], [tile(tn)],
                   tm, tn, "proj_q")
    k, k_bf = _matmul(xn, seg(1, 2), _epi_rope128, [cos128, sin128], t128,
                      [sds(kv_w, F32), sds(kv_w, BF16)], [tile(tn)] * 2, tm, tn, "proj_k")
    v, v_bf = _matmul(xn, seg(2, 3), _epi_store, [], [], [sds(kv_w, F32), sds(kv_w, BF16)],
                      [tile(tn)] * 2, tm, tn, "proj_v")
    (z_att,) = _matmul(xn, seg(3, 4), functools.partial(_epi_store, fn=_silu), [], [],
                       [sds(att_w, BF16)], [tile(tn)], tm, tn, "proj_zatt")
    qi_tn = 4 * IDX_DIM
    (qi,) = _matmul(xn, seg(4, 5), _epi_qi, [cos64, sin64], t64,
                    [jax.ShapeDtypeStruct((IDX_HEADS, m, IDX_DIM), BF16)],
                    [pl.BlockSpec((4, tm, IDX_DIM), lambda i, j: (j, i, 0))], tm, qi_tn, "proj_qi")
    kiwi_n = IDX_DIM + IDX_HEADS
    idx_scale = IDX_HEADS ** -0.5 * IDX_DIM ** -0.5
    row1 = lambda n: pl.BlockSpec((1, n), lambda i, j: (0, 0))
    ki, ki_bf, wi = _matmul(
        xn, seg(5, 7), functools.partial(_epi_kiwi, idx_scale=idx_scale),
        [idx_ln_w.reshape(1, IDX_DIM), idx_ln_b.reshape(1, IDX_DIM), cos64, sin64],
        [row1(IDX_DIM), row1(IDX_DIM)] + t64,
        [sds(IDX_DIM, F32), sds(IDX_DIM, BF16), sds(IDX_HEADS, F32)],
        [pl.BlockSpec((tm, IDX_DIM), lambda i, j: (i, 0))] * 2 + [pl.BlockSpec((tm, IDX_HEADS), lambda i, j: (i, 0))],
        tm, kiwi_n, "proj_kiwi")
    (z_ssd,) = _matmul(xn, seg(7, 8), functools.partial(_epi_store, fn=_silu), [], [],
                       [sds(d_inner, BF16)], [tile(tn)], tm, tn, "proj_zssd")
    (xbc,) = _matmul(xn, seg(8, 9), _epi_store, [], [], [sds(conv_dim, F32)], [tile(tn)], tm, tn, "proj_xbc")
    (dt,) = _matmul(xn, seg(9, 10), _epi_softplus, [dt_bias.reshape(1, ssd_heads)], [row1(ssd_heads)],
                    [sds(ssd_heads, F32)], [pl.BlockSpec((tm, ssd_heads), lambda i, j: (i, 0))],
                    tm, ssd_heads, "proj_dt")
    (gates,) = _matmul(xn, seg(10, 12), functools.partial(_epi_store, fn=jax.nn.sigmoid), [], [],
                       [sds(2 * d, BF16)], [tile(tn)], tm, tn, "proj_gates")
    return dict(q=q, k=k, k_bf=k_bf, v=v, v_bf=v_bf, z_att=z_att, qi=qi, ki=ki, ki_bf=ki_bf, wi=wi,
                z_ssd=z_ssd, xbc=xbc, dt=dt, gates=gates)


def _kth_largest_key(count_ge, shape, top_k):
    kf = jnp.float32(top_k)
    zero = jnp.zeros(shape, I32)
    t0 = jnp.where(count_ge(zero) >= kf, zero, jnp.full(shape, INT_MIN, I32))

    def body(i, t):
        cand = t | lax.shift_left(jnp.int32(1), jnp.int32(30) - i)
        return jnp.where(count_ge(cand) >= kf, cand, t)

    t = lax.fori_loop(0, 31, body, t0)
    return jnp.maximum(t, jnp.int32(INT_MIN + 1))


def _lane_tile_sum(x):
    part = x[:, :LANES]
    for c in range(1, x.shape[1] // LANES):
        part = part + x[:, c * LANES:(c + 1) * LANES]
    return part


def _attn_kernel(q_ref, qi_ref, wi_ref, z_ref, k_ref, v_ref, ki_ref, o_ref,
                 skey_ref, wexp_ref, m_ref, l_ref, acc_ref, *, tq, tk, top_k):
    qt = pl.program_id(1)
    n_kc = ((qt + 1) * tq + tk - 1) // tk
    nl = tk // LANES
    row_pos = qt * tq + lax.broadcasted_iota(I32, (tq, 1), 0)

    wi = wi_ref[...]
    for h in range(IDX_HEADS):
        wexp_ref[h] = jnp.broadcast_to(wi[:, h:h + 1], (tq, LANES))

    def score_chunk(kc, carry):
        off = pl.multiple_of(kc * tk, tk)
        ki = ki_ref[pl.ds(off, tk), :]
        sc = jnp.zeros((tq, tk), F32)
        for h in range(IDX_HEADS):
            d = _nt_dot(qi_ref[h], ki)
            sc = sc + jnp.maximum(d, 0.0) * _tile_lanes(wexp_ref[h], nl)
        kpos = off + lax.broadcasted_iota(I32, (1, tk), 1)
        skey_ref[:, pl.ds(off, tk)] = jnp.where(kpos <= row_pos, _sortable(sc), jnp.int32(INT_MIN))
        return carry

    lax.fori_loop(0, n_kc, score_chunk, 0)

    def count_ge(cand):
        cand_t = _tile_lanes(cand, nl)

        def body(kc, cnt):
            off = pl.multiple_of(kc * tk, tk)
            ge = jnp.where(skey_ref[:, pl.ds(off, tk)] >= cand_t, 1.0, 0.0)
            return cnt + _lane_tile_sum(ge)

        cnt = lax.fori_loop(0, n_kc, body, jnp.zeros((tq, LANES), F32))
        return jnp.sum(cnt, axis=1, keepdims=True)

    thr = _tile_lanes(_kth_largest_key(count_ge, (tq, LANES), top_k), nl)

    m_ref[...] = jnp.full(m_ref.shape, MASKED, F32)
    l_ref[...] = jnp.zeros(l_ref.shape, F32)
    acc_ref[...] = jnp.zeros(acc_ref.shape, F32)
    scale = HEAD_DIM ** -0.5
    rep = ATT_HEADS // KV_HEADS

    def att_chunk(kc, carry):
        off = pl.multiple_of(kc * tk, tk)
        sel = skey_ref[:, pl.ds(off, tk)] >= thr
        for g in range(KV_HEADS):
            gs = slice(g * HEAD_DIM, (g + 1) * HEAD_DIM)
            kg = k_ref[pl.ds(off, tk), gs]
            vg = v_ref[pl.ds(off, tk), gs]
            for r in range(rep):
                h = g * rep + r
                hs = slice(h * HEAD_DIM, (h + 1) * HEAD_DIM)
                s = jnp.where(sel, _nt_dot(q_ref[:, hs], kg) * scale, MASKED)
                m_old = m_ref[h]
                m_new = jnp.maximum(m_old, jnp.max(s, axis=1, keepdims=True))
                alpha = jnp.exp(m_old - m_new)
                p = jnp.exp(s - _tile_lanes(m_new, nl))
                l_ref[h] = alpha * l_ref[h] + jnp.sum(p, axis=1, keepdims=True)
                acc_ref[:, hs] = alpha * acc_ref[:, hs] + jnp.dot(p.astype(BF16), vg, preferred_element_type=F32)
                m_ref[h] = m_new
        return carry

    lax.fori_loop(0, n_kc, att_chunk, 0)

    for h in range(ATT_HEADS):
        hs = slice(h * HEAD_DIM, (h + 1) * HEAD_DIM)
        o_ref[:, hs] = (acc_ref[:, hs] / l_ref[h] * z_ref[:, hs].astype(F32)).astype(o_ref.dtype)


def _prompt_attention(p, batch, seq, top_k):
    m = batch * seq
    tq = min(128, seq)
    tk = min(512, seq)
    nq = seq // tq
    att_w, kv_w = ATT_HEADS * HEAD_DIM, KV_HEADS * HEAD_DIM
    rows = lambda w: pl.BlockSpec((tq, w), lambda b, t: (b * nq + t, 0))
    whole = lambda w: pl.BlockSpec((seq, w), lambda b, t: (b, 0))
    kern = functools.partial(_attn_kernel, tq=tq, tk=tk, top_k=top_k)
    return pl.pallas_call(
        kern, grid=(batch, nq),
        in_specs=[rows(att_w),
                  pl.BlockSpec((IDX_HEADS, tq, IDX_DIM), lambda b, t: (0, b * nq + t, 0)),
                  rows(IDX_HEADS), rows(att_w), whole(kv_w), whole(kv_w), whole(IDX_DIM)],
        out_specs=rows(att_w),
        out_shape=jax.ShapeDtypeStruct((m, att_w), BF16),
        scratch_shapes=[pltpu.VMEM((tq, seq), I32),
                        pltpu.VMEM((IDX_HEADS, tq, LANES), F32),
                        pltpu.VMEM((ATT_HEADS, tq, LANES), F32),
                        pltpu.VMEM((ATT_HEADS, tq, LANES), F32),
                        pltpu.VMEM((tq, att_w), F32)],
        compiler_params=_cparams(("parallel", "arbitrary")), name="prompt_attention",
    )(p["q"], p["qi"], p["wi"], p["z_att"], p["k_bf"], p["v_bf"], p["ki_bf"])


def _sample_score_kernel(pt_ref, qi_ref, wi_ref, *rest, n_pages, n_tok):
    page_refs, o_ref = rest[:n_pages], rest[n_pages]
    qi = qi_ref[...]
    w = jnp.broadcast_to(wi_ref[...], (qi.shape[0], LANES))
    for k in range(n_pages):
        ki = page_refs[k][...].astype(BF16)
        r = jnp.maximum(_nt_dot(qi, ki), 0.0) * _tile_lanes(w, ki.shape[0] // LANES)
        sc = jnp.sum(r.reshape(n_tok, IDX_HEADS, ki.shape[0]), axis=1)
        o_ref[:, k * ki.shape[0]:(k + 1) * ki.shape[0]] = sc


def _sample_scores(qi_rows, wi_col, cache_idx_k, page_table, n_tok):
    bsz, n_pages_total = page_table.shape
    page = cache_idx_k.shape[1]
    g = PAGES_PER_STEP if n_pages_total % PAGES_PER_STEP == 0 else 1
    rows = n_tok * IDX_HEADS
    page_spec = lambda k: pl.BlockSpec((None, page, IDX_DIM), lambda b, s, pt: (pt[b, s * g + k], 0, 0))
    grid_spec = pltpu.PrefetchScalarGridSpec(
        num_scalar_prefetch=1, grid=(bsz, n_pages_total // g),
        in_specs=[pl.BlockSpec((None, rows, IDX_DIM), lambda b, s, pt: (b, 0, 0)),
                  pl.BlockSpec((None, rows, 1), lambda b, s, pt: (b, 0, 0))] + [page_spec(k) for k in range(g)],
        out_specs=pl.BlockSpec((None, n_tok, g * page), lambda b, s, pt: (b, 0, s)))
    kern = functools.partial(_sample_score_kernel, n_pages=g, n_tok=n_tok)
    return pl.pallas_call(
        kern, grid_spec=grid_spec,
        out_shape=jax.ShapeDtypeStruct((bsz, n_tok, n_pages_total * page), F32),
        compiler_params=_cparams(("parallel", "arbitrary")), name="sample_scores",
    )(page_table, qi_rows, wi_col, *([cache_idx_k] * g))


def _sample_select_kernel(sc_ref, qi_ref, ki_ref, wi_ref, skey_ref, skey_new_ref, thr_ref, *, n_tok, n_rows, top_k):
    rows, past = sc_ref.shape
    nk = ki_ref.shape[0]
    wi = wi_ref[...]
    ki = ki_ref[...]
    sc_new = jnp.zeros((rows, nk), F32)
    for h in range(IDX_HEADS):
        w = jnp.broadcast_to(wi[:, h:h + 1], (rows, LANES))
        sc_new = sc_new + jnp.maximum(_nt_dot(qi_ref[h], ki), 0.0) * _tile_lanes(w, nk // LANES)
    r = lax.broadcasted_iota(I32, (rows, nk), 0)
    c = lax.broadcasted_iota(I32, (rows, nk), 1)
    ok = (r // n_tok == c // n_tok) & (c % n_tok <= r % n_tok) & (r < n_rows) & (c < n_rows)
    skey_new = jnp.where(ok, _sortable(sc_new), jnp.int32(INT_MIN))
    skey_new_ref[...] = skey_new
    skey_ref[...] = _sortable(sc_ref[...])

    def count_ge(cand):
        ge = jnp.where(skey_ref[...] >= _tile_lanes(cand, past // LANES), 1.0, 0.0)
        ge_new = jnp.where(skey_new >= _tile_lanes(cand, nk // LANES), 1.0, 0.0)
        return jnp.sum(_lane_tile_sum(ge) + _lane_tile_sum(ge_new), axis=1, keepdims=True)

    thr_ref[...] = _kth_largest_key(count_ge, (rows, LANES), top_k)


def _sample_select(scores, qi, ki_new_bf, wi, n_tok, n_rows, top_k):
    rows, past = scores.shape
    nk = ki_new_bf.shape[0]
    kern = functools.partial(_sample_select_kernel, n_tok=n_tok, n_rows=n_rows, top_k=top_k)
    full = lambda shape: pl.BlockSpec(shape, lambda i: (0,) * len(shape))
    return pl.pallas_call(
        kern, grid=(1,),
        in_specs=[full((rows, past)), full((IDX_HEADS, rows, IDX_DIM)), full((nk, IDX_DIM)), full((rows, IDX_HEADS))],
        out_specs=[full((rows, past)), full((rows, nk)), full((rows, LANES))],
        out_shape=[jax.ShapeDtypeStruct((rows, past), I32), jax.ShapeDtypeStruct((rows, nk), I32),
                   jax.ShapeDtypeStruct((rows, LANES), I32)],
        compiler_params=_cparams(("arbitrary",)), name="sample_select")(scores, qi, ki_new_bf, wi)


def _sample_attn_kernel(pt_ref, q_ref, z_ref, thr_ref, skey_ref, skey_new_ref, knew_ref, vnew_ref, rexp_ref, *rest,
                        n_pages, n_tok, page):
    k_refs, v_refs = rest[:n_pages], rest[n_pages:2 * n_pages]
    o_ref, m_ref, l_ref, acc_ref = rest[2 * n_pages:]
    step = pl.program_id(1)
    rows = n_tok * ATT_HEADS
    cols = page * KV_HEADS
    rep = ATT_HEADS // KV_HEADS
    scale = HEAD_DIM ** -0.5

    @pl.when(step == 0)
    def _():
        m_ref[...] = jnp.full(m_ref.shape, MASKED, F32)
        l_ref[...] = jnp.zeros(l_ref.shape, F32)
        acc_ref[...] = jnp.zeros(acc_ref.shape, F32)

    q = q_ref[...]
    thr = thr_ref[...]
    rr = lax.broadcasted_iota(I32, (rows, cols), 0)
    cc = lax.broadcasted_iota(I32, (rows, cols), 1)
    head_match = (cc % KV_HEADS) == ((rr % ATT_HEADS) // rep)

    def process(kp, vp, skey, n_keys):
        sel = jnp.where(skey >= _tile_lanes(thr, n_keys // LANES), 1.0, 0.0).astype(BF16)
        sel_x = jnp.dot(sel, rexp_ref[...], preferred_element_type=F32)
        sel_rows = jnp.concatenate(
            [jnp.broadcast_to(sel_x[t:t + 1, :], (ATT_HEADS, cols)) for t in range(n_tok)], axis=0)
        keep = (sel_rows > 0.5) & head_match
        s = jnp.where(keep, _nt_dot(q, kp) * scale, MASKED)
        m_old = m_ref[...]
        m_new = jnp.maximum(m_old, jnp.max(s, axis=1, keepdims=True))
        alpha = jnp.exp(m_old - m_new)
        p = jnp.exp(s - _tile_lanes(m_new, cols // LANES))
        l_ref[...] = alpha * l_ref[...] + jnp.sum(p, axis=1, keepdims=True)
        acc_ref[...] = alpha * acc_ref[...] + jnp.dot(p.astype(BF16), vp, preferred_element_type=F32)
        m_ref[...] = m_new

    for k in range(n_pages):
        process(k_refs[k][...].astype(BF16), v_refs[k][...].astype(BF16),
                skey_ref[:, k * page:(k + 1) * page], page)

    @pl.when(step == pl.num_programs(1) - 1)
    def _():
        for c in range(knew_ref.shape[0] // cols):
            ks = slice(c * cols, (c + 1) * cols)
            process(knew_ref[ks, :].astype(BF16), vnew_ref[ks, :].astype(BF16),
                    skey_new_ref[:, c * page:(c + 1) * page], page)
        o_ref[...] = (acc_ref[...] / l_ref[...] * z_ref[...].astype(F32)).astype(o_ref.dtype)


def _sample_attention(q_rows, z_rows, thr, skey, skey_new, k_new, v_new, cache_k, cache_v, page_table, n_tok):
    bsz, n_pages_total = page_table.shape
    cols = cache_k.shape[1]
    page = cols // KV_HEADS
    g = PAGES_PER_STEP if n_pages_total % PAGES_PER_STEP == 0 else 1
    rows = n_tok * ATT_HEADS
    nk = skey_new.shape[-1]
    rexp = (jnp.arange(cols)[None, :] // KV_HEADS == jnp.arange(page)[:, None]).astype(BF16)
    per_b = lambda shape: pl.BlockSpec((None,) + shape, lambda b, s, pt: (b,) + (0,) * len(shape))
    const = lambda shape: pl.BlockSpec(shape, lambda b, s, pt: (0,) * len(shape))
    page_spec = lambda k: pl.BlockSpec((None, cols, HEAD_DIM), lambda b, s, pt: (pt[b, s * g + k], 0, 0))
    grid_spec = pltpu.PrefetchScalarGridSpec(
        num_scalar_prefetch=1, grid=(bsz, n_pages_total // g),
        in_specs=[per_b((rows, HEAD_DIM)), per_b((rows, HEAD_DIM)), per_b((n_tok, LANES)),
                  pl.BlockSpec((None, n_tok, g * page), lambda b, s, pt: (b, 0, s)),
                  per_b((n_tok, nk)), const(k_new.shape), const(v_new.shape), const(rexp.shape)]
                 + [page_spec(k) for k in range(g)] * 2,
        out_specs=per_b((rows, HEAD_DIM)),
        scratch_shapes=[pltpu.VMEM((rows, LANES), F32), pltpu.VMEM((rows, LANES), F32),
                        pltpu.VMEM((rows, HEAD_DIM), F32)])
    kern = functools.partial(_sample_attn_kernel, n_pages=g, n_tok=n_tok, page=page)
    return pl.pallas_call(
        kern, grid_spec=grid_spec,
        out_shape=jax.ShapeDtypeStruct((bsz, rows, HEAD_DIM), BF16),
        compiler_params=_cparams(("parallel", "arbitrary")), name="sample_attention",
    )(page_table, q_rows, z_rows, thr, skey, skey_new, k_new, v_new, rexp, *([cache_k] * g), *([cache_v] * g))


def _ssd_kernel(*refs, t_rows, has_state):
    if has_state:
        (xbc_ref, z_ref, dt_ref, conv0_ref, s0_ref, cw_ref, cb_ref, alog_ref, dskip_ref, ng_ref, e64_ref, e128_ref,
         y_ref, sfin_ref, cnew_ref, xp_ref, act_ref, dtp_ref, acst_ref, st_ref) = refs
    else:
        (xbc_ref, z_ref, dt_ref, conv0_ref, cw_ref, cb_ref, alog_ref, dskip_ref, ng_ref, e64_ref, e128_ref,
         y_ref, sfin_ref, cnew_ref, xp_ref, act_ref, dtp_ref, acst_ref, st_ref) = refs
        s0_ref = None
    c = pl.program_id(1)
    q = CHUNK
    halo = SUBLANES
    n_heads = dt_ref.shape[-1]
    d_inner = n_heads * SSD_HEAD_DIM
    gw = d_inner // SSD_GROUPS
    hpg = n_heads // SSD_GROUPS
    conv_dim = xbc_ref.shape[-1]

    @pl.when(c == 0)
    def _():
        xp_ref[0:halo, :] = conv0_ref[...]
        if t_rows < q:
            xp_ref[halo:, :] = jnp.zeros((q, conv_dim), F32)
            dtp_ref[...] = jnp.zeros(dtp_ref.shape, F32)
        for g in range(SSD_GROUPS):
            if has_state:
                st_ref[g] = s0_ref[g * hpg:(g + 1) * hpg].reshape(gw, D_STATE).T
            else:
                st_ref[g] = jnp.zeros((D_STATE, gw), F32)

    xp_ref[halo:halo + t_rows, :] = xbc_ref[...]
    dtp_ref[0:t_rows, :] = dt_ref[...]

    cblk = 512
    for cbi in range(conv_dim // cblk):
        cs = slice(cbi * cblk, (cbi + 1) * cblk)
        acc = jnp.broadcast_to(cb_ref[:, cs], (q, cblk))
        for tap in range(CONV_W):
            lo = halo - (CONV_W - 1) + tap
            acc = acc + xp_ref[lo:lo + q, cs] * cw_ref[tap:tap + 1, cs]
        act_ref[:, cs] = _silu(acc)

    dt = dtp_ref[...]
    a = -jnp.exp(alog_ref[...])
    ri = lax.broadcasted_iota(I32, (q, q), 0)
    ci = lax.broadcasted_iota(I32, (q, q), 1)
    tril = ri >= ci
    a_cs = _dot_exact_lhs(jnp.where(tril, 1.0, 0.0).astype(BF16), dt * a)
    a_last = a_cs[q - 1:q, :]
    acst_ref[...] = a_cs.T
    pad = jnp.zeros((SUBLANES - 2, n_heads), F32)
    stack = jnp.concatenate([dt, jnp.exp(a_cs), jnp.exp(a_last - a_cs), jnp.exp(a_last), dskip_ref[...], pad], axis=0)
    lane = lax.broadcasted_iota(I32, (1, LANES), 1)
    lo_half = lane < SSD_HEAD_DIM

    def group_body(g, carry):
        xs = act_ref[:, pl.ds(pl.multiple_of(g * gw, gw), gw)]
        bm = act_ref[:, pl.ds(pl.multiple_of(d_inner + g * D_STATE, D_STATE), D_STATE)]
        cm = act_ref[:, pl.ds(pl.multiple_of(d_inner + SSD_GROUPS * D_STATE + g * D_STATE, D_STATE), D_STATE)]
        ex = _dot_exact_rhs(stack, e64_ref[:, pl.ds(pl.multiple_of(g * gw, gw), gw)])
        dt_x, ea_x, dte_x = ex[0:q], ex[q:2 * q], ex[2 * q:3 * q]
        cdec_x, dskip_x = ex[3 * q:3 * q + 1], ex[3 * q + 1:3 * q + 2]
        col_x = _dot_exact_rhs(a_cs, e128_ref[:, pl.ds(pl.multiple_of(g * hpg * LANES, hpg * LANES), hpg * LANES)])
        xdt = xs * dt_x
        xdt_b = xdt.astype(BF16)
        bm_b, cm_b = bm.astype(BF16), cm.astype(BF16)
        cb = _nt_dot(cm_b, bm_b)
        st = st_ref[g]
        y_off = jnp.dot(cm_b, st.astype(BF16), preferred_element_type=F32) * ea_x
        y_parts = []
        for pr in range(hpg // 2):
            xpair = xdt_b[:, pr * LANES:(pr + 1) * LANES]
            halves = (jnp.where(lo_half, xpair, jnp.zeros_like(xpair)), jnp.where(lo_half, jnp.zeros_like(xpair), xpair))
            yp = jnp.zeros((q, LANES), F32)
            for s in range(2):
                hl = 2 * pr + s
                seg = col_x[:, hl * LANES:(hl + 1) * LANES] - acst_ref[pl.ds(g * hpg + hl, 1), :]
                lmat = jnp.where(tril, jnp.exp(jnp.where(tril, seg, 0.0)), 0.0)
                yp = yp + jnp.dot((cb * lmat).astype(BF16), halves[s], preferred_element_type=F32)
            y_parts.append(yp)
        y = jnp.concatenate(y_parts, axis=1) + y_off + dskip_x * xs
        st_ref[g] = cdec_x * st + jnp.dot(bm.T.astype(BF16), (xdt * dte_x).astype(BF16), preferred_element_type=F32)
        gsl = pl.ds(pl.multiple_of(g * gw, gw), gw)
        yz = y[0:t_rows] * z_ref[:, gsl].astype(F32)
        ms = jnp.mean(yz * yz, axis=-1, keepdims=True)
        y_ref[:, gsl] = (yz * lax.rsqrt(ms + EPS) * ng_ref[:, gsl]).astype(y_ref.dtype)
        return carry

    lax.fori_loop(0, SSD_GROUPS, group_body, 0)

    @pl.when(c == pl.num_programs(1) - 1)
    def _():
        cnew_ref[...] = xp_ref[halo + t_rows - (CONV_W - 1):halo + t_rows, :]
        for g in range(SSD_GROUPS):
            sfin_ref[g * hpg:(g + 1) * hpg] = st_ref[g].T.reshape(hpg, SSD_HEAD_DIM, D_STATE)

    if t_rows == q:
        @pl.when(c < pl.num_programs(1) - 1)
        def _():
            xp_ref[0:halo, :] = xp_ref[q:q + halo, :]


def _ssd(xbc, z, dt, conv0, s0, conv_w, conv_b, a_log, d_skip, norm_g, bsz, seq):
    conv_dim = xbc.shape[-1]
    n_heads = dt.shape[-1]
    d_inner = n_heads * SSD_HEAD_DIM
    t_rows = min(CHUNK, seq)
    n_chunks = seq // t_rows
    has_state = s0 is not None
    hpg = n_heads // SSD_GROUPS
    e64 = (jnp.arange(d_inner)[None, :] // SSD_HEAD_DIM == jnp.arange(n_heads)[:, None]).astype(BF16)
    e128 = (jnp.arange(n_heads * LANES)[None, :] // LANES == jnp.arange(n_heads)[:, None]).astype(BF16)
    x3 = lambda a: a.reshape(bsz, seq, a.shape[-1])
    rows = lambda w: pl.BlockSpec((None, t_rows, w), lambda b, c: (b, c, 0))
    const = lambda shape: pl.BlockSpec(shape, lambda b, c: (0,) * len(shape))
    state_spec = pl.BlockSpec((None, n_heads, SSD_HEAD_DIM, D_STATE), lambda b, c: (b, 0, 0, 0))
    args = [x3(xbc), x3(z), x3(dt), conv0]
    in_specs = [rows(conv_dim), rows(d_inner), rows(n_heads),
                pl.BlockSpec((None, SUBLANES, conv_dim), lambda b, c: (b, 0, 0))]
    if has_state:
        args.append(s0)
        in_specs.append(state_spec)
    args += [conv_w, conv_b.reshape(1, conv_dim), a_log.reshape(1, n_heads), d_skip.reshape(1, n_heads),
             norm_g.reshape(1, d_inner), e64, e128]
    in_specs += [const((CONV_W, conv_dim)), const((1, conv_dim)), const((1, n_heads)), const((1, n_heads)),
                 const((1, d_inner)), const(e64.shape), const(e128.shape)]
    kern = functools.partial(_ssd_kernel, t_rows=t_rows, has_state=has_state)
    y, sfin, cnew = pl.pallas_call(
        kern, grid=(bsz, n_chunks), in_specs=in_specs,
        out_specs=[rows(d_inner), state_spec,
                   pl.BlockSpec((None, CONV_W - 1, conv_dim), lambda b, c: (b, 0, 0))],
        out_shape=[jax.ShapeDtypeStruct((bsz, seq, d_inner), BF16),
                   jax.ShapeDtypeStruct((bsz, n_heads, SSD_HEAD_DIM, D_STATE), F32),
                   jax.ShapeDtypeStruct((bsz, CONV_W - 1, conv_dim), F32)],
        scratch_shapes=[pltpu.VMEM((SUBLANES + CHUNK, conv_dim), F32),
                        pltpu.VMEM((CHUNK, conv_dim), F32),
                        pltpu.VMEM((CHUNK, n_heads), F32),
                        pltpu.VMEM((n_heads, CHUNK), F32),
                        pltpu.VMEM((SSD_GROUPS, D_STATE, hpg * SSD_HEAD_DIM), F32)],
        compiler_params=_cparams(("parallel", "arbitrary")), name="ssd")(*args)
    return y.reshape(bsz * seq, d_inner), sfin, cnew


def _merge_kernel(ya_ref, ys_ref, wa_ref, ws_ref, ga_ref, gs_ref, o_ref):
    ya = jnp.dot(ya_ref[...], wa_ref[...], preferred_element_type=F32)
    yb = jnp.dot(ys_ref[...], ws_ref[...], preferred_element_type=F32)
    o_ref[...] = (ga_ref[...].astype(F32) * ya + gs_ref[...].astype(F32) * yb).astype(o_ref.dtype)


def _merge(y_att, y_ssd, wb_att, wb_ssd, gates, tm):
    m, att_w = y_att.shape
    d_inner = y_ssd.shape[1]
    d = wb_att.shape[1]
    tn = 512
    nj = d // tn
    return pl.pallas_call(
        _merge_kernel, grid=(m // tm, nj),
        in_specs=[pl.BlockSpec((tm, att_w), lambda i, j: (i, 0)), pl.BlockSpec((tm, d_inner), lambda i, j: (i, 0)),
                  pl.BlockSpec((att_w, tn), lambda i, j: (0, j)), pl.BlockSpec((d_inner, tn), lambda i, j: (0, j)),
                  pl.BlockSpec((tm, tn), lambda i, j: (i, j)), pl.BlockSpec((tm, tn), lambda i, j: (i, j + nj))],
        out_specs=pl.BlockSpec((tm, tn), lambda i, j: (i, j)),
        out_shape=jax.ShapeDtypeStruct((m, d), BF16),
        compiler_params=_cparams(("parallel", "parallel")), name="merge")(y_att, y_ssd, wb_att, wb_ssd, gates, gates)


def _out_kernel(m_ref, w_ref, x_ref, g_ref, o_ref):
    h = x_ref[...] + jnp.dot(m_ref[...], w_ref[...], preferred_element_type=F32)
    ms = jnp.mean(h * h, axis=-1, keepdims=True)
    o_ref[...] = h * lax.rsqrt(ms + EPS) * g_ref[...]


def _out_proj(merged, w_out, x, final_g, tm):
    m, d = x.shape
    return pl.pallas_call(
        _out_kernel, grid=(m // tm,),
        in_specs=[pl.BlockSpec((tm, d), lambda i: (i, 0)), pl.BlockSpec((d, d), lambda i: (0, 0)),
                  pl.BlockSpec((tm, d), lambda i: (i, 0)), pl.BlockSpec((1, d), lambda i: (0, 0))],
        out_specs=pl.BlockSpec((tm, d), lambda i: (i, 0)),
        out_shape=jax.ShapeDtypeStruct((m, d), F32),
        compiler_params=_cparams(("parallel",)), name="out_proj")(merged, w_out, x, final_g.reshape(1, d))


def _round_up(x, n):
    return (x + n - 1) // n * n


def kernel(x_prompt, x_sample, cache_k, cache_v, cache_idx_k, state_ssm, state_conv, page_table, norm_g, w_in, conv_w,
           conv_b, dt_bias, a_log, d_skip, ssd_norm_g, idx_ln_w, idx_ln_b, w_branch, w_out, final_norm_g):
    assert w_in.shape[0] == 1, "single-layer trunk"
    bp, seq, d = x_prompt.shape
    bs, n_tok, _ = x_sample.shape
    n_phys, page = cache_k.shape[1], cache_k.shape[2]
    past = page_table.shape[1] * page
    att_w, kv_w = ATT_HEADS * HEAD_DIM, KV_HEADS * HEAD_DIM
    d_inner = 2 * d
    conv_dim = d_inner + 2 * SSD_GROUPS * D_STATE
    top_k_p = max(1, min(TOPK_MAX, seq // 4))
    top_k_s = max(1, min(TOPK_MAX, (past + n_tok) // 4))

    wb = w_branch[0].astype(BF16)
    wb_att, wb_ssd = wb[:att_w], wb[att_w:]
    w_out_b = w_out[0].astype(BF16)
    w_in0 = w_in[0]

    mp = bp * seq
    tm_p = min(1024, seq)
    xp = x_prompt.reshape(mp, d)
    xn_p = _rmsnorm_bf16(xp, norm_g[0], min(512, seq))
    pp = _projections(xn_p, w_in0, jnp.arange(seq, dtype=I32), seq // tm_p, tm_p, idx_ln_w[0], idx_ln_b[0], dt_bias[0])
    y_att_p = _prompt_attention(pp, bp, seq, top_k_p)
    y_ssd_p, ssm_p, conv_p = _ssd(pp["xbc"], pp["z_ssd"], pp["dt"], jnp.zeros((bp, SUBLANES, conv_dim), F32), None,
                                  conv_w[0], conv_b[0], a_log[0], d_skip[0], ssd_norm_g[0], bp, seq)
    merged_p = _merge(y_att_p, y_ssd_p, wb_att, wb_ssd, pp["gates"], min(512, seq))
    y_p = _out_proj(merged_p, w_out_b, xp, final_norm_g, min(512, seq))

    ms_rows = bs * n_tok
    rows_pad = _round_up(ms_rows, LANES)
    xs = x_sample.reshape(ms_rows, d)
    xs_pad = jnp.pad(xs, ((0, rows_pad - ms_rows), (0, 0))) if rows_pad != ms_rows else xs
    pos_s = past + (jnp.arange(rows_pad, dtype=I32) % n_tok)
    xn_s = _rmsnorm_bf16(xs_pad, norm_g[0], rows_pad)
    ps = _projections(xn_s, w_in0, pos_s, 1, rows_pad, idx_ln_w[0], idx_ln_b[0], dt_bias[0])
    real = lambda a: a[:ms_rows]

    qi_rows = jnp.transpose(ps["qi"][:, :ms_rows], (1, 0, 2)).reshape(bs, n_tok * IDX_HEADS, IDX_DIM)
    wi_col = real(ps["wi"]).reshape(bs, n_tok * IDX_HEADS, 1)
    scores = _sample_scores(qi_rows, wi_col, cache_idx_k[0], page_table, n_tok)
    scores = scores.reshape(ms_rows, past)
    if rows_pad != ms_rows:
        scores = jnp.pad(scores, ((0, rows_pad - ms_rows), (0, 0)))
    skey, skey_new, thr = _sample_select(scores, ps["qi"], ps["ki_bf"], ps["wi"], n_tok, ms_rows, top_k_s)
    skey = real(skey).reshape(bs, n_tok, past)
    skey_new = real(skey_new).reshape(bs, n_tok, rows_pad)
    thr = real(thr).reshape(bs, n_tok, LANES)
    q_rows = real(ps["q"]).reshape(bs, n_tok * ATT_HEADS, HEAD_DIM)
    z_rows = real(ps["z_att"]).reshape(bs, n_tok * ATT_HEADS, HEAD_DIM)
    k_new = ps["k"].reshape(rows_pad * KV_HEADS, HEAD_DIM)
    v_new = ps["v"].reshape(rows_pad * KV_HEADS, HEAD_DIM)
    ck = cache_k[0].reshape(n_phys, page * KV_HEADS, HEAD_DIM)
    cv = cache_v[0].reshape(n_phys, page * KV_HEADS, HEAD_DIM)
    y_att_s = _sample_attention(q_rows, z_rows, thr, skey, skey_new, k_new, v_new, ck, cv, page_table, n_tok)
    y_att_s = y_att_s.reshape(ms_rows, att_w)

    conv0_s = jnp.pad(state_conv[0], ((0, 0), (SUBLANES - (CONV_W - 1), 0), (0, 0)))
    y_ssd_s, ssm_s, conv_s = _ssd(real(ps["xbc"]), real(ps["z_ssd"]), real(ps["dt"]), conv0_s, state_ssm[0],
                                  conv_w[0], conv_b[0], a_log[0], d_skip[0], ssd_norm_g[0], bs, n_tok)
    if rows_pad != ms_rows:
        padr = lambda a: jnp.pad(a, ((0, rows_pad - ms_rows), (0, 0)))
        y_att_s, y_ssd_s = padr(y_att_s), padr(y_ssd_s)
    merged_s = _merge(y_att_s, y_ssd_s, wb_att, wb_ssd, ps["gates"], rows_pad)
    y_s = real(_out_proj(merged_s, w_out_b, xs_pad, final_norm_g, rows_pad))

    kv5 = lambda a, b, t: a.reshape(1, b, t, KV_HEADS, HEAD_DIM)
    return (y_p.reshape(bp, seq, d), y_s.reshape(bs, n_tok, d),
            kv5(pp["k"], bp, seq), kv5(pp["v"], bp, seq), pp["ki"].reshape(1, bp, seq, IDX_DIM),
            ssm_p[None], conv_p[None],
            kv5(real(ps["k"]), bs, n_tok), kv5(real(ps["v"]), bs, n_tok), real(ps["ki"]).reshape(1, bs, n_tok, IDX_DIM),
            ssm_s[None], conv_s[None])
```

```python
import functools

import numpy as np
import jax
import jax.numpy as jnp
from jax import lax
from jax.experimental import pallas as pl
from jax.experimental.pallas import tpu as pltpu

F32, BF16, I32 = jnp.float32, jnp.bfloat16, jnp.int32

ATT_HEADS = 16
KV_HEADS = 4
HEAD_DIM = 128
IDX_HEADS = 16
IDX_DIM = 64
TOPK_MAX = 256
ROPE_THETA = 10000.0
SSD_HEAD_DIM = 64
SSD_GROUPS = 8
D_STATE = 128
CONV_W = 4
CHUNK = 128
EPS = 1e-6

LANES = 128
SUBLANES = 8
BF16_ROWS = 16
VMEM_LIMIT = 56 * 1024 * 1024
INT_MIN = -2 ** 31
MASKED = -1e30
LOG2E = 1.4426950408889634
PAGES_PER_STEP = 8


def _cparams(sem):
    return pltpu.CompilerParams(dimension_semantics=sem, vmem_limit_bytes=VMEM_LIMIT)


def _nt_dot(a, b):
    return lax.dot_general(a, b, (((1,), (1,)), ((), ())), preferred_element_type=F32)


def _tile_lanes(x, n):
    return x if n == 1 else jnp.concatenate([x] * n, axis=1)


def _split3(x):
    hi = x.astype(BF16)
    r1 = x - hi.astype(F32)
    mid = r1.astype(BF16)
    lo = (r1 - mid.astype(F32)).astype(BF16)
    return hi, mid, lo


def _dot_exact_rhs(x, e):
    hi, mid, lo = _split3(x)
    d = functools.partial(jnp.dot, preferred_element_type=F32)
    return d(hi, e) + d(mid, e) + d(lo, e)


def _dot_exact_lhs(e, x):
    hi, mid, lo = _split3(x)
    d = functools.partial(jnp.dot, preferred_element_type=F32)
    return d(e, hi) + d(e, mid) + d(e, lo)


def _sortable(score):
    score = jnp.where(score == 0.0, 0.0, score)
    bits = pltpu.bitcast(score, I32)
    return jnp.where(bits < 0, bits ^ jnp.int32(0x7FFFFFFF), bits)


def _norm_kernel(x_ref, g_ref, o_ref):
    x = x_ref[...]
    ms = jnp.mean(x * x, axis=-1, keepdims=True)
    o_ref[...] = (x * lax.rsqrt(ms + EPS) * g_ref[...]).astype(o_ref.dtype)


def _rmsnorm_bf16(x, g, tm):
    m, d = x.shape
    return pl.pallas_call(
        _norm_kernel, grid=(m // tm,),
        in_specs=[pl.BlockSpec((tm, d), lambda i: (i, 0)), pl.BlockSpec((1, d), lambda i: (0, 0))],
        out_specs=pl.BlockSpec((tm, d), lambda i: (i, 0)),
        out_shape=jax.ShapeDtypeStruct((m, d), BF16),
        compiler_params=_cparams(("parallel",)), name="rmsnorm")(x, g.reshape(1, d))


def _mm_kernel(x_ref, w_ref, *rest, epilogue, n_aux):
    acc = jnp.dot(x_ref[...], w_ref[...], preferred_element_type=F32)
    epilogue(acc, rest[:n_aux], rest[n_aux:])


def _matmul(x, w, epilogue, aux, aux_specs, out_shapes, out_specs, tm, tn, name):
    m, k = x.shape
    n = w.shape[1]
    kern = functools.partial(_mm_kernel, epilogue=epilogue, n_aux=len(aux))
    return pl.pallas_call(
        kern, grid=(m // tm, n // tn),
        in_specs=[pl.BlockSpec((tm, k), lambda i, j: (i, 0)),
                  pl.BlockSpec((k, tn), lambda i, j: (0, j))] + list(aux_specs),
        out_specs=out_specs, out_shape=out_shapes,
        compiler_params=_cparams(("parallel", "parallel")), name=name)(x, w, *aux)


def _rope_half(x, cos, sin_signed, half):
    if 2 * half == LANES:
        partner = pltpu.roll(x, half, 1)
    else:
        partner = jnp.concatenate([x[:, half:], x[:, :half]], axis=1)
    return x * cos + partner * sin_signed


def _epi_rope128(acc, aux, outs, post_scale=None, head_major=False):
    cos, sin = aux[0][...], aux[1][...]
    for c in range(acc.shape[1] // HEAD_DIM):
        sl = slice(c * HEAD_DIM, (c + 1) * HEAD_DIM)
        r = _rope_half(acc[:, sl], cos, sin, HEAD_DIM // 2)
        if post_scale is not None:
            r = r * post_scale
        for o in outs:
            if head_major:
                o[c] = r.astype(o.dtype)
            else:
                o[:, sl] = r.astype(o.dtype)


def _epi_store(acc, aux, outs, fn=None):
    val = acc if fn is None else fn(acc)
    for o in outs:
        o[...] = val.astype(o.dtype)


def _epi_qi(acc, aux, outs):
    cos, sin = aux[0][...], aux[1][...]
    for hh in range(acc.shape[1] // IDX_DIM):
        x = acc[:, hh * IDX_DIM:(hh + 1) * IDX_DIM]
        outs[0][hh] = _rope_half(x, cos, sin, IDX_DIM // 2).astype(outs[0].dtype)


def _epi_kiwi(acc, aux, outs, idx_scale):
    lnw, lnb, cos, sin = (a[...] for a in aux)
    ki = acc[:, :IDX_DIM]
    mu = jnp.mean(ki, axis=-1, keepdims=True)
    kc = ki - mu
    y = kc * lax.rsqrt(jnp.mean(kc * kc, axis=-1, keepdims=True) + EPS) * lnw + lnb
    r = _rope_half(y, cos, sin, IDX_DIM // 2)
    outs[0][...] = r
    outs[1][...] = r.astype(BF16)
    outs[2][...] = acc[:, IDX_DIM:] * idx_scale


def _epi_softplus(acc, aux, outs):
    x = acc + aux[0][...]
    outs[0][...] = jnp.maximum(x, 0.0) + jnp.log1p(jnp.exp(-jnp.abs(x)))


def _silu(x):
    return x * jax.nn.sigmoid(x)


def _rope_tables(pos, d):
    inv = ROPE_THETA ** (-jnp.arange(0, d, 2, dtype=F32) / d)
    ang = pos.astype(F32)[:, None] * inv[None, :]
    cos, sin = jnp.cos(ang), jnp.sin(ang)
    return jnp.concatenate([cos, cos], axis=-1), jnp.concatenate([-sin, sin], axis=-1)


def _projections(xn, w_in, pos_rows, n_pos_blocks, tm, idx_ln_w, idx_ln_b, dt_bias):
    m, d = xn.shape
    att_w, kv_w = ATT_HEADS * HEAD_DIM, KV_HEADS * HEAD_DIM
    d_inner = 2 * d
    conv_dim = d_inner + 2 * SSD_GROUPS * D_STATE
    ssd_heads = d_inner // SSD_HEAD_DIM
    splits = (att_w, kv_w, kv_w, att_w, IDX_HEADS * IDX_DIM, IDX_DIM, IDX_HEADS, d_inner, conv_dim,
              ssd_heads, d, d)
    assert sum(splits) == w_in.shape[1]
    off = np.concatenate([[0], np.cumsum(splits)]).astype(int)
    seg = lambda a, b: w_in[:, off[a]:off[b]].astype(BF16)

    cos128, sin128 = _rope_tables(pos_rows, HEAD_DIM)
    cos64, sin64 = _rope_tables(pos_rows, IDX_DIM)
    pos_map = lambda i, j: (i % n_pos_blocks, 0)
    t128 = [pl.BlockSpec((tm, HEAD_DIM), pos_map)] * 2
    t64 = [pl.BlockSpec((tm, IDX_DIM), pos_map)] * 2
    tile = lambda tn: pl.BlockSpec((tm, tn), lambda i, j: (i, j))
    sds = lambda n, dt: jax.ShapeDtypeStruct((m, n), dt)
    tn = 512

    (q,) = _matmul(xn, seg(0, 1), functools.partial(_epi_rope128, post_scale=HEAD_DIM ** -0.5 * LOG2E, head_major=True),
                   [cos128, sin128], t128, [jax.ShapeDtypeStruct((ATT_HEADS, m, HEAD_DIM), BF16)],
                   [pl.BlockSpec((tn // HEAD_DIM, tm, HEAD_DIM), lambda i, j: (j, i, 0))], tm, tn, "proj_q")
    k, k_bf = _matmul(xn, seg(1, 2), _epi_rope128, [cos128, sin128], t128,
                      [sds(kv_w, F32), sds(kv_w, BF16)], [tile(tn)] * 2, tm, tn, "proj_k")
    v, v_bf = _matmul(xn, seg(2, 3), _epi_store, [], [], [sds(kv_w, F32), sds(kv_w, BF16)],
                      [tile(tn)] * 2, tm, tn, "proj_v")
    (z_att,) = _matmul(xn, seg(3, 4), functools.partial(_epi_store, fn=_silu), [], [],
                       [sds(att_w, BF16)], [tile(tn)], tm, tn, "proj_zatt")
    qi_tn = 4 * IDX_DIM
    (qi,) = _matmul(xn, seg(4, 5), _epi_qi, [cos64, sin64], t64,
                    [jax.ShapeDtypeStruct((IDX_HEADS, m, IDX_DIM), BF16)],
                    [pl.BlockSpec((4, tm, IDX_DIM), lambda i, j: (j, i, 0))], tm, qi_tn, "proj_qi")
    kiwi_n = IDX_DIM + IDX_HEADS
    idx_scale = IDX_HEADS ** -0.5 * IDX_DIM ** -0.5
    row1 = lambda n: pl.BlockSpec((1, n), lambda i, j: (0, 0))
    ki, ki_bf, wi = _matmul(
        xn, seg(5, 7), functools.partial(_epi_kiwi, idx_scale=idx_scale),
        [idx_ln_w.reshape(1, IDX_DIM), idx_ln_b.reshape(1, IDX_DIM), cos64, sin64],
        [row1(IDX_DIM), row1(IDX_DIM)] + t64,
        [sds(IDX_DIM, F32), sds(IDX_DIM, BF16), sds(IDX_HEADS, F32)],
        [pl.BlockSpec((tm, IDX_DIM), lambda i, j: (i, 0))] * 2 + [pl.BlockSpec((tm, IDX_HEADS), lambda i, j: (i, 0))],
        tm, kiwi_n, "proj_kiwi")
    (z_ssd,) = _matmul(xn, seg(7, 8), functools.partial(_epi_store, fn=_silu), [], [],
                       [sds(d_inner, BF16)], [tile(tn)], tm, tn, "proj_zssd")
    (xbc,) = _matmul(xn, seg(8, 9), _epi_store, [], [], [sds(conv_dim, F32)], [tile(tn)], tm, tn, "proj_xbc")
    (dt,) = _matmul(xn, seg(9, 10), _epi_softplus, [dt_bias.reshape(1, ssd_heads)], [row1(ssd_heads)],
                    [sds(ssd_heads, F32)], [pl.BlockSpec((tm, ssd_heads), lambda i, j: (i, 0))],
                    tm, ssd_heads, "proj_dt")
    (gates,) = _matmul(xn, seg(10, 12), functools.partial(_epi_store, fn=jax.nn.sigmoid), [], [],
                       [sds(2 * d, BF16)], [tile(tn)], tm, tn, "proj_gates")
    return dict(q=q, k=k, k_bf=k_bf, v=v, v_bf=v_bf, z_att=z_att, qi=qi, ki=ki, ki_bf=ki_bf, wi=wi,
                z_ssd=z_ssd, xbc=xbc, dt=dt, gates=gates)


def _kth_largest_key(count_ge, shape, top_k):
    kf = jnp.float32(top_k)
    zero = jnp.zeros(shape, I32)
    t0 = jnp.where(count_ge(zero) >= kf, zero, jnp.full(shape, INT_MIN, I32))

    def body(i, t):
        cand = t | lax.shift_left(jnp.int32(1), jnp.int32(30) - i)
        return jnp.where(count_ge(cand) >= kf, cand, t)

    t = lax.fori_loop(0, 31, body, t0)
    return jnp.maximum(t, jnp.int32(INT_MIN + 1))


def _lane_tile_sum(x):
    part = x[:, :LANES]
    for c in range(1, x.shape[1] // LANES):
        part = part + x[:, c * LANES:(c + 1) * LANES]
    return part


def _attn_kernel(q_ref, qi_ref, wi_ref, z_ref, k_ref, v_ref, ki_ref, o_ref,
                 skey_ref, wexp_ref, bias_ref, s_ref, m_ref, l_ref, acc_ref, *, tq, tk, top_k):
    qt = pl.program_id(1)
    n_kc = ((qt + 1) * tq + tk - 1) // tk
    nl = tk // LANES
    row_pos = qt * tq + lax.broadcasted_iota(I32, (tq, 1), 0)

    wi = wi_ref[...]
    for h in range(IDX_HEADS):
        wexp_ref[h] = jnp.broadcast_to(wi[:, h:h + 1], (tq, LANES))

    def score_chunk(kc, carry):
        off = pl.multiple_of(kc * tk, tk)
        ki = ki_ref[pl.ds(off, tk), :]
        sc = jnp.zeros((tq, tk), F32)
        for h in range(IDX_HEADS):
            d = _nt_dot(qi_ref[h], ki)
            sc = sc + jnp.maximum(d, 0.0) * _tile_lanes(wexp_ref[h], nl)
        kpos = off + lax.broadcasted_iota(I32, (1, tk), 1)
        skey_ref[:, pl.ds(off, tk)] = jnp.where(kpos <= row_pos, _sortable(sc), jnp.int32(INT_MIN))
        return carry

    lax.fori_loop(0, n_kc, score_chunk, 0)

    def count_ge(cand):
        cand_t = _tile_lanes(cand, nl)

        def body(kc, cnt):
            off = pl.multiple_of(kc * tk, tk)
            ge = jnp.where(skey_ref[:, pl.ds(off, tk)] >= cand_t, 1.0, 0.0)
            return cnt + _lane_tile_sum(ge)

        cnt = lax.fori_loop(0, n_kc, body, jnp.zeros((tq, LANES), F32))
        return jnp.sum(cnt, axis=1, keepdims=True)

    thr = _tile_lanes(_kth_largest_key(count_ge, (tq, LANES), top_k), nl)

    def bias_chunk(kc, carry):
        off = pl.multiple_of(kc * tk, tk)
        bias_ref[:, pl.ds(off, tk)] = jnp.where(skey_ref[:, pl.ds(off, tk)] >= thr, 0.0, MASKED)
        return carry

    lax.fori_loop(0, n_kc, bias_chunk, 0)
    rep = ATT_HEADS // KV_HEADS

    rows = rep * tq

    def logits_chunk(g, ks):
        slot = g % 2
        gs = slice(g * HEAD_DIM, (g + 1) * HEAD_DIM)
        qg = q_ref[g * rep:(g + 1) * rep].reshape(rows, HEAD_DIM)
        bias = bias_ref[:, ks]
        s = jnp.concatenate([bias] * rep, axis=0) + _nt_dot(qg, k_ref[ks, gs])
        s_ref[slot, :, ks] = s
        part = s[:, :LANES]
        for c in range(1, nl):
            part = jnp.maximum(part, s[:, c * LANES:(c + 1) * LANES])
        m_ref[slot] = jnp.maximum(m_ref[slot], part)

    def pv_chunk(g, ks):
        slot = g % 2
        gs = slice(g * HEAD_DIM, (g + 1) * HEAD_DIM)
        p = jnp.exp2(s_ref[slot, :, ks] - _tile_lanes(m_ref[slot], nl))
        l_ref[...] += _lane_tile_sum(p)
        acc_ref[...] += jnp.dot(p.astype(BF16), v_ref[ks, gs], preferred_element_type=F32)

    for stage in range(KV_HEADS + 1):
        g_logits = stage if stage < KV_HEADS else None
        g_pv = stage - 1 if stage > 0 else None
        if g_logits is not None:
            m_ref[g_logits % 2] = jnp.full((rows, LANES), MASKED, F32)
        if g_pv is not None:
            slot = g_pv % 2
            m_ref[slot] = jnp.broadcast_to(jnp.max(m_ref[slot], axis=1, keepdims=True), (rows, LANES))
            l_ref[...] = jnp.zeros((rows, LANES), F32)
            acc_ref[...] = jnp.zeros((rows, HEAD_DIM), F32)

        def stage_chunk(kc, carry, g_logits=g_logits, g_pv=g_pv):
            ks = pl.ds(pl.multiple_of(kc * tk, tk), tk)
            if g_logits is not None:
                logits_chunk(g_logits, ks)
            if g_pv is not None:
                pv_chunk(g_pv, ks)
            return carry

        lax.fori_loop(0, n_kc, stage_chunk, 0)
        if g_pv is not None:
            o = acc_ref[...] / jnp.sum(l_ref[...], axis=1, keepdims=True)
            for r in range(rep):
                hs = slice((g_pv * rep + r) * HEAD_DIM, (g_pv * rep + r + 1) * HEAD_DIM)
                o_ref[:, hs] = (o[r * tq:(r + 1) * tq] * z_ref[:, hs].astype(F32)).astype(o_ref.dtype)


def _prompt_attention(p, batch, seq, top_k):
    m = batch * seq
    tq = min(128, seq)
    tk = min(512, seq)
    nq = seq // tq
    att_w, kv_w = ATT_HEADS * HEAD_DIM, KV_HEADS * HEAD_DIM
    rep = ATT_HEADS // KV_HEADS
    rows = lambda w: pl.BlockSpec((tq, w), lambda b, t: (b * nq + t, 0))
    whole = lambda w: pl.BlockSpec((seq, w), lambda b, t: (b, 0))
    kern = functools.partial(_attn_kernel, tq=tq, tk=tk, top_k=top_k)
    return pl.pallas_call(
        kern, grid=(batch, nq),
        in_specs=[pl.BlockSpec((ATT_HEADS, tq, HEAD_DIM), lambda b, t: (0, b * nq + t, 0)),
                  pl.BlockSpec((IDX_HEADS, tq, IDX_DIM), lambda b, t: (0, b * nq + t, 0)),
                  rows(IDX_HEADS), rows(att_w), whole(kv_w), whole(kv_w), whole(IDX_DIM)],
        out_specs=rows(att_w),
        out_shape=jax.ShapeDtypeStruct((m, att_w), BF16),
        scratch_shapes=[pltpu.VMEM((tq, seq), I32),
                        pltpu.VMEM((IDX_HEADS, tq, LANES), F32),
                        pltpu.VMEM((tq, seq), F32),
                        pltpu.VMEM((2, rep * tq, seq), F32),
                        pltpu.VMEM((2, rep * tq, LANES), F32),
                        pltpu.VMEM((rep * tq, LANES), F32),
                        pltpu.VMEM((rep * tq, HEAD_DIM), F32)],
        compiler_params=_cparams(("parallel", "arbitrary")), name="prompt_attention",
    )(p["q"], p["qi"], p["wi"], p["z_att"], p["k_bf"], p["v_bf"], p["ki_bf"])


def _sample_score_kernel(pt_ref, qi_ref, wi_ref, *rest, n_pages, n_tok):
    page_refs, o_ref = rest[:n_pages], rest[n_pages]
    qi = qi_ref[...]
    w = jnp.broadcast_to(wi_ref[...], (qi.shape[0], LANES))
    for k in range(n_pages):
        ki = page_refs[k][...].astype(BF16)
        r = jnp.maximum(_nt_dot(qi, ki), 0.0) * _tile_lanes(w, ki.shape[0] // LANES)
        sc = jnp.sum(r.reshape(n_tok, IDX_HEADS, ki.shape[0]), axis=1)
        o_ref[:, k * ki.shape[0]:(k + 1) * ki.shape[0]] = sc


def _sample_scores(qi_rows, wi_col, cache_idx_k, page_table, n_tok):
    bsz, n_pages_total = page_table.shape
    page = cache_idx_k.shape[1]
    g = PAGES_PER_STEP if n_pages_total % PAGES_PER_STEP == 0 else 1
    rows = n_tok * IDX_HEADS
    page_spec = lambda k: pl.BlockSpec((None, page, IDX_DIM), lambda b, s, pt: (pt[b, s * g + k], 0, 0))
    grid_spec = pltpu.PrefetchScalarGridSpec(
        num_scalar_prefetch=1, grid=(bsz, n_pages_total // g),
        in_specs=[pl.BlockSpec((None, rows, IDX_DIM), lambda b, s, pt: (b, 0, 0)),
                  pl.BlockSpec((None, rows, 1), lambda b, s, pt: (b, 0, 0))] + [page_spec(k) for k in range(g)],
        out_specs=pl.BlockSpec((None, n_tok, g * page), lambda b, s, pt: (b, 0, s)))
    kern = functools.partial(_sample_score_kernel, n_pages=g, n_tok=n_tok)
    return pl.pallas_call(
        kern, grid_spec=grid_spec,
        out_shape=jax.ShapeDtypeStruct((bsz, n_tok, n_pages_total * page), F32),
        compiler_params=_cparams(("parallel", "arbitrary")), name="sample_scores",
    )(page_table, qi_rows, wi_col, *([cache_idx_k] * g))


def _sample_select_kernel(sc_ref, qi_ref, ki_ref, wi_ref, rexp_ref, keep_ref, keep_new_ref, skey_ref,
                          *, n_tok, n_rows, top_k):
    rows, past = sc_ref.shape
    nk = ki_ref.shape[0]
    wi = wi_ref[...]
    ki = ki_ref[...]
    sc_new = jnp.zeros((rows, nk), F32)
    for h in range(IDX_HEADS):
        w = jnp.broadcast_to(wi[:, h:h + 1], (rows, LANES))
        sc_new = sc_new + jnp.maximum(_nt_dot(qi_ref[h], ki), 0.0) * _tile_lanes(w, nk // LANES)
    r = lax.broadcasted_iota(I32, (rows, nk), 0)
    c = lax.broadcasted_iota(I32, (rows, nk), 1)
    ok = (r // n_tok == c // n_tok) & (c % n_tok <= r % n_tok) & (r < n_rows) & (c < n_rows)
    skey_new = jnp.where(ok, _sortable(sc_new), jnp.int32(INT_MIN))
    skey_ref[...] = _sortable(sc_ref[...])

    def count_ge(cand):
        ge = jnp.where(skey_ref[...] >= _tile_lanes(cand, past // LANES), 1.0, 0.0)
        ge_new = jnp.where(skey_new >= _tile_lanes(cand, nk // LANES), 1.0, 0.0)
        return jnp.sum(_lane_tile_sum(ge) + _lane_tile_sum(ge_new), axis=1, keepdims=True)

    thr = _kth_largest_key(count_ge, (rows, LANES), top_k)
    rexp = rexp_ref[...]
    xw = rexp.shape[1]

    def expand(key_tile):
        sel = jnp.where(key_tile >= thr, 1.0, 0.0).astype(BF16)
        return jnp.dot(sel, rexp, preferred_element_type=F32).astype(BF16)

    def past_tile(ct, carry):
        src = pl.ds(pl.multiple_of(ct * LANES, LANES), LANES)
        keep_ref[:, pl.ds(pl.multiple_of(ct * xw, xw), xw)] = expand(skey_ref[:, src])
        return carry

    lax.fori_loop(0, past // LANES, past_tile, 0)
    for ct in range(nk // LANES):
        keep_new_ref[:, ct * xw:(ct + 1) * xw] = expand(skey_new[:, ct * LANES:(ct + 1) * LANES])


def _sample_select(scores, qi, ki_new_bf, wi, n_tok, n_rows, top_k):
    rows, past = scores.shape
    nk = ki_new_bf.shape[0]
    rexp = (jnp.arange(LANES * KV_HEADS)[None, :] // KV_HEADS == jnp.arange(LANES)[:, None]).astype(BF16)
    kern = functools.partial(_sample_select_kernel, n_tok=n_tok, n_rows=n_rows, top_k=top_k)
    full = lambda shape: pl.BlockSpec(shape, lambda i: (0,) * len(shape))
    return pl.pallas_call(
        kern, grid=(1,),
        in_specs=[full((rows, past)), full((IDX_HEADS, rows, IDX_DIM)), full((nk, IDX_DIM)), full((rows, IDX_HEADS)),
                  full(rexp.shape)],
        out_specs=[full((rows, past * KV_HEADS)), full((rows, nk * KV_HEADS))],
        out_shape=[jax.ShapeDtypeStruct((rows, past * KV_HEADS), BF16),
                   jax.ShapeDtypeStruct((rows, nk * KV_HEADS), BF16)],
        scratch_shapes=[pltpu.VMEM((rows, past), I32)],
        compiler_params=_cparams(("arbitrary",)), name="sample_select")(scores, qi, ki_new_bf, wi, rexp)


def _sample_attn_kernel(pt_ref, q_ref, z_ref, keep_ref, keep_new_ref, knew_ref, vnew_ref, *rest, n_pages, n_tok, page):
    k_refs, v_refs = rest[:n_pages], rest[n_pages:2 * n_pages]
    o_ref, m_ref, l_ref, acc_ref, s_ref = rest[2 * n_pages:]
    step = pl.program_id(1)
    rows = n_tok * ATT_HEADS
    cols = page * KV_HEADS
    rep = ATT_HEADS // KV_HEADS
    nl = cols // LANES

    @pl.when(step == 0)
    def _():
        m_ref[...] = jnp.full(m_ref.shape, MASKED, F32)
        l_ref[...] = jnp.zeros(l_ref.shape, F32)
        acc_ref[...] = jnp.zeros(acc_ref.shape, F32)

    q = q_ref[...]
    rr = lax.broadcasted_iota(I32, (rows, cols), 0)
    cc = lax.broadcasted_iota(I32, (rows, cols), 1)
    head_bias = jnp.where((cc % KV_HEADS) == ((rr % ATT_HEADS) // rep), 0.0, MASKED)

    def update(blocks):
        mx = None
        for i, (kp, _, keep) in enumerate(blocks):
            kb = (keep.astype(F32) - 1.0) * (-MASKED)
            kb_rows = jnp.concatenate(
                [jnp.broadcast_to(kb[t:t + 1, :], (ATT_HEADS, cols)) for t in range(n_tok)], axis=0)
            s = _nt_dot(q, kp) + (kb_rows + head_bias)
            s_ref[:, i * cols:(i + 1) * cols] = s
            part = s[:, :LANES]
            for c in range(1, nl):
                part = jnp.maximum(part, s[:, c * LANES:(c + 1) * LANES])
            mx = part if mx is None else jnp.maximum(mx, part)
        m_old = m_ref[...]
        m_new = jnp.maximum(m_old, jnp.max(mx, axis=1, keepdims=True))
        alpha = jnp.exp2(m_old - m_new)
        m_t = _tile_lanes(m_new, nl)
        lsum = jnp.zeros((rows, LANES), F32)
        pv = jnp.zeros((rows, HEAD_DIM), F32)
        for i, (_, vp, _) in enumerate(blocks):
            p = jnp.exp2(s_ref[:, i * cols:(i + 1) * cols] - m_t)
            lsum = lsum + _lane_tile_sum(p)
            pv = pv + jnp.dot(p.astype(BF16), vp, preferred_element_type=F32)
        l_ref[...] = alpha * l_ref[...] + lsum
        acc_ref[...] = alpha * acc_ref[...] + pv
        m_ref[...] = m_new

    update([(k_refs[k][...].astype(BF16), v_refs[k][...].astype(BF16), keep_ref[:, k * cols:(k + 1) * cols])
            for k in range(n_pages)])

    @pl.when(step == pl.num_programs(1) - 1)
    def _():
        update([(knew_ref[c * cols:(c + 1) * cols, :].astype(BF16), vnew_ref[c * cols:(c + 1) * cols, :].astype(BF16),
                 keep_new_ref[:, c * cols:(c + 1) * cols]) for c in range(knew_ref.shape[0] // cols)])
        lsum = jnp.sum(l_ref[...], axis=1, keepdims=True)
        o_ref[...] = (acc_ref[...] / lsum * z_ref[...].astype(F32)).astype(o_ref.dtype)


def _sample_attention(q_rows, z_rows, keep, keep_new, k_new, v_new, cache_k, cache_v, page_table, n_tok):
    bsz, n_pages_total = page_table.shape
    cols = cache_k.shape[1]
    page = cols // KV_HEADS
    g = PAGES_PER_STEP if n_pages_total % PAGES_PER_STEP == 0 else 1
    rows = n_tok * ATT_HEADS
    per_b = lambda shape: pl.BlockSpec((None,) + shape, lambda b, s, pt: (b,) + (0,) * len(shape))
    const = lambda shape: pl.BlockSpec(shape, lambda b, s, pt: (0,) * len(shape))
    page_spec = lambda k: pl.BlockSpec((None, cols, HEAD_DIM), lambda b, s, pt: (pt[b, s * g + k], 0, 0))
    grid_spec = pltpu.PrefetchScalarGridSpec(
        num_scalar_prefetch=1, grid=(bsz, n_pages_total // g),
        in_specs=[per_b((rows, HEAD_DIM)), per_b((rows, HEAD_DIM)),
                  pl.BlockSpec((None, n_tok, g * cols), lambda b, s, pt: (b, 0, s)),
                  per_b((n_tok, keep_new.shape[-1])), const(k_new.shape), const(v_new.shape)]
                 + [page_spec(k) for k in range(g)] * 2,
        out_specs=per_b((rows, HEAD_DIM)),
        scratch_shapes=[pltpu.VMEM((rows, LANES), F32), pltpu.VMEM((rows, LANES), F32),
                        pltpu.VMEM((rows, HEAD_DIM), F32), pltpu.VMEM((rows, g * cols), F32)])
    kern = functools.partial(_sample_attn_kernel, n_pages=g, n_tok=n_tok, page=page)
    return pl.pallas_call(
        kern, grid_spec=grid_spec,
        out_shape=jax.ShapeDtypeStruct((bsz, rows, HEAD_DIM), BF16),
        compiler_params=_cparams(("parallel", "arbitrary")), name="sample_attention",
    )(page_table, q_rows, z_rows, keep, keep_new, k_new, v_new, *([cache_k] * g), *([cache_v] * g))


def _ssd_kernel(*refs, t_rows, q_rows, has_state):
    if has_state:
        (xbc_ref, z_ref, dt_ref, conv0_ref, s0_ref, cw_ref, cb_ref, alog_ref, dskip_ref, ng_ref, e64_ref, e128_ref,
         y_ref, sfin_ref, cnew_ref, xp_ref, act_ref, dtp_ref, acst_ref, st_ref) = refs
    else:
        (xbc_ref, z_ref, dt_ref, conv0_ref, cw_ref, cb_ref, alog_ref, dskip_ref, ng_ref, e64_ref, e128_ref,
         y_ref, sfin_ref, cnew_ref, xp_ref, act_ref, dtp_ref, acst_ref, st_ref) = refs
        s0_ref = None
    c = pl.program_id(1)
    q = q_rows
    halo = SUBLANES
    n_heads = dt_ref.shape[-1]
    d_inner = n_heads * SSD_HEAD_DIM
    gw = d_inner // SSD_GROUPS
    hpg = n_heads // SSD_GROUPS
    conv_dim = xbc_ref.shape[-1]

    @pl.when(c == 0)
    def _():
        xp_ref[0:halo, :] = conv0_ref[...]
        if t_rows < q:
            xp_ref[halo:, :] = jnp.zeros((q, conv_dim), F32)
            dtp_ref[...] = jnp.zeros(dtp_ref.shape, F32)
        for g in range(SSD_GROUPS):
            if has_state:
                st_ref[g] = s0_ref[g * hpg:(g + 1) * hpg].reshape(gw, D_STATE).T
            else:
                st_ref[g] = jnp.zeros((D_STATE, gw), F32)

    xp_ref[halo:halo + t_rows, :] = xbc_ref[...]
    dtp_ref[0:t_rows, :] = dt_ref[...]

    cblk = 512
    for cbi in range(conv_dim // cblk):
        cs = slice(cbi * cblk, (cbi + 1) * cblk)
        acc = jnp.broadcast_to(cb_ref[:, cs], (q, cblk))
        for tap in range(CONV_W):
            lo = halo - (CONV_W - 1) + tap
            acc = acc + xp_ref[lo:lo + q, cs] * cw_ref[tap:tap + 1, cs]
        act_ref[:, cs] = _silu(acc)

    dt = dtp_ref[...]
    a = -jnp.exp(alog_ref[...])
    ri = lax.broadcasted_iota(I32, (q, q), 0)
    ci = lax.broadcasted_iota(I32, (q, q), 1)
    tril = ri >= ci
    a_cs = _dot_exact_lhs(jnp.where(tril, 1.0, 0.0).astype(BF16), dt * a)
    a_last = a_cs[q - 1:q, :]
    acst_ref[...] = a_cs.T
    pad = jnp.zeros((SUBLANES - 2, n_heads), F32)
    stack = jnp.concatenate([dt, jnp.exp(a_cs), jnp.exp(a_last - a_cs), jnp.exp(a_last), dskip_ref[...], pad], axis=0)
    lane = lax.broadcasted_iota(I32, (1, LANES), 1)
    lo_half = lane < SSD_HEAD_DIM

    def group_body(g, carry):
        xs = act_ref[:, pl.ds(pl.multiple_of(g * gw, gw), gw)]
        bm = act_ref[:, pl.ds(pl.multiple_of(d_inner + g * D_STATE, D_STATE), D_STATE)]
        cm = act_ref[:, pl.ds(pl.multiple_of(d_inner + SSD_GROUPS * D_STATE + g * D_STATE, D_STATE), D_STATE)]
        ex = _dot_exact_rhs(stack, e64_ref[:, pl.ds(pl.multiple_of(g * gw, gw), gw)])
        dt_x, ea_x, dte_x = ex[0:q], ex[q:2 * q], ex[2 * q:3 * q]
        cdec_x, dskip_x = ex[3 * q:3 * q + 1], ex[3 * q + 1:3 * q + 2]
        col_x = _dot_exact_rhs(a_cs, e128_ref[:, pl.ds(pl.multiple_of(g * hpg * LANES, hpg * LANES), hpg * LANES)])
        xdt = xs * dt_x
        xdt_b = xdt.astype(BF16)
        bm_b, cm_b = bm.astype(BF16), cm.astype(BF16)
        cb = _nt_dot(cm_b, bm_b)
        st = st_ref[g]
        y_off = jnp.dot(cm_b, st.astype(BF16), preferred_element_type=F32) * ea_x
        y_parts = []
        for pr in range(hpg // 2):
            xpair = xdt_b[:, pr * LANES:(pr + 1) * LANES]
            halves = (jnp.where(lo_half, xpair, jnp.zeros_like(xpair)), jnp.where(lo_half, jnp.zeros_like(xpair), xpair))
            yp = jnp.zeros((q, LANES), F32)
            for s in range(2):
                hl = 2 * pr + s
                seg = col_x[:, hl * LANES:hl * LANES + q] - acst_ref[pl.ds(g * hpg + hl, 1), :]
                lmat = jnp.where(tril, jnp.exp(jnp.where(tril, seg, 0.0)), 0.0)
                yp = yp + jnp.dot((cb * lmat).astype(BF16), halves[s], preferred_element_type=F32)
            y_parts.append(yp)
        y = jnp.concatenate(y_parts, axis=1) + y_off + dskip_x * xs
        st_ref[g] = cdec_x * st + jnp.dot(bm.T.astype(BF16), (xdt * dte_x).astype(BF16), preferred_element_type=F32)
        gsl = pl.ds(pl.multiple_of(g * gw, gw), gw)
        yz = y[0:t_rows] * z_ref[:, gsl].astype(F32)
        ms = jnp.mean(yz * yz, axis=-1, keepdims=True)
        y_ref[:, gsl] = (yz * lax.rsqrt(ms + EPS) * ng_ref[:, gsl]).astype(y_ref.dtype)
        return carry

    lax.fori_loop(0, SSD_GROUPS, group_body, 0)

    @pl.when(c == pl.num_programs(1) - 1)
    def _():
        cnew_ref[...] = xp_ref[halo + t_rows - (CONV_W - 1):halo + t_rows, :]
        for g in range(SSD_GROUPS):
            sfin_ref[g * hpg:(g + 1) * hpg] = st_ref[g].T.reshape(hpg, SSD_HEAD_DIM, D_STATE)

    if t_rows == q:
        @pl.when(c < pl.num_programs(1) - 1)
        def _():
            xp_ref[0:halo, :] = xp_ref[q:q + halo, :]


def _ssd(xbc, z, dt, conv0, s0, conv_w, conv_b, a_log, d_skip, norm_g, bsz, seq):
    conv_dim = xbc.shape[-1]
    n_heads = dt.shape[-1]
    d_inner = n_heads * SSD_HEAD_DIM
    t_rows = min(CHUNK, seq)
    n_chunks = seq // t_rows
    q_rows = CHUNK if t_rows == CHUNK else _round_up(t_rows, BF16_ROWS)
    has_state = s0 is not None
    hpg = n_heads // SSD_GROUPS
    e64 = (jnp.arange(d_inner)[None, :] // SSD_HEAD_DIM == jnp.arange(n_heads)[:, None]).astype(BF16)
    e128 = (jnp.arange(n_heads * LANES)[None, :] // LANES == jnp.arange(n_heads)[:, None]).astype(BF16)
    x3 = lambda a: a.reshape(bsz, seq, a.shape[-1])
    rows = lambda w: pl.BlockSpec((None, t_rows, w), lambda b, c: (b, c, 0))
    const = lambda shape: pl.BlockSpec(shape, lambda b, c: (0,) * len(shape))
    state_spec = pl.BlockSpec((None, n_heads, SSD_HEAD_DIM, D_STATE), lambda b, c: (b, 0, 0, 0))
    args = [x3(xbc), x3(z), x3(dt), conv0]
    in_specs = [rows(conv_dim), rows(d_inner), rows(n_heads),
                pl.BlockSpec((None, SUBLANES, conv_dim), lambda b, c: (b, 0, 0))]
    if has_state:
        args.append(s0)
        in_specs.append(state_spec)
    args += [conv_w, conv_b.reshape(1, conv_dim), a_log.reshape(1, n_heads), d_skip.reshape(1, n_heads),
             norm_g.reshape(1, d_inner), e64, e128]
    in_specs += [const((CONV_W, conv_dim)), const((1, conv_dim)), const((1, n_heads)), const((1, n_heads)),
                 const((1, d_inner)), const(e64.shape), const(e128.shape)]
    kern = functools.partial(_ssd_kernel, t_rows=t_rows, q_rows=q_rows, has_state=has_state)
    y, sfin, cnew = pl.pallas_call(
        kern, grid=(bsz, n_chunks), in_specs=in_specs,
        out_specs=[rows(d_inner), state_spec,
                   pl.BlockSpec((None, CONV_W - 1, conv_dim), lambda b, c: (b, 0, 0))],
        out_shape=[jax.ShapeDtypeStruct((bsz, seq, d_inner), BF16),
                   jax.ShapeDtypeStruct((bsz, n_heads, SSD_HEAD_DIM, D_STATE), F32),
                   jax.ShapeDtypeStruct((bsz, CONV_W - 1, conv_dim), F32)],
        scratch_shapes=[pltpu.VMEM((SUBLANES + q_rows, conv_dim), F32),
                        pltpu.VMEM((q_rows, conv_dim), F32),
                        pltpu.VMEM((q_rows, n_heads), F32),
                        pltpu.VMEM((n_heads, q_rows), F32),
                        pltpu.VMEM((SSD_GROUPS, D_STATE, hpg * SSD_HEAD_DIM), F32)],
        compiler_params=_cparams(("parallel", "arbitrary")), name="ssd")(*args)
    return y.reshape(bsz * seq, d_inner), sfin, cnew


def _merge_kernel(ya_ref, ys_ref, wa_ref, ws_ref, ga_ref, gs_ref, o_ref):
    ya = jnp.dot(ya_ref[...], wa_ref[...], preferred_element_type=F32)
    yb = jnp.dot(ys_ref[...], ws_ref[...], preferred_element_type=F32)
    o_ref[...] = (ga_ref[...].astype(F32) * ya + gs_ref[...].astype(F32) * yb).astype(o_ref.dtype)


def _merge(y_att, y_ssd, wb_att, wb_ssd, gates, tm):
    m, att_w = y_att.shape
    d_inner = y_ssd.shape[1]
    d = wb_att.shape[1]
    tn = 512
    nj = d // tn
    return pl.pallas_call(
        _merge_kernel, grid=(m // tm, nj),
        in_specs=[pl.BlockSpec((tm, att_w), lambda i, j: (i, 0)), pl.BlockSpec((tm, d_inner), lambda i, j: (i, 0)),
                  pl.BlockSpec((att_w, tn), lambda i, j: (0, j)), pl.BlockSpec((d_inner, tn), lambda i, j: (0, j)),
                  pl.BlockSpec((tm, tn), lambda i, j: (i, j)), pl.BlockSpec((tm, tn), lambda i, j: (i, j + nj))],
        out_specs=pl.BlockSpec((tm, tn), lambda i, j: (i, j)),
        out_shape=jax.ShapeDtypeStruct((m, d), BF16),
        compiler_params=_cparams(("parallel", "parallel")), name="merge")(y_att, y_ssd, wb_att, wb_ssd, gates, gates)


def _out_kernel(m_ref, w_ref, x_ref, g_ref, o_ref):
    h = x_ref[...] + jnp.dot(m_ref[...], w_ref[...], preferred_element_type=F32)
    ms = jnp.mean(h * h, axis=-1, keepdims=True)
    o_ref[...] = h * lax.rsqrt(ms + EPS) * g_ref[...]


def _out_proj(merged, w_out, x, final_g, tm):
    m, d = x.shape
    return pl.pallas_call(
        _out_kernel, grid=(m // tm,),
        in_specs=[pl.BlockSpec((tm, d), lambda i: (i, 0)), pl.BlockSpec((d, d), lambda i: (0, 0)),
                  pl.BlockSpec((tm, d), lambda i: (i, 0)), pl.BlockSpec((1, d), lambda i: (0, 0))],
        out_specs=pl.BlockSpec((tm, d), lambda i: (i, 0)),
        out_shape=jax.ShapeDtypeStruct((m, d), F32),
        compiler_params=_cparams(("parallel",)), name="out_proj")(merged, w_out, x, final_g.reshape(1, d))


def _round_up(x, n):
    return (x + n - 1) // n * n


def kernel(x_prompt, x_sample, cache_k, cache_v, cache_idx_k, state_ssm, state_conv, page_table, norm_g, w_in, conv_w,
           conv_b, dt_bias, a_log, d_skip, ssd_norm_g, idx_ln_w, idx_ln_b, w_branch, w_out, final_norm_g):
    assert w_in.shape[0] == 1, "single-layer trunk"
    bp, seq, d = x_prompt.shape
    bs, n_tok, _ = x_sample.shape
    n_phys, page = cache_k.shape[1], cache_k.shape[2]
    past = page_table.shape[1] * page
    att_w, kv_w = ATT_HEADS * HEAD_DIM, KV_HEADS * HEAD_DIM
    d_inner = 2 * d
    conv_dim = d_inner + 2 * SSD_GROUPS * D_STATE
    top_k_p = max(1, min(TOPK_MAX, seq // 4))
    top_k_s = max(1, min(TOPK_MAX, (past + n_tok) // 4))

    wb = w_branch[0].astype(BF16)
    wb_att, wb_ssd = wb[:att_w], wb[att_w:]
    w_out_b = w_out[0].astype(BF16)
    w_in0 = w_in[0]

    mp = bp * seq
    tm_p = min(1024, seq)
    xp = x_prompt.reshape(mp, d)
    xn_p = _rmsnorm_bf16(xp, norm_g[0], min(512, seq))
    pp = _projections(xn_p, w_in0, jnp.arange(seq, dtype=I32), seq // tm_p, tm_p, idx_ln_w[0], idx_ln_b[0], dt_bias[0])
    y_att_p = _prompt_attention(pp, bp, seq, top_k_p)
    y_ssd_p, ssm_p, conv_p = _ssd(pp["xbc"], pp["z_ssd"], pp["dt"], jnp.zeros((bp, SUBLANES, conv_dim), F32), None,
                                  conv_w[0], conv_b[0], a_log[0], d_skip[0], ssd_norm_g[0], bp, seq)
    merged_p = _merge(y_att_p, y_ssd_p, wb_att, wb_ssd, pp["gates"], min(512, seq))
    y_p = _out_proj(merged_p, w_out_b, xp, final_norm_g, min(512, seq))

    ms_rows = bs * n_tok
    rows_pad = _round_up(ms_rows, LANES)
    xs = x_sample.reshape(ms_rows, d)
    xs_pad = jnp.pad(xs, ((0, rows_pad - ms_rows), (0, 0))) if rows_pad != ms_rows else xs
    pos_s = past + (jnp.arange(rows_pad, dtype=I32) % n_tok)
    xn_s = _rmsnorm_bf16(xs_pad, norm_g[0], rows_pad)
    ps = _projections(xn_s, w_in0, pos_s, 1, rows_pad, idx_ln_w[0], idx_ln_b[0], dt_bias[0])
    real = lambda a: a[:ms_rows]

    qi_rows = jnp.transpose(ps["qi"][:, :ms_rows], (1, 0, 2)).reshape(bs, n_tok * IDX_HEADS, IDX_DIM)
    wi_col = real(ps["wi"]).reshape(bs, n_tok * IDX_HEADS, 1)
    scores = _sample_scores(qi_rows, wi_col, cache_idx_k[0], page_table, n_tok)
    scores = scores.reshape(ms_rows, past)
    if rows_pad != ms_rows:
        scores = jnp.pad(scores, ((0, rows_pad - ms_rows), (0, 0)))
    keep, keep_new = _sample_select(scores, ps["qi"], ps["ki_bf"], ps["wi"], n_tok, ms_rows, top_k_s)
    keep = real(keep).reshape(bs, n_tok, past * KV_HEADS)
    keep_new = real(keep_new).reshape(bs, n_tok, rows_pad * KV_HEADS)
    q_rows = jnp.transpose(ps["q"][:, :ms_rows], (1, 0, 2)).reshape(bs, n_tok * ATT_HEADS, HEAD_DIM)
    z_rows = real(ps["z_att"]).reshape(bs, n_tok * ATT_HEADS, HEAD_DIM)
    k_new = ps["k"].reshape(rows_pad * KV_HEADS, HEAD_DIM)
    v_new = ps["v"].reshape(rows_pad * KV_HEADS, HEAD_DIM)
    ck = cache_k[0].reshape(n_phys, page * KV_HEADS, HEAD_DIM)
    cv = cache_v[0].reshape(n_phys, page * KV_HEADS, HEAD_DIM)
    y_att_s = _sample_attention(q_rows, z_rows, keep, keep_new, k_new, v_new, ck, cv, page_table, n_tok)
    y_att_s = y_att_s.reshape(ms_rows, att_w)

    conv0_s = jnp.pad(state_conv[0], ((0, 0), (SUBLANES - (CONV_W - 1), 0), (0, 0)))
    y_ssd_s, ssm_s, conv_s = _ssd(real(ps["xbc"]), real(ps["z_ssd"]), real(ps["dt"]), conv0_s, state_ssm[0],
                                  conv_w[0], conv_b[0], a_log[0], d_skip[0], ssd_norm_g[0], bs, n_tok)
    if rows_pad != ms_rows:
        padr = lambda a: jnp.pad(a, ((0, rows_pad - ms_rows), (0, 0)))
        y_att_s, y_ssd_s = padr(y_att_s), padr(y_ssd_s)
    merged_s = _merge(y_att_s, y_ssd_s, wb_att, wb_ssd, ps["gates"], rows_pad)
    y_s = real(_out_proj(merged_s, w_out_b, xs_pad, final_norm_g, rows_pad))

    kv5 = lambda a, b, t: a.reshape(1, b, t, KV_HEADS, HEAD_DIM)
    return (y_p.reshape(bp, seq, d), y_s.reshape(bs, n_tok, d),
            kv5(pp["k"], bp, seq), kv5(pp["v"], bp, seq), pp["ki"].reshape(1, bp, seq, IDX_DIM),
            ssm_p[None], conv_p[None],
            kv5(real(ps["k"]), bs, n_tok), kv5(real(ps["v"]), bs, n_tok), real(ps["ki"]).reshape(1, bs, n_tok, IDX_DIM),
            ssm_s[None], conv_s[None])
```

```python
import functools

import numpy as np
import jax
import jax.numpy as jnp
from jax import lax
from jax.experimental import pallas as pl
from jax.experimental.pallas import tpu as pltpu

F32, BF16, I32 = jnp.float32, jnp.bfloat16, jnp.int32

ATT_HEADS = 16
KV_HEADS = 4
HEAD_DIM = 128
IDX_HEADS = 16
IDX_DIM = 64
TOPK_MAX = 256
ROPE_THETA = 10000.0
SSD_HEAD_DIM = 64
SSD_GROUPS = 8
D_STATE = 128
CONV_W = 4
CHUNK = 128
EPS = 1e-6

LANES = 128
SUBLANES = 8
BF16_ROWS = 16
VMEM_LIMIT = 56 * 1024 * 1024
INT_MIN = -2 ** 31
MASKED = -1e30
LOG2E = 1.4426950408889634
PAGES_PER_STEP = 8
SCORE_PAGES_PER_STEP = 32


def _cparams(sem):
    return pltpu.CompilerParams(dimension_semantics=sem, vmem_limit_bytes=VMEM_LIMIT)


def _nt_dot(a, b):
    return lax.dot_general(a, b, (((1,), (1,)), ((), ())), preferred_element_type=F32)


def _tile_lanes(x, n):
    return x if n == 1 else jnp.concatenate([x] * n, axis=1)


def _split3(x):
    hi = x.astype(BF16)
    r1 = x - hi.astype(F32)
    mid = r1.astype(BF16)
    lo = (r1 - mid.astype(F32)).astype(BF16)
    return hi, mid, lo


def _dot_exact_rhs(x, e):
    hi, mid, lo = _split3(x)
    d = functools.partial(jnp.dot, preferred_element_type=F32)
    return d(hi, e) + d(mid, e) + d(lo, e)


def _dot_exact_lhs(e, x):
    hi, mid, lo = _split3(x)
    d = functools.partial(jnp.dot, preferred_element_type=F32)
    return d(e, hi) + d(e, mid) + d(e, lo)


def _sortable(score):
    score = jnp.where(score == 0.0, 0.0, score)
    bits = pltpu.bitcast(score, I32)
    return jnp.where(bits < 0, bits ^ jnp.int32(0x7FFFFFFF), bits)


def _norm_kernel(x_ref, g_ref, o_ref):
    x = x_ref[...]
    ms = jnp.mean(x * x, axis=-1, keepdims=True)
    o_ref[...] = (x * lax.rsqrt(ms + EPS) * g_ref[...]).astype(o_ref.dtype)


def _rmsnorm_bf16(x, g, tm):
    m, d = x.shape
    return pl.pallas_call(
        _norm_kernel, grid=(m // tm,),
        in_specs=[pl.BlockSpec((tm, d), lambda i: (i, 0)), pl.BlockSpec((1, d), lambda i: (0, 0))],
        out_specs=pl.BlockSpec((tm, d), lambda i: (i, 0)),
        out_shape=jax.ShapeDtypeStruct((m, d), BF16),
        compiler_params=_cparams(("parallel",)), name="rmsnorm")(x, g.reshape(1, d))


def _mm_kernel(x_ref, w_ref, *rest, epilogue, n_aux):
    acc = _nt_dot(x_ref[...], w_ref[...].astype(BF16))
    epilogue(acc, rest[:n_aux], rest[n_aux:])


def _matmul(x, wt, row0, n, epilogue, aux, aux_specs, out_shapes, out_specs, tm, tn, name):
    m, k = x.shape
    kern = functools.partial(_mm_kernel, epilogue=epilogue, n_aux=len(aux))
    return pl.pallas_call(
        kern, grid=(m // tm, n // tn),
        in_specs=[pl.BlockSpec((tm, k), lambda i, j: (i, 0)),
                  pl.BlockSpec((pl.Element(tn), pl.Element(k)),
                               lambda i, j: (pl.multiple_of(row0 + j * tn, SUBLANES), 0))] + list(aux_specs),
        out_specs=out_specs, out_shape=out_shapes,
        compiler_params=_cparams(("parallel", "parallel")), name=name)(x, wt, *aux)


def _rope_half(x, cos, sin_signed, half):
    if 2 * half == LANES:
        partner = pltpu.roll(x, half, 1)
    else:
        partner = jnp.concatenate([x[:, half:], x[:, :half]], axis=1)
    return x * cos + partner * sin_signed


def _epi_rope128(acc, aux, outs, post_scale=None, head_major=False):
    cos, sin = aux[0][...], aux[1][...]
    for c in range(acc.shape[1] // HEAD_DIM):
        sl = slice(c * HEAD_DIM, (c + 1) * HEAD_DIM)
        r = _rope_half(acc[:, sl], cos, sin, HEAD_DIM // 2)
        if post_scale is not None:
            r = r * post_scale
        for o in outs:
            if head_major:
                o[c] = r.astype(o.dtype)
            else:
                o[:, sl] = r.astype(o.dtype)


def _epi_store(acc, aux, outs, fn=None):
    val = acc if fn is None else fn(acc)
    for o in outs:
        o[...] = val.astype(o.dtype)


def _epi_qi(acc, aux, outs):
    cos, sin = aux[0][...], aux[1][...]
    for hh in range(acc.shape[1] // IDX_DIM):
        x = acc[:, hh * IDX_DIM:(hh + 1) * IDX_DIM]
        outs[0][hh] = _rope_half(x, cos, sin, IDX_DIM // 2).astype(outs[0].dtype)


def _epi_kiwi(acc, aux, outs, idx_scale):
    lnw, lnb, cos, sin = (a[...] for a in aux)
    ki = acc[:, :IDX_DIM]
    mu = jnp.mean(ki, axis=-1, keepdims=True)
    kc = ki - mu
    y = kc * lax.rsqrt(jnp.mean(kc * kc, axis=-1, keepdims=True) + EPS) * lnw + lnb
    r = _rope_half(y, cos, sin, IDX_DIM // 2)
    outs[0][...] = r
    outs[1][...] = r.astype(BF16)
    outs[2][...] = acc[:, IDX_DIM:] * idx_scale


def _epi_softplus(acc, aux, outs):
    x = acc + aux[0][...]
    outs[0][...] = jnp.maximum(x, 0.0) + jnp.log1p(jnp.exp(-jnp.abs(x)))


def _silu(x):
    return x * jax.nn.sigmoid(x)


def _rope_tables(pos, d):
    inv = ROPE_THETA ** (-jnp.arange(0, d, 2, dtype=F32) / d)
    ang = pos.astype(F32)[:, None] * inv[None, :]
    cos, sin = jnp.cos(ang), jnp.sin(ang)
    return jnp.concatenate([cos, cos], axis=-1), jnp.concatenate([-sin, sin], axis=-1)


def _projections(xn, wt, pos_rows, n_pos_blocks, tm, idx_ln_w, idx_ln_b, dt_bias):
    m, d = xn.shape
    att_w, kv_w = ATT_HEADS * HEAD_DIM, KV_HEADS * HEAD_DIM
    d_inner = 2 * d
    conv_dim = d_inner + 2 * SSD_GROUPS * D_STATE
    ssd_heads = d_inner // SSD_HEAD_DIM
    splits = (att_w, kv_w, kv_w, att_w, IDX_HEADS * IDX_DIM, IDX_DIM, IDX_HEADS, d_inner, conv_dim,
              ssd_heads, d, d)
    assert sum(splits) == wt.shape[0]
    off = np.concatenate([[0], np.cumsum(splits)]).astype(int)
    assert all(o % BF16_ROWS == 0 for o in off), "segment rows of the weight must start on packed-row boundaries"
    seg = lambda a, b: (wt, int(off[a]), int(off[b] - off[a]))

    cos128, sin128 = _rope_tables(pos_rows, HEAD_DIM)
    cos64, sin64 = _rope_tables(pos_rows, IDX_DIM)
    pos_map = lambda i, j: (i % n_pos_blocks, 0)
    t128 = [pl.BlockSpec((tm, HEAD_DIM), pos_map)] * 2
    t64 = [pl.BlockSpec((tm, IDX_DIM), pos_map)] * 2
    tile = lambda tn: pl.BlockSpec((tm, tn), lambda i, j: (i, j))
    sds = lambda n, dt: jax.ShapeDtypeStruct((m, n), dt)
    tn = 512
    tw = 1024

    (q,) = _matmul(xn, *seg(0, 1), functools.partial(_epi_rope128, post_scale=HEAD_DIM ** -0.5 * LOG2E, head_major=True),
                   [cos128, sin128], t128, [jax.ShapeDtypeStruct((ATT_HEADS, m, HEAD_DIM), BF16)],
                   [pl.BlockSpec((tw // HEAD_DIM, tm, HEAD_DIM), lambda i, j: (j, i, 0))], tm, tw, "proj_q")
    k, k_bf = _matmul(xn, *seg(1, 2), _epi_rope128, [cos128, sin128], t128,
                      [sds(kv_w, F32), sds(kv_w, BF16)], [tile(tn)] * 2, tm, tn, "proj_k")
    v, v_bf = _matmul(xn, *seg(2, 3), _epi_store, [], [], [sds(kv_w, F32), sds(kv_w, BF16)],
                      [tile(tn)] * 2, tm, tn, "proj_v")
    (z_att,) = _matmul(xn, *seg(3, 4), functools.partial(_epi_store, fn=_silu), [], [],
                       [sds(att_w, BF16)], [tile(tw)], tm, tw, "proj_zatt")
    qi_tn = 4 * IDX_DIM
    (qi,) = _matmul(xn, *seg(4, 5), _epi_qi, [cos64, sin64], t64,
                    [jax.ShapeDtypeStruct((IDX_HEADS, m, IDX_DIM), BF16)],
                    [pl.BlockSpec((4, tm, IDX_DIM), lambda i, j: (j, i, 0))], tm, qi_tn, "proj_qi")
    kiwi_n = IDX_DIM + IDX_HEADS
    idx_scale = IDX_HEADS ** -0.5 * IDX_DIM ** -0.5
    row1 = lambda n: pl.BlockSpec((1, n), lambda i, j: (0, 0))
    ki, ki_bf, wi = _matmul(
        xn, *seg(5, 7), functools.partial(_epi_kiwi, idx_scale=idx_scale),
        [idx_ln_w.reshape(1, IDX_DIM), idx_ln_b.reshape(1, IDX_DIM), cos64, sin64],
        [row1(IDX_DIM), row1(IDX_DIM)] + t64,
        [sds(IDX_DIM, F32), sds(IDX_DIM, BF16), sds(IDX_HEADS, F32)],
        [pl.BlockSpec((tm, IDX_DIM), lambda i, j: (i, 0))] * 2 + [pl.BlockSpec((tm, IDX_HEADS), lambda i, j: (i, 0))],
        tm, kiwi_n, "proj_kiwi")
    (z_ssd,) = _matmul(xn, *seg(7, 8), functools.partial(_epi_store, fn=_silu), [], [],
                       [sds(d_inner, BF16)], [tile(tw)], tm, tw, "proj_zssd")
    (xbc,) = _matmul(xn, *seg(8, 9), _epi_store, [], [], [sds(conv_dim, F32)], [tile(tw)], tm, tw, "proj_xbc")
    (dt,) = _matmul(xn, *seg(9, 10), _epi_softplus, [dt_bias.reshape(1, ssd_heads)], [row1(ssd_heads)],
                    [sds(ssd_heads, F32)], [pl.BlockSpec((tm, ssd_heads), lambda i, j: (i, 0))],
                    tm, ssd_heads, "proj_dt")
    (gates,) = _matmul(xn, *seg(10, 12), functools.partial(_epi_store, fn=jax.nn.sigmoid), [], [],
                       [sds(2 * d, BF16)], [tile(tw)], tm, tw, "proj_gates")
    return dict(q=q, k=k, k_bf=k_bf, v=v, v_bf=v_bf, z_att=z_att, qi=qi, ki=ki, ki_bf=ki_bf, wi=wi,
                z_ssd=z_ssd, xbc=xbc, dt=dt, gates=gates)


def _kth_largest_key(count_ge, shape, top_k):
    kf = jnp.float32(top_k)
    zero = jnp.zeros(shape, I32)
    t0 = jnp.where(count_ge(zero) >= kf, zero, jnp.full(shape, INT_MIN, I32))

    def body(i, t):
        cand = t | lax.shift_left(jnp.int32(1), jnp.int32(30) - i)
        return jnp.where(count_ge(cand) >= kf, cand, t)

    t = lax.fori_loop(0, 31, body, t0)
    return jnp.maximum(t, jnp.int32(INT_MIN + 1))


def _lane_tile_sum(x):
    part = x[:, :LANES]
    for c in range(1, x.shape[1] // LANES):
        part = part + x[:, c * LANES:(c + 1) * LANES]
    return part


def _attn_kernel(q_ref, qi_ref, wi_ref, z_ref, k_ref, v_ref, ki_ref, o_ref,
                 skey_ref, wexp_ref, bias_ref, s_ref, m_ref, l_ref, acc_ref, *, tq, tk, top_k):
    qt = pl.program_id(1)
    n_kc = ((qt + 1) * tq + tk - 1) // tk
    nl = tk // LANES
    row_pos = qt * tq + lax.broadcasted_iota(I32, (tq, 1), 0)

    wi = wi_ref[...]
    for h in range(IDX_HEADS):
        wexp_ref[h] = jnp.broadcast_to(wi[:, h:h + 1], (tq, LANES))

    def score_chunk(kc, carry):
        off = pl.multiple_of(kc * tk, tk)
        ki = ki_ref[pl.ds(off, tk), :]
        sc = jnp.zeros((tq, tk), F32)
        for h in range(IDX_HEADS):
            d = _nt_dot(qi_ref[h], ki)
            sc = sc + jnp.maximum(d, 0.0) * _tile_lanes(wexp_ref[h], nl)
        kpos = off + lax.broadcasted_iota(I32, (1, tk), 1)
        skey_ref[:, pl.ds(off, tk)] = jnp.where(kpos <= row_pos, _sortable(sc), jnp.int32(INT_MIN))
        return carry

    lax.fori_loop(0, n_kc, score_chunk, 0)

    def count_ge(cand):
        cand_t = _tile_lanes(cand, nl)

        def body(kc, cnt):
            off = pl.multiple_of(kc * tk, tk)
            ge = jnp.where(skey_ref[:, pl.ds(off, tk)] >= cand_t, 1.0, 0.0)
            return cnt + _lane_tile_sum(ge)

        cnt = lax.fori_loop(0, n_kc, body, jnp.zeros((tq, LANES), F32))
        return jnp.sum(cnt, axis=1, keepdims=True)

    thr = _tile_lanes(_kth_largest_key(count_ge, (tq, LANES), top_k), nl)

    def bias_chunk(kc, carry):
        off = pl.multiple_of(kc * tk, tk)
        bias_ref[:, pl.ds(off, tk)] = jnp.where(skey_ref[:, pl.ds(off, tk)] >= thr, 0.0, MASKED)
        return carry

    lax.fori_loop(0, n_kc, bias_chunk, 0)
    rep = ATT_HEADS // KV_HEADS

    rows = rep * tq

    def logits_chunk(g, ks):
        slot = g % 2
        gs = slice(g * HEAD_DIM, (g + 1) * HEAD_DIM)
        qg = q_ref[g * rep:(g + 1) * rep].reshape(rows, HEAD_DIM)
        bias = bias_ref[:, ks]
        s = jnp.concatenate([bias] * rep, axis=0) + _nt_dot(qg, k_ref[ks, gs])
        s_ref[slot, :, ks] = s
        part = s[:, :LANES]
        for c in range(1, nl):
            part = jnp.maximum(part, s[:, c * LANES:(c + 1) * LANES])
        m_ref[slot] = jnp.maximum(m_ref[slot], part)

    def pv_chunk(g, ks):
        slot = g % 2
        gs = slice(g * HEAD_DIM, (g + 1) * HEAD_DIM)
        p = jnp.exp2(s_ref[slot, :, ks] - _tile_lanes(m_ref[slot], nl))
        l_ref[...] += _lane_tile_sum(p)
        acc_ref[...] += jnp.dot(p.astype(BF16), v_ref[ks, gs], preferred_element_type=F32)

    for stage in range(KV_HEADS + 1):
        g_logits = stage if stage < KV_HEADS else None
        g_pv = stage - 1 if stage > 0 else None
        if g_logits is not None:
            m_ref[g_logits % 2] = jnp.full((rows, LANES), MASKED, F32)
        if g_pv is not None:
            slot = g_pv % 2
            m_ref[slot] = jnp.broadcast_to(jnp.max(m_ref[slot], axis=1, keepdims=True), (rows, LANES))
            l_ref[...] = jnp.zeros((rows, LANES), F32)
            acc_ref[...] = jnp.zeros((rows, HEAD_DIM), F32)

        def stage_chunk(kc, carry, g_logits=g_logits, g_pv=g_pv):
            ks = pl.ds(pl.multiple_of(kc * tk, tk), tk)
            if g_logits is not None:
                logits_chunk(g_logits, ks)
            if g_pv is not None:
                pv_chunk(g_pv, ks)
            return carry

        lax.fori_loop(0, n_kc, stage_chunk, 0)
        if g_pv is not None:
            o = acc_ref[...] / jnp.sum(l_ref[...], axis=1, keepdims=True)
            for r in range(rep):
                hs = slice((g_pv * rep + r) * HEAD_DIM, (g_pv * rep + r + 1) * HEAD_DIM)
                o_ref[:, hs] = (o[r * tq:(r + 1) * tq] * z_ref[:, hs].astype(F32)).astype(o_ref.dtype)


def _prompt_attention(p, batch, seq, top_k):
    m = batch * seq
    tq = min(128, seq)
    tk = min(512, seq)
    nq = seq // tq
    att_w, kv_w = ATT_HEADS * HEAD_DIM, KV_HEADS * HEAD_DIM
    rep = ATT_HEADS // KV_HEADS
    rows = lambda w: pl.BlockSpec((tq, w), lambda b, t: (b * nq + t, 0))
    whole = lambda w: pl.BlockSpec((seq, w), lambda b, t: (b, 0))
    kern = functools.partial(_attn_kernel, tq=tq, tk=tk, top_k=top_k)
    return pl.pallas_call(
        kern, grid=(batch, nq),
        in_specs=[pl.BlockSpec((ATT_HEADS, tq, HEAD_DIM), lambda b, t: (0, b * nq + t, 0)),
                  pl.BlockSpec((IDX_HEADS, tq, IDX_DIM), lambda b, t: (0, b * nq + t, 0)),
                  rows(IDX_HEADS), rows(att_w), whole(kv_w), whole(kv_w), whole(IDX_DIM)],
        out_specs=rows(att_w),
        out_shape=jax.ShapeDtypeStruct((m, att_w), BF16),
        scratch_shapes=[pltpu.VMEM((tq, seq), I32),
                        pltpu.VMEM((IDX_HEADS, tq, LANES), F32),
                        pltpu.VMEM((tq, seq), F32),
                        pltpu.VMEM((2, rep * tq, seq), F32),
                        pltpu.VMEM((2, rep * tq, LANES), F32),
                        pltpu.VMEM((rep * tq, LANES), F32),
                        pltpu.VMEM((rep * tq, HEAD_DIM), F32)],
        compiler_params=_cparams(("parallel", "arbitrary")), name="prompt_attention",
    )(p["q"], p["qi"], p["wi"], p["z_att"], p["k_bf"], p["v_bf"], p["ki_bf"])


def _sample_score_kernel(pt_ref, qi_ref, wi_ref, *rest, n_pages, n_tok):
    page_refs, o_ref = rest[:n_pages], rest[n_pages]
    qi = qi_ref[...]
    w = jnp.broadcast_to(wi_ref[...], (qi.shape[0], LANES))
    for k in range(n_pages):
        kit = page_refs[k][...].astype(BF16)
        page = kit.shape[1]
        r = jnp.maximum(jnp.dot(qi, kit, preferred_element_type=F32), 0.0) * _tile_lanes(w, page // LANES)
        sc = jnp.sum(r.reshape(n_tok, IDX_HEADS, page), axis=1)
        o_ref[:, k * page:(k + 1) * page] = sc


def _sample_scores(qi_rows, wi_col, cache_idx_kt, page_table, n_tok):
    bsz, n_pages_total = page_table.shape
    page = cache_idx_kt.shape[2]
    g = SCORE_PAGES_PER_STEP if n_pages_total % SCORE_PAGES_PER_STEP == 0 else 1
    rows = n_tok * IDX_HEADS
    page_spec = lambda k: pl.BlockSpec((None, IDX_DIM, page), lambda b, s, pt: (pt[b, s * g + k], 0, 0))
    grid_spec = pltpu.PrefetchScalarGridSpec(
        num_scalar_prefetch=1, grid=(bsz, n_pages_total // g),
        in_specs=[pl.BlockSpec((None, rows, IDX_DIM), lambda b, s, pt: (b, 0, 0)),
                  pl.BlockSpec((None, rows, 1), lambda b, s, pt: (b, 0, 0))] + [page_spec(k) for k in range(g)],
        out_specs=pl.BlockSpec((None, n_tok, g * page), lambda b, s, pt: (b, 0, s)))
    kern = functools.partial(_sample_score_kernel, n_pages=g, n_tok=n_tok)
    return pl.pallas_call(
        kern, grid_spec=grid_spec,
        out_shape=jax.ShapeDtypeStruct((bsz, n_tok, n_pages_total * page), F32),
        compiler_params=_cparams(("parallel", "arbitrary")), name="sample_scores",
    )(page_table, qi_rows, wi_col, *([cache_idx_kt] * g))


def _sample_select_kernel(sc_ref, qi_ref, ki_ref, wi_ref, rexp_ref, keep_ref, keep_new_ref, skey_ref,
                          *, n_tok, n_rows, top_k):
    rows, past = sc_ref.shape
    nk = ki_ref.shape[0]
    wi = wi_ref[...]
    ki = ki_ref[...]
    sc_new = jnp.zeros((rows, nk), F32)
    for h in range(IDX_HEADS):
        w = jnp.broadcast_to(wi[:, h:h + 1], (rows, LANES))
        sc_new = sc_new + jnp.maximum(_nt_dot(qi_ref[h], ki), 0.0) * _tile_lanes(w, nk // LANES)
    r = lax.broadcasted_iota(I32, (rows, nk), 0)
    c = lax.broadcasted_iota(I32, (rows, nk), 1)
    ok = (r // n_tok == c // n_tok) & (c % n_tok <= r % n_tok) & (r < n_rows) & (c < n_rows)
    skey_new = jnp.where(ok, _sortable(sc_new), jnp.int32(INT_MIN))
    skey_ref[...] = _sortable(sc_ref[...])

    def count_ge(cand):
        ge = jnp.where(skey_ref[...] >= _tile_lanes(cand, past // LANES), 1.0, 0.0)
        ge_new = jnp.where(skey_new >= _tile_lanes(cand, nk // LANES), 1.0, 0.0)
        return jnp.sum(_lane_tile_sum(ge) + _lane_tile_sum(ge_new), axis=1, keepdims=True)

    thr = _kth_largest_key(count_ge, (rows, LANES), top_k)
    rexp = rexp_ref[...]
    xw = rexp.shape[1]

    def expand(key_tile):
        sel = jnp.where(key_tile >= thr, 1.0, 0.0).astype(BF16)
        return jnp.dot(sel, rexp, preferred_element_type=F32).astype(BF16)

    def past_tile(ct, carry):
        src = pl.ds(pl.multiple_of(ct * LANES, LANES), LANES)
        keep_ref[:, pl.ds(pl.multiple_of(ct * xw, xw), xw)] = expand(skey_ref[:, src])
        return carry

    lax.fori_loop(0, past // LANES, past_tile, 0)
    for ct in range(nk // LANES):
        keep_new_ref[:, ct * xw:(ct + 1) * xw] = expand(skey_new[:, ct * LANES:(ct + 1) * LANES])


def _sample_select(scores, qi, ki_new_bf, wi, n_tok, n_rows, top_k):
    rows, past = scores.shape
    nk = ki_new_bf.shape[0]
    rexp = (jnp.arange(LANES * KV_HEADS)[None, :] // KV_HEADS == jnp.arange(LANES)[:, None]).astype(BF16)
    kern = functools.partial(_sample_select_kernel, n_tok=n_tok, n_rows=n_rows, top_k=top_k)
    full = lambda shape: pl.BlockSpec(shape, lambda i: (0,) * len(shape))
    return pl.pallas_call(
        kern, grid=(1,),
        in_specs=[full((rows, past)), full((IDX_HEADS, rows, IDX_DIM)), full((nk, IDX_DIM)), full((rows, IDX_HEADS)),
                  full(rexp.shape)],
        out_specs=[full((rows, past * KV_HEADS)), full((rows, nk * KV_HEADS))],
        out_shape=[jax.ShapeDtypeStruct((rows, past * KV_HEADS), BF16),
                   jax.ShapeDtypeStruct((rows, nk * KV_HEADS), BF16)],
        scratch_shapes=[pltpu.VMEM((rows, past), I32)],
        compiler_params=_cparams(("arbitrary",)), name="sample_select")(scores, qi, ki_new_bf, wi, rexp)


def _sample_attn_kernel(pt_ref, q_ref, z_ref, keep_ref, keep_new_ref, knew_ref, vnew_ref, *rest, n_pages, n_tok, page):
    k_refs, v_refs = rest[:n_pages], rest[n_pages:2 * n_pages]
    o_ref, m_ref, l_ref, acc_ref, s_ref = rest[2 * n_pages:]
    step = pl.program_id(1)
    rows = n_tok * ATT_HEADS
    cols = page * KV_HEADS
    rep = ATT_HEADS // KV_HEADS
    nl = cols // LANES

    @pl.when(step == 0)
    def _():
        m_ref[...] = jnp.full(m_ref.shape, MASKED, F32)
        l_ref[...] = jnp.zeros(l_ref.shape, F32)
        acc_ref[...] = jnp.zeros(acc_ref.shape, F32)

    q = q_ref[...]
    rr = lax.broadcasted_iota(I32, (rows, cols), 0)
    cc = lax.broadcasted_iota(I32, (rows, cols), 1)
    head_bias = jnp.where((cc % KV_HEADS) == ((rr % ATT_HEADS) // rep), 0.0, MASKED)

    def update(blocks):
        mx = None
        for i, (kp, _, keep) in enumerate(blocks):
            kb = (keep.astype(F32) - 1.0) * (-MASKED)
            kb_rows = jnp.concatenate(
                [jnp.broadcast_to(kb[t:t + 1, :], (ATT_HEADS, cols)) for t in range(n_tok)], axis=0)
            s = _nt_dot(q, kp) + (kb_rows + head_bias)
            s_ref[:, i * cols:(i + 1) * cols] = s
            part = s[:, :LANES]
            for c in range(1, nl):
                part = jnp.maximum(part, s[:, c * LANES:(c + 1) * LANES])
            mx = part if mx is None else jnp.maximum(mx, part)
        m_old = m_ref[...]
        m_new = jnp.maximum(m_old, jnp.max(mx, axis=1, keepdims=True))
        alpha = jnp.exp2(m_old - m_new)
        m_t = _tile_lanes(m_new, nl)
        lsum = jnp.zeros((rows, LANES), F32)
        pv = jnp.zeros((rows, HEAD_DIM), F32)
        for i, (_, vp, _) in enumerate(blocks):
            p = jnp.exp2(s_ref[:, i * cols:(i + 1) * cols] - m_t)
            lsum = lsum + _lane_tile_sum(p)
            pv = pv + jnp.dot(p.astype(BF16), vp, preferred_element_type=F32)
        l_ref[...] = alpha * l_ref[...] + lsum
        acc_ref[...] = alpha * acc_ref[...] + pv
        m_ref[...] = m_new

    update([(k_refs[k][...].astype(BF16), v_refs[k][...].astype(BF16), keep_ref[:, k * cols:(k + 1) * cols])
            for k in range(n_pages)])

    @pl.when(step == pl.num_programs(1) - 1)
    def _():
        update([(knew_ref[c * cols:(c + 1) * cols, :].astype(BF16), vnew_ref[c * cols:(c + 1) * cols, :].astype(BF16),
                 keep_new_ref[:, c * cols:(c + 1) * cols]) for c in range(knew_ref.shape[0] // cols)])
        lsum = jnp.sum(l_ref[...], axis=1, keepdims=True)
        o_ref[...] = (acc_ref[...] / lsum * z_ref[...].astype(F32)).astype(o_ref.dtype)


def _sample_attention(q_rows, z_rows, keep, keep_new, k_new, v_new, cache_k, cache_v, page_table, n_tok):
    bsz, n_pages_total = page_table.shape
    cols = cache_k.shape[1]
    page = cols // KV_HEADS
    g = PAGES_PER_STEP if n_pages_total % PAGES_PER_STEP == 0 else 1
    rows = n_tok * ATT_HEADS
    per_b = lambda shape: pl.BlockSpec((None,) + shape, lambda b, s, pt: (b,) + (0,) * len(shape))
    const = lambda shape: pl.BlockSpec(shape, lambda b, s, pt: (0,) * len(shape))
    page_spec = lambda k: pl.BlockSpec((None, cols, HEAD_DIM), lambda b, s, pt: (pt[b, s * g + k], 0, 0))
    grid_spec = pltpu.PrefetchScalarGridSpec(
        num_scalar_prefetch=1, grid=(bsz, n_pages_total // g),
        in_specs=[per_b((rows, HEAD_DIM)), per_b((rows, HEAD_DIM)),
                  pl.BlockSpec((None, n_tok, g * cols), lambda b, s, pt: (b, 0, s)),
                  per_b((n_tok, keep_new.shape[-1])), const(k_new.shape), const(v_new.shape)]
                 + [page_spec(k) for k in range(g)] * 2,
        out_specs=per_b((rows, HEAD_DIM)),
        scratch_shapes=[pltpu.VMEM((rows, LANES), F32), pltpu.VMEM((rows, LANES), F32),
                        pltpu.VMEM((rows, HEAD_DIM), F32), pltpu.VMEM((rows, g * cols), F32)])
    kern = functools.partial(_sample_attn_kernel, n_pages=g, n_tok=n_tok, page=page)
    return pl.pallas_call(
        kern, grid_spec=grid_spec,
        out_shape=jax.ShapeDtypeStruct((bsz, rows, HEAD_DIM), BF16),
        compiler_params=_cparams(("parallel", "arbitrary")), name="sample_attention",
    )(page_table, q_rows, z_rows, keep, keep_new, k_new, v_new, *([cache_k] * g), *([cache_v] * g))


def _ssd_kernel(*refs, t_rows, q_rows, has_state):
    if has_state:
        (xbc_ref, z_ref, dt_ref, conv0_ref, s0_ref, cw_ref, cb_ref, alog_ref, dskip_ref, ng_ref, e64_ref, e128_ref,
         y_ref, sfin_ref, cnew_ref, xp_ref, act_ref, dtp_ref, acst_ref, st_ref) = refs
    else:
        (xbc_ref, z_ref, dt_ref, conv0_ref, cw_ref, cb_ref, alog_ref, dskip_ref, ng_ref, e64_ref, e128_ref,
         y_ref, sfin_ref, cnew_ref, xp_ref, act_ref, dtp_ref, acst_ref, st_ref) = refs
        s0_ref = None
    c = pl.program_id(1)
    q = q_rows
    halo = SUBLANES
    n_heads = dt_ref.shape[-1]
    d_inner = n_heads * SSD_HEAD_DIM
    gw = d_inner // SSD_GROUPS
    hpg = n_heads // SSD_GROUPS
    conv_dim = xbc_ref.shape[-1]

    @pl.when(c == 0)
    def _():
        xp_ref[0:halo, :] = conv0_ref[...]
        if t_rows < q:
            xp_ref[halo:, :] = jnp.zeros((q, conv_dim), F32)
            dtp_ref[...] = jnp.zeros(dtp_ref.shape, F32)
        for g in range(SSD_GROUPS):
            if has_state:
                st_ref[g] = s0_ref[g * hpg:(g + 1) * hpg].reshape(gw, D_STATE).T
            else:
                st_ref[g] = jnp.zeros((D_STATE, gw), F32)

    xp_ref[halo:halo + t_rows, :] = xbc_ref[...]
    dtp_ref[0:t_rows, :] = dt_ref[...]

    cblk = 512
    for cbi in range(conv_dim // cblk):
        cs = slice(cbi * cblk, (cbi + 1) * cblk)
        acc = jnp.broadcast_to(cb_ref[:, cs], (q, cblk))
        for tap in range(CONV_W):
            lo = halo - (CONV_W - 1) + tap
            acc = acc + xp_ref[lo:lo + q, cs] * cw_ref[tap:tap + 1, cs]
        act_ref[:, cs] = _silu(acc)

    dt = dtp_ref[...]
    a = -jnp.exp(alog_ref[...])
    ri = lax.broadcasted_iota(I32, (q, q), 0)
    ci = lax.broadcasted_iota(I32, (q, q), 1)
    tril = ri >= ci
    a_cs = _dot_exact_lhs(jnp.where(tril, 1.0, 0.0).astype(BF16), dt * a)
    a_last = a_cs[q - 1:q, :]
    acst_ref[...] = a_cs.T
    dt_b = dt.astype(BF16)
    dte_b = jnp.exp(a_last - a_cs).astype(BF16)
    ea_hi, ea_mid, _ = _split3(jnp.exp(a_cs))
    pad = jnp.zeros((SUBLANES - 2, n_heads), F32)
    row_pieces = _split3(jnp.concatenate([jnp.exp(a_last), dskip_ref[...], pad], axis=0))
    acs_pieces = _split3(a_cs)
    mxu = functools.partial(jnp.dot, preferred_element_type=F32)
    lane = lax.broadcasted_iota(I32, (1, LANES), 1)
    lo_half = lane < SSD_HEAD_DIM

    def group_body(g, carry):
        xs = act_ref[:, pl.ds(pl.multiple_of(g * gw, gw), gw)]
        bm = act_ref[:, pl.ds(pl.multiple_of(d_inner + g * D_STATE, D_STATE), D_STATE)]
        cm = act_ref[:, pl.ds(pl.multiple_of(d_inner + SSD_GROUPS * D_STATE + g * D_STATE, D_STATE), D_STATE)]
        e64g = e64_ref[:, pl.ds(pl.multiple_of(g * gw, gw), gw)]
        dt_x, dte_x = mxu(dt_b, e64g), mxu(dte_b, e64g)
        ea_x = mxu(ea_hi, e64g) + mxu(ea_mid, e64g)
        rows_x = mxu(row_pieces[0], e64g) + mxu(row_pieces[1], e64g) + mxu(row_pieces[2], e64g)
        cdec_x, dskip_x = rows_x[0:1], rows_x[1:2]
        e128g = e128_ref[:, pl.ds(pl.multiple_of(g * hpg * LANES, hpg * LANES), hpg * LANES)]
        col_x = mxu(acs_pieces[0], e128g) + mxu(acs_pieces[1], e128g) + mxu(acs_pieces[2], e128g)
        xdt = xs * dt_x
        xdt_b = xdt.astype(BF16)
        bm_b, cm_b = bm.astype(BF16), cm.astype(BF16)
        cb = _nt_dot(cm_b, bm_b)
        st = st_ref[g]
        y_off = jnp.dot(cm_b, st.astype(BF16), preferred_element_type=F32) * ea_x
        y_parts = []
        for pr in range(hpg // 2):
            xpair = xdt_b[:, pr * LANES:(pr + 1) * LANES]
            halves = (jnp.where(lo_half, xpair, jnp.zeros_like(xpair)), jnp.where(lo_half, jnp.zeros_like(xpair), xpair))
            yp = jnp.zeros((q, LANES), F32)
            for s in range(2):
                hl = 2 * pr + s
                seg = col_x[:, hl * LANES:hl * LANES + q] - acst_ref[pl.ds(g * hpg + hl, 1), :]
                lmat = jnp.where(tril, jnp.exp(jnp.where(tril, seg, 0.0)), 0.0)
                yp = yp + jnp.dot((cb * lmat).astype(BF16), halves[s], preferred_element_type=F32)
            y_parts.append(yp)
        y = jnp.concatenate(y_parts, axis=1) + y_off + dskip_x * xs
        st_ref[g] = cdec_x * st + jnp.dot(bm.T.astype(BF16), (xdt * dte_x).astype(BF16), preferred_element_type=F32)
        gsl = pl.ds(pl.multiple_of(g * gw, gw), gw)
        yz = y[0:t_rows] * z_ref[:, gsl].astype(F32)
        ms = jnp.mean(yz * yz, axis=-1, keepdims=True)
        y_ref[:, gsl] = (yz * lax.rsqrt(ms + EPS) * ng_ref[:, gsl]).astype(y_ref.dtype)
        return carry

    lax.fori_loop(0, SSD_GROUPS, group_body, 0)

    @pl.when(c == pl.num_programs(1) - 1)
    def _():
        cnew_ref[...] = xp_ref[halo + t_rows - (CONV_W - 1):halo + t_rows, :]
        for g in range(SSD_GROUPS):
            sfin_ref[g * hpg:(g + 1) * hpg] = st_ref[g].T.reshape(hpg, SSD_HEAD_DIM, D_STATE)

    if t_rows == q:
        @pl.when(c < pl.num_programs(1) - 1)
        def _():
            xp_ref[0:halo, :] = xp_ref[q:q + halo, :]


def _ssd(xbc, z, dt, conv0, s0, conv_w, conv_b, a_log, d_skip, norm_g, bsz, seq):
    conv_dim = xbc.shape[-1]
    n_heads = dt.shape[-1]
    d_inner = n_heads * SSD_HEAD_DIM
    t_rows = min(CHUNK, seq)
    n_chunks = seq // t_rows
    q_rows = CHUNK if t_rows == CHUNK else _round_up(t_rows, BF16_ROWS)
    has_state = s0 is not None
    hpg = n_heads // SSD_GROUPS
    e64 = (jnp.arange(d_inner)[None, :] // SSD_HEAD_DIM == jnp.arange(n_heads)[:, None]).astype(BF16)
    e128 = (jnp.arange(n_heads * LANES)[None, :] // LANES == jnp.arange(n_heads)[:, None]).astype(BF16)
    x3 = lambda a: a.reshape(bsz, seq, a.shape[-1])
    rows = lambda w: pl.BlockSpec((None, t_rows, w), lambda b, c: (b, c, 0))
    const = lambda shape: pl.BlockSpec(shape, lambda b, c: (0,) * len(shape))
    state_spec = pl.BlockSpec((None, n_heads, SSD_HEAD_DIM, D_STATE), lambda b, c: (b, 0, 0, 0))
    args = [x3(xbc), x3(z), x3(dt), conv0]
    in_specs = [rows(conv_dim), rows(d_inner), rows(n_heads),
                pl.BlockSpec((None, SUBLANES, conv_dim), lambda b, c: (b, 0, 0))]
    if has_state:
        args.append(s0)
        in_specs.append(state_spec)
    args += [conv_w, conv_b.reshape(1, conv_dim), a_log.reshape(1, n_heads), d_skip.reshape(1, n_heads),
             norm_g.reshape(1, d_inner), e64, e128]
    in_specs += [const((CONV_W, conv_dim)), const((1, conv_dim)), const((1, n_heads)), const((1, n_heads)),
                 const((1, d_inner)), const(e64.shape), const(e128.shape)]
    kern = functools.partial(_ssd_kernel, t_rows=t_rows, q_rows=q_rows, has_state=has_state)
    y, sfin, cnew = pl.pallas_call(
        kern, grid=(bsz, n_chunks), in_specs=in_specs,
        out_specs=[rows(d_inner), state_spec,
                   pl.BlockSpec((None, CONV_W - 1, conv_dim), lambda b, c: (b, 0, 0))],
        out_shape=[jax.ShapeDtypeStruct((bsz, seq, d_inner), BF16),
                   jax.ShapeDtypeStruct((bsz, n_heads, SSD_HEAD_DIM, D_STATE), F32),
                   jax.ShapeDtypeStruct((bsz, CONV_W - 1, conv_dim), F32)],
        scratch_shapes=[pltpu.VMEM((SUBLANES + q_rows, conv_dim), F32),
                        pltpu.VMEM((q_rows, conv_dim), F32),
                        pltpu.VMEM((q_rows, n_heads), F32),
                        pltpu.VMEM((n_heads, q_rows), F32),
                        pltpu.VMEM((SSD_GROUPS, D_STATE, hpg * SSD_HEAD_DIM), F32)],
        compiler_params=_cparams(("parallel", "arbitrary")), name="ssd")(*args)
    return y.reshape(bsz * seq, d_inner), sfin, cnew


def _merge_kernel(ya_ref, ys_ref, wa_ref, *rest):
    ws_refs, (ga_ref, gs_ref, o_ref) = rest[:-3], rest[-3:]
    kb = wa_ref.shape[0]
    ya = jnp.dot(ya_ref[...], wa_ref[...], preferred_element_type=F32)
    yb = jnp.dot(ys_ref[:, :kb], ws_refs[0][...], preferred_element_type=F32)
    for i in range(1, len(ws_refs)):
        yb = yb + jnp.dot(ys_ref[:, i * kb:(i + 1) * kb], ws_refs[i][...], preferred_element_type=F32)
    o_ref[...] = (ga_ref[...].astype(F32) * ya + gs_ref[...].astype(F32) * yb).astype(o_ref.dtype)


def _merge(y_att, y_ssd, wb, gates, tm):
    m, att_w = y_att.shape
    d_inner = y_ssd.shape[1]
    d = wb.shape[1]
    tn = 512
    nj = d // tn
    n_ssd = d_inner // att_w
    w_blk = lambda r: pl.BlockSpec((att_w, tn), lambda i, j: (r, j))
    return pl.pallas_call(
        _merge_kernel, grid=(m // tm, nj),
        in_specs=[pl.BlockSpec((tm, att_w), lambda i, j: (i, 0)), pl.BlockSpec((tm, d_inner), lambda i, j: (i, 0))]
                 + [w_blk(r) for r in range(1 + n_ssd)]
                 + [pl.BlockSpec((tm, tn), lambda i, j: (i, j)), pl.BlockSpec((tm, tn), lambda i, j: (i, j + nj))],
        out_specs=pl.BlockSpec((tm, tn), lambda i, j: (i, j)),
        out_shape=jax.ShapeDtypeStruct((m, d), BF16),
        compiler_params=_cparams(("parallel", "parallel")), name="merge",
    )(y_att, y_ssd, *([wb] * (1 + n_ssd)), gates, gates)


def _out_kernel(m_ref, w_ref, x_ref, g_ref, o_ref):
    h = x_ref[...] + jnp.dot(m_ref[...], w_ref[...], preferred_element_type=F32)
    ms = jnp.mean(h * h, axis=-1, keepdims=True)
    o_ref[...] = h * lax.rsqrt(ms + EPS) * g_ref[...]


def _out_proj(merged, w_out, x, final_g, tm):
    m, d = x.shape
    return pl.pallas_call(
        _out_kernel, grid=(m // tm,),
        in_specs=[pl.BlockSpec((tm, d), lambda i: (i, 0)), pl.BlockSpec((d, d), lambda i: (0, 0)),
                  pl.BlockSpec((tm, d), lambda i: (i, 0)), pl.BlockSpec((1, d), lambda i: (0, 0))],
        out_specs=pl.BlockSpec((tm, d), lambda i: (i, 0)),
        out_shape=jax.ShapeDtypeStruct((m, d), F32),
        compiler_params=_cparams(("parallel",)), name="out_proj")(merged, w_out, x, final_g.reshape(1, d))


def _round_up(x, n):
    return (x + n - 1) // n * n


def kernel(x_prompt, x_sample, cache_k, cache_v, cache_idx_k, state_ssm, state_conv, page_table, norm_g, w_in, conv_w,
           conv_b, dt_bias, a_log, d_skip, ssd_norm_g, idx_ln_w, idx_ln_b, w_branch, w_out, final_norm_g):
    assert w_in.shape[0] == 1, "single-layer trunk"
    bp, seq, d = x_prompt.shape
    bs, n_tok, _ = x_sample.shape
    n_phys, page = cache_k.shape[1], cache_k.shape[2]
    past = page_table.shape[1] * page
    att_w, kv_w = ATT_HEADS * HEAD_DIM, KV_HEADS * HEAD_DIM
    d_inner = 2 * d
    conv_dim = d_inner + 2 * SSD_GROUPS * D_STATE
    top_k_p = max(1, min(TOPK_MAX, seq // 4))
    top_k_s = max(1, min(TOPK_MAX, (past + n_tok) // 4))

    wb = w_branch[0].astype(BF16)
    w_out_b = w_out[0].astype(BF16)
    w_in0 = jnp.swapaxes(w_in[0], 0, 1)

    mp = bp * seq
    tm_p = min(1024, seq)
    xp = x_prompt.reshape(mp, d)
    xn_p = _rmsnorm_bf16(xp, norm_g[0], min(512, seq))
    pp = _projections(xn_p, w_in0, jnp.arange(seq, dtype=I32), seq // tm_p, tm_p, idx_ln_w[0], idx_ln_b[0], dt_bias[0])
    y_att_p = _prompt_attention(pp, bp, seq, top_k_p)
    y_ssd_p, ssm_p, conv_p = _ssd(pp["xbc"], pp["z_ssd"], pp["dt"], jnp.zeros((bp, SUBLANES, conv_dim), F32), None,
                                  conv_w[0], conv_b[0], a_log[0], d_skip[0], ssd_norm_g[0], bp, seq)
    merged_p = _merge(y_att_p, y_ssd_p, wb, pp["gates"], min(512, seq))
    y_p = _out_proj(merged_p, w_out_b, xp, final_norm_g, min(512, seq))

    ms_rows = bs * n_tok
    rows_pad = _round_up(ms_rows, LANES)
    xs = x_sample.reshape(ms_rows, d)
    xs_pad = jnp.pad(xs, ((0, rows_pad - ms_rows), (0, 0))) if rows_pad != ms_rows else xs
    pos_s = past + (jnp.arange(rows_pad, dtype=I32) % n_tok)
    xn_s = _rmsnorm_bf16(xs_pad, norm_g[0], rows_pad)
    ps = _projections(xn_s, w_in0, pos_s, 1, rows_pad, idx_ln_w[0], idx_ln_b[0], dt_bias[0])
    real = lambda a: a[:ms_rows]

    qi_rows = jnp.transpose(ps["qi"][:, :ms_rows], (1, 0, 2)).reshape(bs, n_tok * IDX_HEADS, IDX_DIM)
    wi_col = real(ps["wi"]).reshape(bs, n_tok * IDX_HEADS, 1)
    scores = _sample_scores(qi_rows, wi_col, jnp.swapaxes(cache_idx_k[0], 1, 2), page_table, n_tok)
    scores = scores.reshape(ms_rows, past)
    if rows_pad != ms_rows:
        scores = jnp.pad(scores, ((0, rows_pad - ms_rows), (0, 0)))
    keep, keep_new = _sample_select(scores, ps["qi"], ps["ki_bf"], ps["wi"], n_tok, ms_rows, top_k_s)
    keep = real(keep).reshape(bs, n_tok, past * KV_HEADS)
    keep_new = real(keep_new).reshape(bs, n_tok, rows_pad * KV_HEADS)
    q_rows = jnp.transpose(ps["q"][:, :ms_rows], (1, 0, 2)).reshape(bs, n_tok * ATT_HEADS, HEAD_DIM)
    z_rows = real(ps["z_att"]).reshape(bs, n_tok * ATT_HEADS, HEAD_DIM)
    k_new = ps["k"].reshape(rows_pad * KV_HEADS, HEAD_DIM)
    v_new = ps["v"].reshape(rows_pad * KV_HEADS, HEAD_DIM)
    ck = cache_k[0].reshape(n_phys, page * KV_HEADS, HEAD_DIM)
    cv = cache_v[0].reshape(n_phys, page * KV_HEADS, HEAD_DIM)
    y_att_s = _sample_attention(q_rows, z_rows, keep, keep_new, k_new, v_new, ck, cv, page_table, n_tok)
    y_att_s = y_att_s.reshape(ms_rows, att_w)

    conv0_s = jnp.pad(state_conv[0], ((0, 0), (SUBLANES - (CONV_W - 1), 0), (0, 0)))
    y_ssd_s, ssm_s, conv_s = _ssd(real(ps["xbc"]), real(ps["z_ssd"]), real(ps["dt"]), conv0_s, state_ssm[0],
                                  conv_w[0], conv_b[0], a_log[0], d_skip[0], ssd_norm_g[0], bs, n_tok)
    if rows_pad != ms_rows:
        padr = lambda a: jnp.pad(a, ((0, rows_pad - ms_rows), (0, 0)))
        y_att_s, y_ssd_s = padr(y_att_s), padr(y_ssd_s)
    merged_s = _merge(y_att_s, y_ssd_s, wb, ps["gates"], rows_pad)
    y_s = real(_out_proj(merged_s, w_out_b, xs_pad, final_norm_g, rows_pad))

    kv5 = lambda a, b, t: a.reshape(1, b, t, KV_HEADS, HEAD_DIM)
    return (y_p.reshape(bp, seq, d), y_s.reshape(bs, n_tok, d),
            kv5(pp["k"], bp, seq), kv5(pp["v"], bp, seq), pp["ki"].reshape(1, bp, seq, IDX_DIM),
            ssm_p[None], conv_p[None],
            kv5(real(ps["k"]), bs, n_tok), kv5(real(ps["v"]), bs, n_tok), real(ps["ki"]).reshape(1, bs, n_tok, IDX_DIM),
            ssm_s[None], conv_s[None])
```

```python
import functools

import numpy as np
import jax
import jax.numpy as jnp
from jax import lax
from jax.experimental import pallas as pl
from jax.experimental.pallas import tpu as pltpu

F32, BF16, I32 = jnp.float32, jnp.bfloat16, jnp.int32

ATT_HEADS = 16
KV_HEADS = 4
HEAD_DIM = 128
IDX_HEADS = 16
IDX_DIM = 64
TOPK_MAX = 256
ROPE_THETA = 10000.0
SSD_HEAD_DIM = 64
SSD_GROUPS = 8
D_STATE = 128
CONV_W = 4
CHUNK = 128
EPS = 1e-6

LANES = 128
SUBLANES = 8
BF16_ROWS = 16
VMEM_LIMIT = 56 * 1024 * 1024
INT_MIN = -2 ** 31
MASKED = -1e30
LOG2E = 1.4426950408889634
PAGES_PER_STEP = 16
SCORE_PAGES_PER_STEP = 32


def _cparams(sem):
    return pltpu.CompilerParams(dimension_semantics=sem, vmem_limit_bytes=VMEM_LIMIT)


def _nt_dot(a, b):
    return lax.dot_general(a, b, (((1,), (1,)), ((), ())), preferred_element_type=F32)


def _tile_lanes(x, n):
    return x if n == 1 else jnp.concatenate([x] * n, axis=1)


def _split3(x):
    hi = x.astype(BF16)
    r1 = x - hi.astype(F32)
    mid = r1.astype(BF16)
    lo = (r1 - mid.astype(F32)).astype(BF16)
    return hi, mid, lo


def _dot_exact_rhs(x, e):
    hi, mid, lo = _split3(x)
    d = functools.partial(jnp.dot, preferred_element_type=F32)
    return d(hi, e) + d(mid, e) + d(lo, e)


def _dot_exact_lhs(e, x):
    hi, mid, lo = _split3(x)
    d = functools.partial(jnp.dot, preferred_element_type=F32)
    return d(e, hi) + d(e, mid) + d(e, lo)


def _sortable(score):
    score = jnp.where(score == 0.0, 0.0, score)
    bits = pltpu.bitcast(score, I32)
    return jnp.where(bits < 0, bits ^ jnp.int32(0x7FFFFFFF), bits)


def _norm_kernel(x_ref, g_ref, o_ref):
    x = x_ref[...]
    ms = jnp.mean(x * x, axis=-1, keepdims=True)
    o_ref[...] = (x * lax.rsqrt(ms + EPS) * g_ref[...]).astype(o_ref.dtype)


def _rmsnorm_bf16(x, g, tm):
    m, d = x.shape
    return pl.pallas_call(
        _norm_kernel, grid=(m // tm,),
        in_specs=[pl.BlockSpec((tm, d), lambda i: (i, 0)), pl.BlockSpec((1, d), lambda i: (0, 0))],
        out_specs=pl.BlockSpec((tm, d), lambda i: (i, 0)),
        out_shape=jax.ShapeDtypeStruct((m, d), BF16),
        compiler_params=_cparams(("parallel",)), name="rmsnorm")(x, g.reshape(1, d))


def _mm_kernel(x_ref, w_ref, *rest, epilogue, n_aux):
    acc = _nt_dot(x_ref[...], w_ref[...].astype(BF16))
    epilogue(acc, rest[:n_aux], rest[n_aux:])


def _matmul(x, wt, row0, n, epilogue, aux, aux_specs, out_shapes, out_specs, tm, tn, name):
    m, k = x.shape
    kern = functools.partial(_mm_kernel, epilogue=epilogue, n_aux=len(aux))
    return pl.pallas_call(
        kern, grid=(m // tm, n // tn),
        in_specs=[pl.BlockSpec((tm, k), lambda i, j: (i, 0)),
                  pl.BlockSpec((pl.Element(tn), pl.Element(k)),
                               lambda i, j: (pl.multiple_of(row0 + j * tn, SUBLANES), 0))] + list(aux_specs),
        out_specs=out_specs, out_shape=out_shapes,
        compiler_params=_cparams(("parallel", "parallel")), name=name)(x, wt, *aux)


def _rope_half(x, cos, sin_signed, half):
    if 2 * half == LANES:
        partner = pltpu.roll(x, half, 1)
    else:
        partner = jnp.concatenate([x[:, half:], x[:, :half]], axis=1)
    return x * cos + partner * sin_signed


def _epi_rope128(acc, aux, outs, post_scale=None, head_major=False):
    cos, sin = aux[0][...], aux[1][...]
    for c in range(acc.shape[1] // HEAD_DIM):
        sl = slice(c * HEAD_DIM, (c + 1) * HEAD_DIM)
        r = _rope_half(acc[:, sl], cos, sin, HEAD_DIM // 2)
        if post_scale is not None:
            r = r * post_scale
        for o in outs:
            if head_major:
                o[c] = r.astype(o.dtype)
            else:
                o[:, sl] = r.astype(o.dtype)


def _epi_store(acc, aux, outs, fn=None):
    val = acc if fn is None else fn(acc)
    for o in outs:
        o[...] = val.astype(o.dtype)


def _epi_qi(acc, aux, outs):
    cos, sin = aux[0][...], aux[1][...]
    for hh in range(acc.shape[1] // IDX_DIM):
        x = acc[:, hh * IDX_DIM:(hh + 1) * IDX_DIM]
        outs[0][hh] = _rope_half(x, cos, sin, IDX_DIM // 2).astype(outs[0].dtype)


def _epi_kiwi(acc, aux, outs, idx_scale):
    lnw, lnb, cos, sin = (a[...] for a in aux)
    ki = acc[:, :IDX_DIM]
    mu = jnp.mean(ki, axis=-1, keepdims=True)
    kc = ki - mu
    y = kc * lax.rsqrt(jnp.mean(kc * kc, axis=-1, keepdims=True) + EPS) * lnw + lnb
    r = _rope_half(y, cos, sin, IDX_DIM // 2)
    outs[0][...] = r
    outs[1][...] = r.astype(BF16)
    outs[2][...] = acc[:, IDX_DIM:] * idx_scale


def _epi_softplus(acc, aux, outs):
    x = acc + aux[0][...]
    outs[0][...] = jnp.maximum(x, 0.0) + jnp.log1p(jnp.exp(-jnp.abs(x)))


def _silu(x):
    return x * jax.nn.sigmoid(x)


def _rope_tables(pos, d):
    inv = ROPE_THETA ** (-jnp.arange(0, d, 2, dtype=F32) / d)
    ang = pos.astype(F32)[:, None] * inv[None, :]
    cos, sin = jnp.cos(ang), jnp.sin(ang)
    return jnp.concatenate([cos, cos], axis=-1), jnp.concatenate([-sin, sin], axis=-1)


def _projections(xn, wt, pos_rows, n_pos_blocks, tm, idx_ln_w, idx_ln_b, dt_bias):
    m, d = xn.shape
    att_w, kv_w = ATT_HEADS * HEAD_DIM, KV_HEADS * HEAD_DIM
    d_inner = 2 * d
    conv_dim = d_inner + 2 * SSD_GROUPS * D_STATE
    ssd_heads = d_inner // SSD_HEAD_DIM
    splits = (att_w, kv_w, kv_w, att_w, IDX_HEADS * IDX_DIM, IDX_DIM, IDX_HEADS, d_inner, conv_dim,
              ssd_heads, d, d)
    assert sum(splits) == wt.shape[0]
    off = np.concatenate([[0], np.cumsum(splits)]).astype(int)
    assert all(o % BF16_ROWS == 0 for o in off), "segment rows of the weight must start on packed-row boundaries"
    seg = lambda a, b: (wt, int(off[a]), int(off[b] - off[a]))

    cos128, sin128 = _rope_tables(pos_rows, HEAD_DIM)
    cos64, sin64 = _rope_tables(pos_rows, IDX_DIM)
    pos_map = lambda i, j: (i % n_pos_blocks, 0)
    t128 = [pl.BlockSpec((tm, HEAD_DIM), pos_map)] * 2
    t64 = [pl.BlockSpec((tm, IDX_DIM), pos_map)] * 2
    tile = lambda tn: pl.BlockSpec((tm, tn), lambda i, j: (i, j))
    sds = lambda n, dt: jax.ShapeDtypeStruct((m, n), dt)
    tn = 512
    tw = 1024

    (q,) = _matmul(xn, *seg(0, 1), functools.partial(_epi_rope128, post_scale=HEAD_DIM ** -0.5 * LOG2E, head_major=True),
                   [cos128, sin128], t128, [jax.ShapeDtypeStruct((ATT_HEADS, m, HEAD_DIM), BF16)],
                   [pl.BlockSpec((tw // HEAD_DIM, tm, HEAD_DIM), lambda i, j: (j, i, 0))], tm, tw, "proj_q")
    k, k_bf = _matmul(xn, *seg(1, 2), _epi_rope128, [cos128, sin128], t128,
                      [sds(kv_w, F32), sds(kv_w, BF16)], [tile(tn)] * 2, tm, tn, "proj_k")
    v, v_bf = _matmul(xn, *seg(2, 3), _epi_store, [], [], [sds(kv_w, F32), sds(kv_w, BF16)],
                      [tile(tn)] * 2, tm, tn, "proj_v")
    (z_att,) = _matmul(xn, *seg(3, 4), functools.partial(_epi_store, fn=_silu), [], [],
                       [sds(att_w, BF16)], [tile(tw)], tm, tw, "proj_zatt")
    qi_tn = 4 * IDX_DIM
    (qi,) = _matmul(xn, *seg(4, 5), _epi_qi, [cos64, sin64], t64,
                    [jax.ShapeDtypeStruct((IDX_HEADS, m, IDX_DIM), BF16)],
                    [pl.BlockSpec((4, tm, IDX_DIM), lambda i, j: (j, i, 0))], tm, qi_tn, "proj_qi")
    kiwi_n = IDX_DIM + IDX_HEADS
    idx_scale = IDX_HEADS ** -0.5 * IDX_DIM ** -0.5
    row1 = lambda n: pl.BlockSpec((1, n), lambda i, j: (0, 0))
    ki, ki_bf, wi = _matmul(
        xn, *seg(5, 7), functools.partial(_epi_kiwi, idx_scale=idx_scale),
        [idx_ln_w.reshape(1, IDX_DIM), idx_ln_b.reshape(1, IDX_DIM), cos64, sin64],
        [row1(IDX_DIM), row1(IDX_DIM)] + t64,
        [sds(IDX_DIM, F32), sds(IDX_DIM, BF16), sds(IDX_HEADS, F32)],
        [pl.BlockSpec((tm, IDX_DIM), lambda i, j: (i, 0))] * 2 + [pl.BlockSpec((tm, IDX_HEADS), lambda i, j: (i, 0))],
        tm, kiwi_n, "proj_kiwi")
    (z_ssd,) = _matmul(xn, *seg(7, 8), functools.partial(_epi_store, fn=_silu), [], [],
                       [sds(d_inner, BF16)], [tile(tw)], tm, tw, "proj_zssd")
    (xbc,) = _matmul(xn, *seg(8, 9), _epi_store, [], [], [sds(conv_dim, F32)], [tile(tw)], tm, tw, "proj_xbc")
    (dt,) = _matmul(xn, *seg(9, 10), _epi_softplus, [dt_bias.reshape(1, ssd_heads)], [row1(ssd_heads)],
                    [sds(ssd_heads, F32)], [pl.BlockSpec((tm, ssd_heads), lambda i, j: (i, 0))],
                    tm, ssd_heads, "proj_dt")
    (gates,) = _matmul(xn, *seg(10, 12), functools.partial(_epi_store, fn=jax.nn.sigmoid), [], [],
                       [sds(2 * d, BF16)], [tile(tw)], tm, tw, "proj_gates")
    return dict(q=q, k=k, k_bf=k_bf, v=v, v_bf=v_bf, z_att=z_att, qi=qi, ki=ki, ki_bf=ki_bf, wi=wi,
                z_ssd=z_ssd, xbc=xbc, dt=dt, gates=gates)


def _kth_largest_key(count_ge, shape, top_k):
    kf = jnp.float32(top_k)
    zero = jnp.zeros(shape, I32)
    t0 = jnp.where(count_ge(zero) >= kf, zero, jnp.full(shape, INT_MIN, I32))

    def body(i, t):
        cand = t | lax.shift_left(jnp.int32(1), jnp.int32(30) - i)
        return jnp.where(count_ge(cand) >= kf, cand, t)

    t = lax.fori_loop(0, 31, body, t0)
    return jnp.maximum(t, jnp.int32(INT_MIN + 1))


def _lane_tile_sum(x):
    part = x[:, :LANES]
    for c in range(1, x.shape[1] // LANES):
        part = part + x[:, c * LANES:(c + 1) * LANES]
    return part


def _attn_kernel(q_ref, qi_ref, wi_ref, z_ref, k_ref, v_ref, ki_ref, o_ref,
                 skey_ref, wexp_ref, bias_ref, s_ref, m_ref, l_ref, acc_ref, *, tq, tk, tkc, top_k):
    qt = pl.program_id(1)
    n_kc = ((qt + 1) * tq + tk - 1) // tk
    nl = tk // LANES
    row_pos = qt * tq + lax.broadcasted_iota(I32, (tq, 1), 0)

    wi = wi_ref[...]
    for h in range(IDX_HEADS):
        wexp_ref[h] = jnp.broadcast_to(wi[:, h:h + 1], (tq, LANES))

    def score_chunk(kc, carry):
        off = pl.multiple_of(kc * tk, tk)
        ki = ki_ref[pl.ds(off, tk), :]
        d = _nt_dot(qi_ref[...].reshape(IDX_HEADS * tq, IDX_DIM), ki)
        r = jnp.maximum(d, 0.0) * _tile_lanes(wexp_ref[...].reshape(IDX_HEADS * tq, LANES), nl)
        sc = r[0:tq]
        for h in range(1, IDX_HEADS):
            sc = sc + r[h * tq:(h + 1) * tq]
        kpos = off + lax.broadcasted_iota(I32, (1, tk), 1)
        skey_ref[:, pl.ds(off, tk)] = jnp.where(kpos <= row_pos, _sortable(sc), jnp.int32(INT_MIN))
        return carry

    lax.fori_loop(0, n_kc, score_chunk, 0)

    def count_ge(cand):
        cand_t = _tile_lanes(cand, nl)

        def body(kc, cnt):
            off = pl.multiple_of(kc * tk, tk)
            ge = jnp.where(skey_ref[:, pl.ds(off, tk)] >= cand_t, 1.0, 0.0)
            return cnt + _lane_tile_sum(ge)

        cnt = lax.fori_loop(0, n_kc, body, jnp.zeros((tq, LANES), F32))
        return jnp.sum(cnt, axis=1, keepdims=True)

    thr = _tile_lanes(_kth_largest_key(count_ge, (tq, LANES), top_k), nl)

    def bias_chunk(kc, carry):
        off = pl.multiple_of(kc * tk, tk)
        bias_ref[:, pl.ds(off, tk)] = jnp.where(skey_ref[:, pl.ds(off, tk)] >= thr, 0.0, MASKED)
        return carry

    lax.fori_loop(0, n_kc, bias_chunk, 0)

    n_kcc = ((qt + 1) * tq + tkc - 1) // tkc
    nlc = tkc // LANES

    def bias_tail(kc, carry):
        bias_ref[:, pl.ds(pl.multiple_of(kc * tk, tk), tk)] = jnp.full((tq, tk), MASKED, F32)
        return carry

    lax.fori_loop(n_kc, n_kcc * (tkc // tk), bias_tail, 0)
    rep = ATT_HEADS // KV_HEADS

    rows = rep * tq

    def logits_chunk(g, ks):
        slot = g % 2
        gs = slice(g * HEAD_DIM, (g + 1) * HEAD_DIM)
        qg = q_ref[g * rep:(g + 1) * rep].reshape(rows, HEAD_DIM)
        bias = bias_ref[:, ks]
        s = jnp.concatenate([bias] * rep, axis=0) + _nt_dot(qg, k_ref[ks, gs])
        s_ref[slot, :, ks] = s
        part = s[:, :LANES]
        for c in range(1, nlc):
            part = jnp.maximum(part, s[:, c * LANES:(c + 1) * LANES])
        m_ref[slot] = jnp.maximum(m_ref[slot], part)

    def pv_chunk(g, ks):
        slot = g % 2
        gs = slice(g * HEAD_DIM, (g + 1) * HEAD_DIM)
        p = jnp.exp2(s_ref[slot, :, ks] - _tile_lanes(m_ref[slot], nlc))
        l_ref[...] += _lane_tile_sum(p)
        acc_ref[...] += jnp.dot(p.astype(BF16), v_ref[ks, gs], preferred_element_type=F32)

    for stage in range(KV_HEADS + 1):
        g_logits = stage if stage < KV_HEADS else None
        g_pv = stage - 1 if stage > 0 else None
        if g_logits is not None:
            m_ref[g_logits % 2] = jnp.full((rows, LANES), MASKED, F32)
        if g_pv is not None:
            slot = g_pv % 2
            m_ref[slot] = jnp.broadcast_to(jnp.max(m_ref[slot], axis=1, keepdims=True), (rows, LANES))
            l_ref[...] = jnp.zeros((rows, LANES), F32)
            acc_ref[...] = jnp.zeros((rows, HEAD_DIM), F32)

        def stage_chunk(kc, carry, g_logits=g_logits, g_pv=g_pv):
            ks = pl.ds(pl.multiple_of(kc * tkc, tkc), tkc)
            if g_logits is not None:
                logits_chunk(g_logits, ks)
            if g_pv is not None:
                pv_chunk(g_pv, ks)
            return carry

        lax.fori_loop(0, n_kcc, stage_chunk, 0)
        if g_pv is not None:
            o = acc_ref[...] / jnp.sum(l_ref[...], axis=1, keepdims=True)
            for r in range(rep):
                hs = slice((g_pv * rep + r) * HEAD_DIM, (g_pv * rep + r + 1) * HEAD_DIM)
                o_ref[:, hs] = (o[r * tq:(r + 1) * tq] * z_ref[:, hs].astype(F32)).astype(o_ref.dtype)


def _prompt_attention(p, batch, seq, top_k):
    m = batch * seq
    tq = min(128, seq)
    tk = min(512, seq)
    nq = seq // tq
    att_w, kv_w = ATT_HEADS * HEAD_DIM, KV_HEADS * HEAD_DIM
    rep = ATT_HEADS // KV_HEADS
    rows = lambda w: pl.BlockSpec((tq, w), lambda b, t: (b * nq + t, 0))
    whole = lambda w: pl.BlockSpec((seq, w), lambda b, t: (b, 0))
    tkc = min(1024, seq)
    kern = functools.partial(_attn_kernel, tq=tq, tk=tk, tkc=tkc, top_k=top_k)
    return pl.pallas_call(
        kern, grid=(batch, nq),
        in_specs=[pl.BlockSpec((ATT_HEADS, tq, HEAD_DIM), lambda b, t: (0, b * nq + t, 0)),
                  pl.BlockSpec((IDX_HEADS, tq, IDX_DIM), lambda b, t: (0, b * nq + t, 0)),
                  rows(IDX_HEADS), rows(att_w), whole(kv_w), whole(kv_w), whole(IDX_DIM)],
        out_specs=rows(att_w),
        out_shape=jax.ShapeDtypeStruct((m, att_w), BF16),
        scratch_shapes=[pltpu.VMEM((tq, seq), I32),
                        pltpu.VMEM((IDX_HEADS, tq, LANES), F32),
                        pltpu.VMEM((tq, seq), F32),
                        pltpu.VMEM((2, rep * tq, seq), F32),
                        pltpu.VMEM((2, rep * tq, LANES), F32),
                        pltpu.VMEM((rep * tq, LANES), F32),
                        pltpu.VMEM((rep * tq, HEAD_DIM), F32)],
        compiler_params=_cparams(("parallel", "arbitrary")), name="prompt_attention",
    )(p["q"], p["qi"], p["wi"], p["z_att"], p["k_bf"], p["v_bf"], p["ki_bf"])


def _sample_score_kernel(pt_ref, qi_ref, wi_ref, *rest, n_pages, n_tok):
    page_refs, o_ref = rest[:n_pages], rest[n_pages]
    qi = qi_ref[...]
    w = jnp.broadcast_to(wi_ref[...], (qi.shape[0], LANES))
    for k in range(n_pages):
        kit = page_refs[k][...].astype(BF16)
        page = kit.shape[1]
        r = jnp.maximum(jnp.dot(qi, kit, preferred_element_type=F32), 0.0) * _tile_lanes(w, page // LANES)
        sc = jnp.sum(r.reshape(n_tok, IDX_HEADS, page), axis=1)
        o_ref[:, k * page:(k + 1) * page] = sc


def _sample_scores(qi_rows, wi_col, cache_idx_kt, page_table, n_tok):
    bsz, n_pages_total = page_table.shape
    page = cache_idx_kt.shape[2]
    g = SCORE_PAGES_PER_STEP if n_pages_total % SCORE_PAGES_PER_STEP == 0 else 1
    rows = n_tok * IDX_HEADS
    page_spec = lambda k: pl.BlockSpec((None, IDX_DIM, page), lambda b, s, pt: (pt[b, s * g + k], 0, 0))
    grid_spec = pltpu.PrefetchScalarGridSpec(
        num_scalar_prefetch=1, grid=(bsz, n_pages_total // g),
        in_specs=[pl.BlockSpec((None, rows, IDX_DIM), lambda b, s, pt: (b, 0, 0)),
                  pl.BlockSpec((None, rows, 1), lambda b, s, pt: (b, 0, 0))] + [page_spec(k) for k in range(g)],
        out_specs=pl.BlockSpec((None, n_tok, g * page), lambda b, s, pt: (b, 0, s)))
    kern = functools.partial(_sample_score_kernel, n_pages=g, n_tok=n_tok)
    return pl.pallas_call(
        kern, grid_spec=grid_spec,
        out_shape=jax.ShapeDtypeStruct((bsz, n_tok, n_pages_total * page), F32),
        compiler_params=_cparams(("parallel", "arbitrary")), name="sample_scores",
    )(page_table, qi_rows, wi_col, *([cache_idx_kt] * g))


def _sample_select_kernel(sc_ref, qi_ref, ki_ref, wi_ref, rexp_ref, keep_ref, keep_new_ref, skey_ref,
                          *, n_tok, n_rows, top_k):
    rows, past = sc_ref.shape
    nk = ki_ref.shape[0]
    wi = wi_ref[...]
    ki = ki_ref[...]
    sc_new = jnp.zeros((rows, nk), F32)
    for h in range(IDX_HEADS):
        w = jnp.broadcast_to(wi[:, h:h + 1], (rows, LANES))
        sc_new = sc_new + jnp.maximum(_nt_dot(qi_ref[h], ki), 0.0) * _tile_lanes(w, nk // LANES)
    r = lax.broadcasted_iota(I32, (rows, nk), 0)
    c = lax.broadcasted_iota(I32, (rows, nk), 1)
    ok = (r // n_tok == c // n_tok) & (c % n_tok <= r % n_tok) & (r < n_rows) & (c < n_rows)
    skey_new = jnp.where(ok, _sortable(sc_new), jnp.int32(INT_MIN))
    skey_ref[...] = _sortable(sc_ref[...])

    def count_ge(cand):
        ge = jnp.where(skey_ref[...] >= _tile_lanes(cand, past // LANES), 1.0, 0.0)
        ge_new = jnp.where(skey_new >= _tile_lanes(cand, nk // LANES), 1.0, 0.0)
        return jnp.sum(_lane_tile_sum(ge) + _lane_tile_sum(ge_new), axis=1, keepdims=True)

    thr = _kth_largest_key(count_ge, (rows, LANES), top_k)
    rexp = rexp_ref[...]
    xw = rexp.shape[1]

    def expand(key_tile):
        sel = jnp.where(key_tile >= thr, 1.0, 0.0).astype(BF16)
        return jnp.dot(sel, rexp, preferred_element_type=F32).astype(BF16)

    def past_tile(ct, carry):
        src = pl.ds(pl.multiple_of(ct * LANES, LANES), LANES)
        keep_ref[:, pl.ds(pl.multiple_of(ct * xw, xw), xw)] = expand(skey_ref[:, src])
        return carry

    lax.fori_loop(0, past // LANES, past_tile, 0)
    for ct in range(nk // LANES):
        keep_new_ref[:, ct * xw:(ct + 1) * xw] = expand(skey_new[:, ct * LANES:(ct + 1) * LANES])


def _sample_select(scores, qi, ki_new_bf, wi, n_tok, n_rows, top_k):
    rows, past = scores.shape
    nk = ki_new_bf.shape[0]
    rexp = (jnp.arange(LANES * KV_HEADS)[None, :] // KV_HEADS == jnp.arange(LANES)[:, None]).astype(BF16)
    kern = functools.partial(_sample_select_kernel, n_tok=n_tok, n_rows=n_rows, top_k=top_k)
    full = lambda shape: pl.BlockSpec(shape, lambda i: (0,) * len(shape))
    return pl.pallas_call(
        kern, grid=(1,),
        in_specs=[full((rows, past)), full((IDX_HEADS, rows, IDX_DIM)), full((nk, IDX_DIM)), full((rows, IDX_HEADS)),
                  full(rexp.shape)],
        out_specs=[full((rows, past * KV_HEADS)), full((rows, nk * KV_HEADS))],
        out_shape=[jax.ShapeDtypeStruct((rows, past * KV_HEADS), BF16),
                   jax.ShapeDtypeStruct((rows, nk * KV_HEADS), BF16)],
        scratch_shapes=[pltpu.VMEM((rows, past), I32)],
        compiler_params=_cparams(("arbitrary",)), name="sample_select")(scores, qi, ki_new_bf, wi, rexp)


def _sample_attn_kernel(pt_ref, q_ref, z_ref, keep_ref, keep_new_ref, knew_ref, vnew_ref, *rest, n_pages, n_tok, page):
    k_refs, v_refs = rest[:n_pages], rest[n_pages:2 * n_pages]
    o_ref, m_ref, l_ref, acc_ref, s_ref = rest[2 * n_pages:]
    step = pl.program_id(1)
    rows = n_tok * ATT_HEADS
    cols = page * KV_HEADS
    rep = ATT_HEADS // KV_HEADS
    nl = cols // LANES

    @pl.when(step == 0)
    def _():
        m_ref[...] = jnp.full(m_ref.shape, MASKED, F32)
        l_ref[...] = jnp.zeros(l_ref.shape, F32)
        acc_ref[...] = jnp.zeros(acc_ref.shape, F32)

    q = q_ref[...]
    rr = lax.broadcasted_iota(I32, (rows, cols), 0)
    cc = lax.broadcasted_iota(I32, (rows, cols), 1)
    head_bias = jnp.where((cc % KV_HEADS) == ((rr % ATT_HEADS) // rep), 0.0, MASKED)

    def update(blocks):
        mx = None
        for i, (kp, _, keep) in enumerate(blocks):
            kb = (keep.astype(F32) - 1.0) * (-MASKED)
            kb_rows = jnp.concatenate(
                [jnp.broadcast_to(kb[t:t + 1, :], (ATT_HEADS, cols)) for t in range(n_tok)], axis=0)
            s = _nt_dot(q, kp) + (kb_rows + head_bias)
            s_ref[:, i * cols:(i + 1) * cols] = s
            part = s[:, :LANES]
            for c in range(1, nl):
                part = jnp.maximum(part, s[:, c * LANES:(c + 1) * LANES])
            mx = part if mx is None else jnp.maximum(mx, part)
        m_old = m_ref[...]
        m_new = jnp.maximum(m_old, jnp.max(mx, axis=1, keepdims=True))
        alpha = jnp.exp2(m_old - m_new)
        m_t = _tile_lanes(m_new, nl)
        lsum = jnp.zeros((rows, LANES), F32)
        pv = jnp.zeros((rows, HEAD_DIM), F32)
        for i, (_, vp, _) in enumerate(blocks):
            p = jnp.exp2(s_ref[:, i * cols:(i + 1) * cols] - m_t)
            lsum = lsum + _lane_tile_sum(p)
            pv = pv + jnp.dot(p.astype(BF16), vp, preferred_element_type=F32)
        l_ref[...] = alpha * l_ref[...] + lsum
        acc_ref[...] = alpha * acc_ref[...] + pv
        m_ref[...] = m_new

    update([(k_refs[k][...].astype(BF16), v_refs[k][...].astype(BF16), keep_ref[:, k * cols:(k + 1) * cols])
            for k in range(n_pages)])

    @pl.when(step == pl.num_programs(1) - 1)
    def _():
        update([(knew_ref[c * cols:(c + 1) * cols, :].astype(BF16), vnew_ref[c * cols:(c + 1) * cols, :].astype(BF16),
                 keep_new_ref[:, c * cols:(c + 1) * cols]) for c in range(knew_ref.shape[0] // cols)])
        lsum = jnp.sum(l_ref[...], axis=1, keepdims=True)
        o_ref[...] = (acc_ref[...] / lsum * z_ref[...].astype(F32)).astype(o_ref.dtype)


def _sample_attention(q_rows, z_rows, keep, keep_new, k_new, v_new, cache_k, cache_v, page_table, n_tok):
    bsz, n_pages_total = page_table.shape
    cols = cache_k.shape[1]
    page = cols // KV_HEADS
    g = PAGES_PER_STEP if n_pages_total % PAGES_PER_STEP == 0 else 1
    rows = n_tok * ATT_HEADS
    per_b = lambda shape: pl.BlockSpec((None,) + shape, lambda b, s, pt: (b,) + (0,) * len(shape))
    const = lambda shape: pl.BlockSpec(shape, lambda b, s, pt: (0,) * len(shape))
    page_spec = lambda k: pl.BlockSpec((None, cols, HEAD_DIM), lambda b, s, pt: (pt[b, s * g + k], 0, 0))
    grid_spec = pltpu.PrefetchScalarGridSpec(
        num_scalar_prefetch=1, grid=(bsz, n_pages_total // g),
        in_specs=[per_b((rows, HEAD_DIM)), per_b((rows, HEAD_DIM)),
                  pl.BlockSpec((None, n_tok, g * cols), lambda b, s, pt: (b, 0, s)),
                  per_b((n_tok, keep_new.shape[-1])), const(k_new.shape), const(v_new.shape)]
                 + [page_spec(k) for k in range(g)] * 2,
        out_specs=per_b((rows, HEAD_DIM)),
        scratch_shapes=[pltpu.VMEM((rows, LANES), F32), pltpu.VMEM((rows, LANES), F32),
                        pltpu.VMEM((rows, HEAD_DIM), F32), pltpu.VMEM((rows, g * cols), F32)])
    kern = functools.partial(_sample_attn_kernel, n_pages=g, n_tok=n_tok, page=page)
    return pl.pallas_call(
        kern, grid_spec=grid_spec,
        out_shape=jax.ShapeDtypeStruct((bsz, rows, HEAD_DIM), BF16),
        compiler_params=_cparams(("parallel", "arbitrary")), name="sample_attention",
    )(page_table, q_rows, z_rows, keep, keep_new, k_new, v_new, *([cache_k] * g), *([cache_v] * g))


def _ssd_kernel(*refs, t_rows, q_rows, has_state):
    if has_state:
        (xbc_ref, z_ref, dt_ref, conv0_ref, s0_ref, cw_ref, cb_ref, alog_ref, dskip_ref, ng_ref, e64_ref,
         y_ref, sfin_ref, cnew_ref, xp_ref, act_ref, dtp_ref, acst_ref, acsc_ref, st_ref) = refs
    else:
        (xbc_ref, z_ref, dt_ref, conv0_ref, cw_ref, cb_ref, alog_ref, dskip_ref, ng_ref, e64_ref,
         y_ref, sfin_ref, cnew_ref, xp_ref, act_ref, dtp_ref, acst_ref, acsc_ref, st_ref) = refs
        s0_ref = None
    c = pl.program_id(1)
    q = q_rows
    halo = SUBLANES
    n_heads = dt_ref.shape[-1]
    d_inner = n_heads * SSD_HEAD_DIM
    gw = d_inner // SSD_GROUPS
    hpg = n_heads // SSD_GROUPS
    conv_dim = xbc_ref.shape[-1]

    @pl.when(c == 0)
    def _():
        xp_ref[0:halo, :] = conv0_ref[...]
        if t_rows < q:
            xp_ref[halo:, :] = jnp.zeros((q, conv_dim), F32)
            dtp_ref[...] = jnp.zeros(dtp_ref.shape, F32)
        for g in range(SSD_GROUPS):
            if has_state:
                st_ref[g] = s0_ref[g * hpg:(g + 1) * hpg].reshape(gw, D_STATE).T
            else:
                st_ref[g] = jnp.zeros((D_STATE, gw), F32)

    xp_ref[halo:halo + t_rows, :] = xbc_ref[...]
    dtp_ref[0:t_rows, :] = dt_ref[...]

    cblk = LANES
    for cbi in range(conv_dim // cblk):
        cs = slice(cbi * cblk, (cbi + 1) * cblk)
        acc = jnp.broadcast_to(cb_ref[:, cs], (q, cblk))
        for tap in range(CONV_W):
            lo = halo - (CONV_W - 1) + tap
            acc = acc + xp_ref[lo:lo + q, cs] * cw_ref[tap:tap + 1, cs]
        act_ref[:, cs] = _silu(acc)

    dt = dtp_ref[...]
    a = -jnp.exp(alog_ref[...])
    ri = lax.broadcasted_iota(I32, (q, q), 0)
    ci = lax.broadcasted_iota(I32, (q, q), 1)
    tril = ri >= ci
    a_cs = _dot_exact_lhs(jnp.where(tril, 1.0, 0.0).astype(BF16), dt * a)
    a_last = a_cs[q - 1:q, :]
    acst_ref[...] = a_cs.T
    for h in range(n_heads):
        acsc_ref[h] = jnp.broadcast_to(a_cs[:, h:h + 1], (q, LANES))
    dt_b = dt.astype(BF16)
    dte_b = jnp.exp(a_last - a_cs).astype(BF16)
    ea_hi, ea_mid, _ = _split3(jnp.exp(a_cs))
    pad = jnp.zeros((SUBLANES - 2, n_heads), F32)
    row_pieces = _split3(jnp.concatenate([jnp.exp(a_last), dskip_ref[...], pad], axis=0))
    mxu = functools.partial(jnp.dot, preferred_element_type=F32)
    lane = lax.broadcasted_iota(I32, (1, LANES), 1)
    lo_half = lane < SSD_HEAD_DIM

    def group_body(g, carry):
        xs = act_ref[:, pl.ds(pl.multiple_of(g * gw, gw), gw)]
        bm = act_ref[:, pl.ds(pl.multiple_of(d_inner + g * D_STATE, D_STATE), D_STATE)]
        cm = act_ref[:, pl.ds(pl.multiple_of(d_inner + SSD_GROUPS * D_STATE + g * D_STATE, D_STATE), D_STATE)]
        e64g = e64_ref[:, pl.ds(pl.multiple_of(g * gw, gw), gw)]
        dt_x, dte_x = mxu(dt_b, e64g), mxu(dte_b, e64g)
        ea_x = mxu(ea_hi, e64g) + mxu(ea_mid, e64g)
        rows_x = mxu(row_pieces[0], e64g) + mxu(row_pieces[1], e64g) + mxu(row_pieces[2], e64g)
        cdec_x, dskip_x = rows_x[0:1], rows_x[1:2]
        xdt = xs * dt_x
        xdt_b = xdt.astype(BF16)
        bm_b, cm_b = bm.astype(BF16), cm.astype(BF16)
        cb = _nt_dot(cm_b, bm_b)
        st = st_ref[g]
        y_off = jnp.dot(cm_b, st.astype(BF16), preferred_element_type=F32) * ea_x
        y_parts = []
        for pr in range(hpg // 2):
            xpair = xdt_b[:, pr * LANES:(pr + 1) * LANES]
            halves = (jnp.where(lo_half, xpair, jnp.zeros_like(xpair)), jnp.where(lo_half, jnp.zeros_like(xpair), xpair))
            yp = jnp.zeros((q, LANES), F32)
            for s in range(2):
                hl = 2 * pr + s
                seg = acsc_ref[g * hpg + hl][:, :q] - acst_ref[pl.ds(g * hpg + hl, 1), :]
                lmat = jnp.where(tril, jnp.exp(jnp.where(tril, seg, 0.0)), 0.0)
                yp = yp + jnp.dot((cb * lmat).astype(BF16), halves[s], preferred_element_type=F32)
            y_parts.append(yp)
        y = jnp.concatenate(y_parts, axis=1) + y_off + dskip_x * xs
        st_ref[g] = cdec_x * st + jnp.dot(bm.T.astype(BF16), (xdt * dte_x).astype(BF16), preferred_element_type=F32)
        gsl = pl.ds(pl.multiple_of(g * gw, gw), gw)
        yz = y[0:t_rows] * z_ref[:, gsl].astype(F32)
        ms = jnp.mean(yz * yz, axis=-1, keepdims=True)
        y_ref[:, gsl] = (yz * lax.rsqrt(ms + EPS) * ng_ref[:, gsl]).astype(y_ref.dtype)
        return carry

    lax.fori_loop(0, SSD_GROUPS, group_body, 0)

    @pl.when(c == pl.num_programs(1) - 1)
    def _():
        cnew_ref[...] = xp_ref[halo + t_rows - (CONV_W - 1):halo + t_rows, :]
        for g in range(SSD_GROUPS):
            sfin_ref[g * hpg:(g + 1) * hpg] = st_ref[g].T.reshape(hpg, SSD_HEAD_DIM, D_STATE)

    if t_rows == q:
        @pl.when(c < pl.num_programs(1) - 1)
        def _():
            xp_ref[0:halo, :] = xp_ref[q:q + halo, :]


def _ssd(xbc, z, dt, conv0, s0, conv_w, conv_b, a_log, d_skip, norm_g, bsz, seq):
    conv_dim = xbc.shape[-1]
    n_heads = dt.shape[-1]
    d_inner = n_heads * SSD_HEAD_DIM
    t_rows = min(CHUNK, seq)
    n_chunks = seq // t_rows
    q_rows = CHUNK if t_rows == CHUNK else _round_up(t_rows, BF16_ROWS)
    has_state = s0 is not None
    hpg = n_heads // SSD_GROUPS
    e64 = (jnp.arange(d_inner)[None, :] // SSD_HEAD_DIM == jnp.arange(n_heads)[:, None]).astype(BF16)
    x3 = lambda a: a.reshape(bsz, seq, a.shape[-1])
    rows = lambda w: pl.BlockSpec((None, t_rows, w), lambda b, c: (b, c, 0))
    const = lambda shape: pl.BlockSpec(shape, lambda b, c: (0,) * len(shape))
    state_spec = pl.BlockSpec((None, n_heads, SSD_HEAD_DIM, D_STATE), lambda b, c: (b, 0, 0, 0))
    args = [x3(xbc), x3(z), x3(dt), conv0]
    in_specs = [rows(conv_dim), rows(d_inner), rows(n_heads),
                pl.BlockSpec((None, SUBLANES, conv_dim), lambda b, c: (b, 0, 0))]
    if has_state:
        args.append(s0)
        in_specs.append(state_spec)
    args += [conv_w, conv_b.reshape(1, conv_dim), a_log.reshape(1, n_heads), d_skip.reshape(1, n_heads),
             norm_g.reshape(1, d_inner), e64]
    in_specs += [const((CONV_W, conv_dim)), const((1, conv_dim)), const((1, n_heads)), const((1, n_heads)),
                 const((1, d_inner)), const(e64.shape)]
    kern = functools.partial(_ssd_kernel, t_rows=t_rows, q_rows=q_rows, has_state=has_state)
    y, sfin, cnew = pl.pallas_call(
        kern, grid=(bsz, n_chunks), in_specs=in_specs,
        out_specs=[rows(d_inner), state_spec,
                   pl.BlockSpec((None, CONV_W - 1, conv_dim), lambda b, c: (b, 0, 0))],
        out_shape=[jax.ShapeDtypeStruct((bsz, seq, d_inner), BF16),
                   jax.ShapeDtypeStruct((bsz, n_heads, SSD_HEAD_DIM, D_STATE), F32),
                   jax.ShapeDtypeStruct((bsz, CONV_W - 1, conv_dim), F32)],
        scratch_shapes=[pltpu.VMEM((SUBLANES + q_rows, conv_dim), F32),
                        pltpu.VMEM((q_rows, conv_dim), F32),
                        pltpu.VMEM((q_rows, n_heads), F32),
                        pltpu.VMEM((n_heads, q_rows), F32),
                        pltpu.VMEM((n_heads, q_rows, LANES), F32),
                        pltpu.VMEM((SSD_GROUPS, D_STATE, hpg * SSD_HEAD_DIM), F32)],
        compiler_params=_cparams(("parallel", "arbitrary")), name="ssd")(*args)
    return y.reshape(bsz * seq, d_inner), sfin, cnew


def _merge_kernel(ya_ref, ys_ref, wa_ref, *rest):
    ws_refs, (ga_ref, gs_ref, o_ref) = rest[:-3], rest[-3:]
    kb = wa_ref.shape[0]
    ya = jnp.dot(ya_ref[...], wa_ref[...], preferred_element_type=F32)
    yb = jnp.dot(ys_ref[:, :kb], ws_refs[0][...], preferred_element_type=F32)
    for i in range(1, len(ws_refs)):
        yb = yb + jnp.dot(ys_ref[:, i * kb:(i + 1) * kb], ws_refs[i][...], preferred_element_type=F32)
    o_ref[...] = (ga_ref[...].astype(F32) * ya + gs_ref[...].astype(F32) * yb).astype(o_ref.dtype)


def _merge(y_att, y_ssd, wb, gates, tm):
    m, att_w = y_att.shape
    d_inner = y_ssd.shape[1]
    d = wb.shape[1]
    tn = 512
    nj = d // tn
    n_ssd = d_inner // att_w
    w_blk = lambda r: pl.BlockSpec((att_w, tn), lambda i, j: (r, j))
    return pl.pallas_call(
        _merge_kernel, grid=(m // tm, nj),
        in_specs=[pl.BlockSpec((tm, att_w), lambda i, j: (i, 0)), pl.BlockSpec((tm, d_inner), lambda i, j: (i, 0))]
                 + [w_blk(r) for r in range(1 + n_ssd)]
                 + [pl.BlockSpec((tm, tn), lambda i, j: (i, j)), pl.BlockSpec((tm, tn), lambda i, j: (i, j + nj))],
        out_specs=pl.BlockSpec((tm, tn), lambda i, j: (i, j)),
        out_shape=jax.ShapeDtypeStruct((m, d), BF16),
        compiler_params=_cparams(("parallel", "parallel")), name="merge",
    )(y_att, y_ssd, *([wb] * (1 + n_ssd)), gates, gates)


def _out_kernel(m_ref, w_ref, x_ref, g_ref, o_ref):
    h = x_ref[...] + jnp.dot(m_ref[...], w_ref[...], preferred_element_type=F32)
    ms = jnp.mean(h * h, axis=-1, keepdims=True)
    o_ref[...] = h * lax.rsqrt(ms + EPS) * g_ref[...]


def _out_proj(merged, w_out, x, final_g, tm):
    m, d = x.shape
    return pl.pallas_call(
        _out_kernel, grid=(m // tm,),
        in_specs=[pl.BlockSpec((tm, d), lambda i: (i, 0)), pl.BlockSpec((d, d), lambda i: (0, 0)),
                  pl.BlockSpec((tm, d), lambda i: (i, 0)), pl.BlockSpec((1, d), lambda i: (0, 0))],
        out_specs=pl.BlockSpec((tm, d), lambda i: (i, 0)),
        out_shape=jax.ShapeDtypeStruct((m, d), F32),
        compiler_params=_cparams(("parallel",)), name="out_proj")(merged, w_out, x, final_g.reshape(1, d))


def _round_up(x, n):
    return (x + n - 1) // n * n


def kernel(x_prompt, x_sample, cache_k, cache_v, cache_idx_k, state_ssm, state_conv, page_table, norm_g, w_in, conv_w,
           conv_b, dt_bias, a_log, d_skip, ssd_norm_g, idx_ln_w, idx_ln_b, w_branch, w_out, final_norm_g):
    assert w_in.shape[0] == 1, "single-layer trunk"
    bp, seq, d = x_prompt.shape
    bs, n_tok, _ = x_sample.shape
    n_phys, page = cache_k.shape[1], cache_k.shape[2]
    past = page_table.shape[1] * page
    att_w, kv_w = ATT_HEADS * HEAD_DIM, KV_HEADS * HEAD_DIM
    d_inner = 2 * d
    conv_dim = d_inner + 2 * SSD_GROUPS * D_STATE
    top_k_p = max(1, min(TOPK_MAX, seq // 4))
    top_k_s = max(1, min(TOPK_MAX, (past + n_tok) // 4))

    wb = w_branch[0].astype(BF16)
    w_out_b = w_out[0].astype(BF16)
    w_in0 = jnp.swapaxes(w_in[0], 0, 1)

    mp = bp * seq
    tm_p = min(1024, seq)
    xp = x_prompt.reshape(mp, d)
    xn_p = _rmsnorm_bf16(xp, norm_g[0], min(512, seq))
    pp = _projections(xn_p, w_in0, jnp.arange(seq, dtype=I32), seq // tm_p, tm_p, idx_ln_w[0], idx_ln_b[0], dt_bias[0])
    y_att_p = _prompt_attention(pp, bp, seq, top_k_p)
    y_ssd_p, ssm_p, conv_p = _ssd(pp["xbc"], pp["z_ssd"], pp["dt"], jnp.zeros((bp, SUBLANES, conv_dim), F32), None,
                                  conv_w[0], conv_b[0], a_log[0], d_skip[0], ssd_norm_g[0], bp, seq)
    merged_p = _merge(y_att_p, y_ssd_p, wb, pp["gates"], min(512, seq))
    y_p = _out_proj(merged_p, w_out_b, xp, final_norm_g, min(512, seq))

    ms_rows = bs * n_tok
    rows_pad = _round_up(ms_rows, LANES)
    xs = x_sample.reshape(ms_rows, d)
    xs_pad = jnp.pad(xs, ((0, rows_pad - ms_rows), (0, 0))) if rows_pad != ms_rows else xs
    pos_s = past + (jnp.arange(rows_pad, dtype=I32) % n_tok)
    xn_s = _rmsnorm_bf16(xs_pad, norm_g[0], rows_pad)
    ps = _projections(xn_s, w_in0, pos_s, 1, rows_pad, idx_ln_w[0], idx_ln_b[0], dt_bias[0])
    real = lambda a: a[:ms_rows]

    qi_rows = jnp.transpose(ps["qi"][:, :ms_rows], (1, 0, 2)).reshape(bs, n_tok * IDX_HEADS, IDX_DIM)
    wi_col = real(ps["wi"]).reshape(bs, n_tok * IDX_HEADS, 1)
    scores = _sample_scores(qi_rows, wi_col, jnp.swapaxes(cache_idx_k[0], 1, 2), page_table, n_tok)
    scores = scores.reshape(ms_rows, past)
    if rows_pad != ms_rows:
        scores = jnp.pad(scores, ((0, rows_pad - ms_rows), (0, 0)))
    keep, keep_new = _sample_select(scores, ps["qi"], ps["ki_bf"], ps["wi"], n_tok, ms_rows, top_k_s)
    keep = real(keep).reshape(bs, n_tok, past * KV_HEADS)
    keep_new = real(keep_new).reshape(bs, n_tok, rows_pad * KV_HEADS)
    q_rows = jnp.transpose(ps["q"][:, :ms_rows], (1, 0, 2)).reshape(bs, n_tok * ATT_HEADS, HEAD_DIM)
    z_rows = real(ps["z_att"]).reshape(bs, n_tok * ATT_HEADS, HEAD_DIM)
    k_new = ps["k"].reshape(rows_pad * KV_HEADS, HEAD_DIM)
    v_new = ps["v"].reshape(rows_pad * KV_HEADS, HEAD_DIM)
    ck = cache_k[0].reshape(n_phys, page * KV_HEADS, HEAD_DIM)
    cv = cache_v[0].reshape(n_phys, page * KV_HEADS, HEAD_DIM)
    y_att_s = _sample_attention(q_rows, z_rows, keep, keep_new, k_new, v_new, ck, cv, page_table, n_tok)
    y_att_s = y_att_s.reshape(ms_rows, att_w)

    conv0_s = jnp.pad(state_conv[0], ((0, 0), (SUBLANES - (CONV_W - 1), 0), (0, 0)))
    y_ssd_s, ssm_s, conv_s = _ssd(real(ps["xbc"]), real(ps["z_ssd"]), real(ps["dt"]), conv0_s, state_ssm[0],
                                  conv_w[0], conv_b[0], a_log[0], d_skip[0], ssd_norm_g[0], bs, n_tok)
    if rows_pad != ms_rows:
        padr = lambda a: jnp.pad(a, ((0, rows_pad - ms_rows), (0, 0)))
        y_att_s, y_ssd_s = padr(y_att_s), padr(y_ssd_s)
    merged_s = _merge(y_att_s, y_ssd_s, wb, ps["gates"], rows_pad)
    y_s = real(_out_proj(merged_s, w_out_b, xs_pad, final_norm_g, rows_pad))

    kv5 = lambda a, b, t: a.reshape(1, b, t, KV_HEADS, HEAD_DIM)
    return (y_p.reshape(bp, seq, d), y_s.reshape(bs, n_tok, d),
            kv5(pp["k"], bp, seq), kv5(pp["v"], bp, seq), pp["ki"].reshape(1, bp, seq, IDX_DIM),
            ssm_p[None], conv_p[None],
            kv5(real(ps["k"]), bs, n_tok), kv5(real(ps["v"]), bs, n_tok), real(ps["ki"]).reshape(1, bs, n_tok, IDX_DIM),
            ssm_s[None], conv_s[None])
```

```python
import functools

import numpy as np
import jax
import jax.numpy as jnp
from jax import lax
from jax.experimental import pallas as pl
from jax.experimental.pallas import tpu as pltpu

F32, BF16, I32 = jnp.float32, jnp.bfloat16, jnp.int32

ATT_HEADS = 16
KV_HEADS = 4
HEAD_DIM = 128
IDX_HEADS = 16
IDX_DIM = 64
TOPK_MAX = 256
ROPE_THETA = 10000.0
SSD_HEAD_DIM = 64
SSD_GROUPS = 8
D_STATE = 128
CONV_W = 4
CHUNK = 128
EPS = 1e-6

LANES = 128
SUBLANES = 8
BF16_ROWS = 16
VMEM_LIMIT = 56 * 1024 * 1024
INT_MIN = -2 ** 31
KEY_OF_MOST_NEGATIVE_FLOAT = INT_MIN + 0x00800000
MASKED = -1e30
LOG2E = 1.4426950408889634
PAGES_PER_STEP = 32
SCORE_PAGES_PER_STEP = 32


def _cparams(sem):
    return pltpu.CompilerParams(dimension_semantics=sem, vmem_limit_bytes=VMEM_LIMIT)


def _nt_dot(a, b):
    return lax.dot_general(a, b, (((1,), (1,)), ((), ())), preferred_element_type=F32)


def _tile_lanes(x, n):
    return x if n == 1 else jnp.concatenate([x] * n, axis=1)


def _split3(x):
    hi = x.astype(BF16)
    r1 = x - hi.astype(F32)
    mid = r1.astype(BF16)
    lo = (r1 - mid.astype(F32)).astype(BF16)
    return hi, mid, lo


def _dot_exact_rhs(x, e):
    hi, mid, lo = _split3(x)
    d = functools.partial(jnp.dot, preferred_element_type=F32)
    return d(hi, e) + d(mid, e) + d(lo, e)


def _dot_exact_lhs(e, x):
    hi, mid, lo = _split3(x)
    d = functools.partial(jnp.dot, preferred_element_type=F32)
    return d(e, hi) + d(e, mid) + d(e, lo)


def _norm_kernel(x_ref, g_ref, o_ref):
    x = x_ref[...]
    ms = jnp.mean(x * x, axis=-1, keepdims=True)
    o_ref[...] = (x * lax.rsqrt(ms + EPS) * g_ref[...]).astype(o_ref.dtype)


def _rmsnorm_bf16(x, g, tm):
    m, d = x.shape
    return pl.pallas_call(
        _norm_kernel, grid=(m // tm,),
        in_specs=[pl.BlockSpec((tm, d), lambda i: (i, 0)), pl.BlockSpec((1, d), lambda i: (0, 0))],
        out_specs=pl.BlockSpec((tm, d), lambda i: (i, 0)),
        out_shape=jax.ShapeDtypeStruct((m, d), BF16),
        compiler_params=_cparams(("parallel",)), name="rmsnorm")(x, g.reshape(1, d))


def _mm_kernel(x_ref, w_ref, *rest, epilogue, n_aux):
    acc = _nt_dot(x_ref[...], w_ref[...].astype(BF16))
    epilogue(acc, rest[:n_aux], rest[n_aux:])


def _matmul(x, wt, row0, n, epilogue, aux, aux_specs, out_shapes, out_specs, tm, tn, name):
    m, k = x.shape
    kern = functools.partial(_mm_kernel, epilogue=epilogue, n_aux=len(aux))
    return pl.pallas_call(
        kern, grid=(m // tm, n // tn),
        in_specs=[pl.BlockSpec((tm, k), lambda i, j: (i, 0)),
                  pl.BlockSpec((pl.Element(tn), pl.Element(k)),
                               lambda i, j: (pl.multiple_of(row0 + j * tn, SUBLANES), 0))] + list(aux_specs),
        out_specs=out_specs, out_shape=out_shapes,
        compiler_params=_cparams(("parallel", "parallel")), name=name)(x, wt, *aux)


def _rope_half(x, cos, sin_signed, half):
    if 2 * half == LANES:
        partner = pltpu.roll(x, half, 1)
    else:
        partner = jnp.concatenate([x[:, half:], x[:, :half]], axis=1)
    return x * cos + partner * sin_signed


def _epi_rope128(acc, aux, outs, post_scale=None, head_major=False):
    cos, sin = aux[0][...], aux[1][...]
    for c in range(acc.shape[1] // HEAD_DIM):
        sl = slice(c * HEAD_DIM, (c + 1) * HEAD_DIM)
        r = _rope_half(acc[:, sl], cos, sin, HEAD_DIM // 2)
        if post_scale is not None:
            r = r * post_scale
        for o in outs:
            if head_major:
                o[c] = r.astype(o.dtype)
            else:
                o[:, sl] = r.astype(o.dtype)


def _epi_store(acc, aux, outs, fn=None):
    val = acc if fn is None else fn(acc)
    for o in outs:
        o[...] = val.astype(o.dtype)


def _epi_qi(acc, aux, outs):
    cos, sin = aux[0][...], aux[1][...]
    for hh in range(acc.shape[1] // IDX_DIM):
        x = acc[:, hh * IDX_DIM:(hh + 1) * IDX_DIM]
        outs[0][hh] = _rope_half(x, cos, sin, IDX_DIM // 2).astype(outs[0].dtype)


def _epi_kiwi(acc, aux, outs, idx_scale):
    lnw, lnb, cos, sin = (a[...] for a in aux)
    ki = acc[:, :IDX_DIM]
    mu = jnp.mean(ki, axis=-1, keepdims=True)
    kc = ki - mu
    y = kc * lax.rsqrt(jnp.mean(kc * kc, axis=-1, keepdims=True) + EPS) * lnw + lnb
    r = _rope_half(y, cos, sin, IDX_DIM // 2)
    outs[0][...] = r
    outs[1][...] = r.astype(BF16)
    outs[2][...] = acc[:, IDX_DIM:] * idx_scale


def _epi_softplus(acc, aux, outs):
    x = acc + aux[0][...]
    outs[0][...] = jnp.maximum(x, 0.0) + jnp.log1p(jnp.exp(-jnp.abs(x)))


def _silu(x):
    return x * jax.nn.sigmoid(x)


def _rope_tables(pos, d):
    inv = ROPE_THETA ** (-jnp.arange(0, d, 2, dtype=F32) / d)
    ang = pos.astype(F32)[:, None] * inv[None, :]
    cos, sin = jnp.cos(ang), jnp.sin(ang)
    return jnp.concatenate([cos, cos], axis=-1), jnp.concatenate([-sin, sin], axis=-1)


def _projections(xn, wt, pos_rows, n_pos_blocks, tm, idx_ln_w, idx_ln_b, dt_bias):
    m, d = xn.shape
    att_w, kv_w = ATT_HEADS * HEAD_DIM, KV_HEADS * HEAD_DIM
    d_inner = 2 * d
    conv_dim = d_inner + 2 * SSD_GROUPS * D_STATE
    ssd_heads = d_inner // SSD_HEAD_DIM
    splits = (att_w, kv_w, kv_w, att_w, IDX_HEADS * IDX_DIM, IDX_DIM, IDX_HEADS, d_inner, conv_dim,
              ssd_heads, d, d)
    assert sum(splits) == wt.shape[0]
    off = np.concatenate([[0], np.cumsum(splits)]).astype(int)
    assert all(o % BF16_ROWS == 0 for o in off), "segment rows of the weight must start on packed-row boundaries"
    seg = lambda a, b: (wt, int(off[a]), int(off[b] - off[a]))

    cos128, sin128 = _rope_tables(pos_rows, HEAD_DIM)
    cos64, sin64 = _rope_tables(pos_rows, IDX_DIM)
    pos_map = lambda i, j: (i % n_pos_blocks, 0)
    t128 = [pl.BlockSpec((tm, HEAD_DIM), pos_map)] * 2
    t64 = [pl.BlockSpec((tm, IDX_DIM), pos_map)] * 2
    tile = lambda tn: pl.BlockSpec((tm, tn), lambda i, j: (i, j))
    sds = lambda n, dt: jax.ShapeDtypeStruct((m, n), dt)
    tn = 512
    tw = 1024

    (q,) = _matmul(xn, *seg(0, 1), functools.partial(_epi_rope128, post_scale=HEAD_DIM ** -0.5 * LOG2E, head_major=True),
                   [cos128, sin128], t128, [jax.ShapeDtypeStruct((ATT_HEADS, m, HEAD_DIM), BF16)],
                   [pl.BlockSpec((tw // HEAD_DIM, tm, HEAD_DIM), lambda i, j: (j, i, 0))], tm, tw, "proj_q")
    k, k_bf = _matmul(xn, *seg(1, 2), _epi_rope128, [cos128, sin128], t128,
                      [sds(kv_w, F32), sds(kv_w, BF16)], [tile(tn)] * 2, tm, tn, "proj_k")
    v, v_bf = _matmul(xn, *seg(2, 3), _epi_store, [], [], [sds(kv_w, F32), sds(kv_w, BF16)],
                      [tile(tn)] * 2, tm, tn, "proj_v")
    (z_att,) = _matmul(xn, *seg(3, 4), functools.partial(_epi_store, fn=_silu), [], [],
                       [sds(att_w, BF16)], [tile(tw)], tm, tw, "proj_zatt")
    qi_tn = 4 * IDX_DIM
    (qi,) = _matmul(xn, *seg(4, 5), _epi_qi, [cos64, sin64], t64,
                    [jax.ShapeDtypeStruct((IDX_HEADS, m, IDX_DIM), BF16)],
                    [pl.BlockSpec((4, tm, IDX_DIM), lambda i, j: (j, i, 0))], tm, qi_tn, "proj_qi")
    kiwi_n = IDX_DIM + IDX_HEADS
    idx_scale = IDX_HEADS ** -0.5 * IDX_DIM ** -0.5
    row1 = lambda n: pl.BlockSpec((1, n), lambda i, j: (0, 0))
    ki, ki_bf, wi = _matmul(
        xn, *seg(5, 7), functools.partial(_epi_kiwi, idx_scale=idx_scale),
        [idx_ln_w.reshape(1, IDX_DIM), idx_ln_b.reshape(1, IDX_DIM), cos64, sin64],
        [row1(IDX_DIM), row1(IDX_DIM)] + t64,
        [sds(IDX_DIM, F32), sds(IDX_DIM, BF16), sds(IDX_HEADS, F32)],
        [pl.BlockSpec((tm, IDX_DIM), lambda i, j: (i, 0))] * 2 + [pl.BlockSpec((tm, IDX_HEADS), lambda i, j: (i, 0))],
        tm, kiwi_n, "proj_kiwi")
    (z_ssd,) = _matmul(xn, *seg(7, 8), functools.partial(_epi_store, fn=_silu), [], [],
                       [sds(d_inner, BF16)], [tile(tw)], tm, tw, "proj_zssd")
    (xbc,) = _matmul(xn, *seg(8, 9), _epi_store, [], [], [sds(conv_dim, F32)], [tile(tw)], tm, tw, "proj_xbc")
    (dt,) = _matmul(xn, *seg(9, 10), _epi_softplus, [dt_bias.reshape(1, ssd_heads)], [row1(ssd_heads)],
                    [sds(ssd_heads, F32)], [pl.BlockSpec((tm, ssd_heads), lambda i, j: (i, 0))],
                    tm, ssd_heads, "proj_dt")
    (gates,) = _matmul(xn, *seg(10, 12), functools.partial(_epi_store, fn=jax.nn.sigmoid), [], [],
                       [sds(2 * d, BF16)], [tile(tw)], tm, tw, "proj_gates")
    return dict(q=q, k=k, k_bf=k_bf, v=v, v_bf=v_bf, z_att=z_att, qi=qi, ki=ki, ki_bf=ki_bf, wi=wi,
                z_ssd=z_ssd, xbc=xbc, dt=dt, gates=gates)


def _key_to_float(key):
    return pltpu.bitcast(jnp.where(key < 0, key ^ jnp.int32(0x7FFFFFFF), key), F32)


def _kth_largest(count_ge, shape, top_k):
    kf = jnp.float32(top_k)
    zero = jnp.zeros(shape, I32)
    c0 = jnp.broadcast_to(count_ge(_key_to_float(zero)), shape)
    t0 = jnp.where(c0 >= kf, zero, jnp.full(shape, INT_MIN, I32))
    n0 = jnp.where(c0 >= kf, c0, 0.0)

    def body(i, tn):
        t, n = tn
        cand = t | lax.shift_left(jnp.int32(1), jnp.int32(30) - i)
        c = jnp.broadcast_to(count_ge(_key_to_float(cand)), shape)
        return jnp.where(c >= kf, cand, t), jnp.where(c >= kf, c, n)

    t, n = lax.fori_loop(0, 31, body, (t0, n0))
    return _key_to_float(jnp.maximum(t, jnp.int32(KEY_OF_MOST_NEGATIVE_FLOAT))), n


def _lane_tile_sum(x):
    part = x[:, :LANES]
    for c in range(1, x.shape[1] // LANES):
        part = part + x[:, c * LANES:(c + 1) * LANES]
    return part


def _attn_kernel(q_ref, qi_ref, wi_ref, z_ref, k_ref, v_ref, ki_ref, o_ref,
                 score_ref, wexp_ref, bias_ref, s_ref, m_ref, l_ref, acc_ref, *, tq, tk, tkc, top_k):
    qt = pl.program_id(1)
    n_kc = ((qt + 1) * tq + tk - 1) // tk
    nl = tk // LANES
    row_pos = qt * tq + lax.broadcasted_iota(I32, (tq, 1), 0)

    wi = wi_ref[...]
    for h in range(IDX_HEADS):
        wexp_ref[h] = jnp.broadcast_to(wi[:, h:h + 1], (tq, LANES))

    def score_chunk(kc, carry):
        off = pl.multiple_of(kc * tk, tk)
        ki = ki_ref[pl.ds(off, tk), :]
        d = _nt_dot(qi_ref[...].reshape(IDX_HEADS * tq, IDX_DIM), ki)
        r = jnp.maximum(d, 0.0) * _tile_lanes(wexp_ref[...].reshape(IDX_HEADS * tq, LANES), nl)
        sc = r[0:tq]
        for h in range(1, IDX_HEADS):
            sc = sc + r[h * tq:(h + 1) * tq]
        kpos = off + lax.broadcasted_iota(I32, (1, tk), 1)
        score_ref[:, pl.ds(off, tk)] = jnp.where(kpos <= row_pos, sc, -jnp.inf)
        return carry

    lax.fori_loop(0, n_kc, score_chunk, 0)

    def count(cand, strict=False):
        cand_t = _tile_lanes(cand, nl)

        def body(kc, cnt):
            sc = score_ref[:, pl.ds(pl.multiple_of(kc * tk, tk), tk)]
            hit = jnp.where((sc > cand_t) if strict else (sc >= cand_t), 1.0, 0.0)
            return cnt + _lane_tile_sum(hit)

        cnt = lax.fori_loop(0, n_kc, body, jnp.zeros((tq, LANES), F32))
        return jnp.sum(cnt, axis=1, keepdims=True)

    thr1, n_ge = _kth_largest(count, (tq, LANES), top_k)
    thr = _tile_lanes(thr1, nl)
    has_ties = jnp.max(n_ge) > jnp.float32(top_k)

    @pl.when(jnp.logical_not(has_ties))
    def _():
        def bias_chunk(kc, carry):
            off = pl.multiple_of(kc * tk, tk)
            bias_ref[:, pl.ds(off, tk)] = jnp.where(score_ref[:, pl.ds(off, tk)] >= thr, 0.0, MASKED)
            return carry

        lax.fori_loop(0, n_kc, bias_chunk, 0)

    @pl.when(has_ties)
    def _():
        room = jnp.float32(top_k) - count(thr1, strict=True)
        ri = lax.broadcasted_iota(I32, (tk, tk), 0)
        ci = lax.broadcasted_iota(I32, (tk, tk), 1)
        prefix = jnp.where(ri <= ci, 1.0, 0.0).astype(BF16)

        def bias_chunk(kc, seen):
            off = pl.multiple_of(kc * tk, tk)
            key = score_ref[:, pl.ds(off, tk)]
            eq = jnp.where(key == thr, 1.0, 0.0)
            rank = seen + jnp.dot(eq.astype(BF16), prefix, preferred_element_type=F32)
            keep = (key > thr) | ((key == thr) & (rank <= room))
            bias_ref[:, pl.ds(off, tk)] = jnp.where(keep, 0.0, MASKED)
            return seen + jnp.sum(eq, axis=1, keepdims=True)

        lax.fori_loop(0, n_kc, bias_chunk, jnp.zeros((tq, 1), F32))

    n_kcc = ((qt + 1) * tq + tkc - 1) // tkc
    nlc = tkc // LANES

    def bias_tail(kc, carry):
        bias_ref[:, pl.ds(pl.multiple_of(kc * tk, tk), tk)] = jnp.full((tq, tk), MASKED, F32)
        return carry

    lax.fori_loop(n_kc, n_kcc * (tkc // tk), bias_tail, 0)
    rep = ATT_HEADS // KV_HEADS

    rows = rep * tq

    def logits_chunk(g, ks):
        slot = g % 2
        gs = slice(g * HEAD_DIM, (g + 1) * HEAD_DIM)
        qg = q_ref[g * rep:(g + 1) * rep].reshape(rows, HEAD_DIM)
        bias = bias_ref[:, ks]
        s = jnp.concatenate([bias] * rep, axis=0) + _nt_dot(qg, k_ref[ks, gs])
        s_ref[slot, :, ks] = s
        part = s[:, :LANES]
        for c in range(1, nlc):
            part = jnp.maximum(part, s[:, c * LANES:(c + 1) * LANES])
        m_ref[slot] = jnp.maximum(m_ref[slot], part)

    def pv_chunk(g, ks):
        slot = g % 2
        gs = slice(g * HEAD_DIM, (g + 1) * HEAD_DIM)
        p = jnp.exp2(s_ref[slot, :, ks] - _tile_lanes(m_ref[slot], nlc))
        l_ref[...] += _lane_tile_sum(p)
        acc_ref[...] += jnp.dot(p.astype(BF16), v_ref[ks, gs], preferred_element_type=F32)

    for stage in range(KV_HEADS + 1):
        g_logits = stage if stage < KV_HEADS else None
        g_pv = stage - 1 if stage > 0 else None
        if g_logits is not None:
            m_ref[g_logits % 2] = jnp.full((rows, LANES), MASKED, F32)
        if g_pv is not None:
            slot = g_pv % 2
            m_ref[slot] = jnp.broadcast_to(jnp.max(m_ref[slot], axis=1, keepdims=True), (rows, LANES))
            l_ref[...] = jnp.zeros((rows, LANES), F32)
            acc_ref[...] = jnp.zeros((rows, HEAD_DIM), F32)

        def stage_chunk(kc, carry, g_logits=g_logits, g_pv=g_pv):
            ks = pl.ds(pl.multiple_of(kc * tkc, tkc), tkc)
            if g_logits is not None:
                logits_chunk(g_logits, ks)
            if g_pv is not None:
                pv_chunk(g_pv, ks)
            return carry

        lax.fori_loop(0, n_kcc, stage_chunk, 0)
        if g_pv is not None:
            o = acc_ref[...] / jnp.sum(l_ref[...], axis=1, keepdims=True)
            for r in range(rep):
                hs = slice((g_pv * rep + r) * HEAD_DIM, (g_pv * rep + r + 1) * HEAD_DIM)
                o_ref[:, hs] = (o[r * tq:(r + 1) * tq] * z_ref[:, hs].astype(F32)).astype(o_ref.dtype)


def _prompt_attention(p, batch, seq, top_k):
    m = batch * seq
    tq = min(128, seq)
    tk = min(512, seq)
    nq = seq // tq
    att_w, kv_w = ATT_HEADS * HEAD_DIM, KV_HEADS * HEAD_DIM
    rep = ATT_HEADS // KV_HEADS
    rows = lambda w: pl.BlockSpec((tq, w), lambda b, t: (b * nq + t, 0))
    whole = lambda w: pl.BlockSpec((seq, w), lambda b, t: (b, 0))
    tkc = min(1024, seq)
    kern = functools.partial(_attn_kernel, tq=tq, tk=tk, tkc=tkc, top_k=top_k)
    return pl.pallas_call(
        kern, grid=(batch, nq),
        in_specs=[pl.BlockSpec((ATT_HEADS, tq, HEAD_DIM), lambda b, t: (0, b * nq + t, 0)),
                  pl.BlockSpec((IDX_HEADS, tq, IDX_DIM), lambda b, t: (0, b * nq + t, 0)),
                  rows(IDX_HEADS), rows(att_w), whole(kv_w), whole(kv_w), whole(IDX_DIM)],
        out_specs=rows(att_w),
        out_shape=jax.ShapeDtypeStruct((m, att_w), BF16),
        scratch_shapes=[pltpu.VMEM((tq, seq), F32),
                        pltpu.VMEM((IDX_HEADS, tq, LANES), F32),
                        pltpu.VMEM((tq, seq), F32),
                        pltpu.VMEM((2, rep * tq, seq), F32),
                        pltpu.VMEM((2, rep * tq, LANES), F32),
                        pltpu.VMEM((rep * tq, LANES), F32),
                        pltpu.VMEM((rep * tq, HEAD_DIM), F32)],
        compiler_params=_cparams(("parallel", "arbitrary")), name="prompt_attention",
    )(p["q"], p["qi"], p["wi"], p["z_att"], p["k_bf"], p["v_bf"], p["ki_bf"])


def _sample_score_kernel(pt_ref, qi_ref, wi_ref, *rest, n_pages, n_tok):
    page_refs, o_ref = rest[:n_pages], rest[n_pages]
    qi = qi_ref[...]
    w = jnp.broadcast_to(wi_ref[...], (qi.shape[0], LANES))
    for k in range(n_pages):
        kit = page_refs[k][...].astype(BF16)
        page = kit.shape[1]
        r = jnp.maximum(jnp.dot(qi, kit, preferred_element_type=F32), 0.0) * _tile_lanes(w, page // LANES)
        sc = jnp.sum(r.reshape(n_tok, IDX_HEADS, page), axis=1)
        o_ref[:, k * page:(k + 1) * page] = sc


def _sample_scores(qi_rows, wi_col, cache_idx_kt, page_table, n_tok):
    bsz, n_pages_total = page_table.shape
    page = cache_idx_kt.shape[2]
    g = SCORE_PAGES_PER_STEP if n_pages_total % SCORE_PAGES_PER_STEP == 0 else 1
    rows = n_tok * IDX_HEADS
    page_spec = lambda k: pl.BlockSpec((None, IDX_DIM, page), lambda b, s, pt: (pt[b, s * g + k], 0, 0))
    grid_spec = pltpu.PrefetchScalarGridSpec(
        num_scalar_prefetch=1, grid=(bsz, n_pages_total // g),
        in_specs=[pl.BlockSpec((None, rows, IDX_DIM), lambda b, s, pt: (b, 0, 0)),
                  pl.BlockSpec((None, rows, 1), lambda b, s, pt: (b, 0, 0))] + [page_spec(k) for k in range(g)],
        out_specs=pl.BlockSpec((None, n_tok, g * page), lambda b, s, pt: (b, 0, s)))
    kern = functools.partial(_sample_score_kernel, n_pages=g, n_tok=n_tok)
    return pl.pallas_call(
        kern, grid_spec=grid_spec,
        out_shape=jax.ShapeDtypeStruct((bsz, n_tok, n_pages_total * page), F32),
        compiler_params=_cparams(("parallel", "arbitrary")), name="sample_scores",
    )(page_table, qi_rows, wi_col, *([cache_idx_kt] * g))


def _sample_select_kernel(sc_ref, qi_ref, ki_ref, wi_ref, rexp_ref, keep_ref, keep_new_ref, *, n_tok, n_rows, top_k):
    rows, past = sc_ref.shape
    nk = ki_ref.shape[0]
    wi = wi_ref[...]
    ki = ki_ref[...]
    sc_new = jnp.zeros((rows, nk), F32)
    for h in range(IDX_HEADS):
        w = jnp.broadcast_to(wi[:, h:h + 1], (rows, LANES))
        sc_new = sc_new + jnp.maximum(_nt_dot(qi_ref[h], ki), 0.0) * _tile_lanes(w, nk // LANES)
    r = lax.broadcasted_iota(I32, (rows, nk), 0)
    c = lax.broadcasted_iota(I32, (rows, nk), 1)
    ok = (r // n_tok == c // n_tok) & (c % n_tok <= r % n_tok) & (r < n_rows) & (c < n_rows)
    sc_new = jnp.where(ok, sc_new, -jnp.inf)

    def count(cand, strict=False):
        hit = lambda sc, n: jnp.where((sc > _tile_lanes(cand, n)) if strict else (sc >= _tile_lanes(cand, n)), 1.0, 0.0)
        return jnp.sum(_lane_tile_sum(hit(sc_ref[...], past // LANES)) + _lane_tile_sum(hit(sc_new, nk // LANES)),
                       axis=1, keepdims=True)

    thr, _ = _kth_largest(count, (rows, LANES), top_k)
    rexp = rexp_ref[...]
    xw = rexp.shape[1]
    room = jnp.float32(top_k) - count(thr, strict=True)
    ri = lax.broadcasted_iota(I32, (LANES, LANES), 0)
    ci = lax.broadcasted_iota(I32, (LANES, LANES), 1)
    prefix = jnp.where(ri <= ci, 1.0, 0.0).astype(BF16)

    def select(key_tile, seen):
        eq = jnp.where(key_tile == thr, 1.0, 0.0)
        rank = seen + jnp.dot(eq.astype(BF16), prefix, preferred_element_type=F32)
        keep = (key_tile > thr) | ((key_tile == thr) & (rank <= room))
        sel = jnp.where(keep, 1.0, 0.0).astype(BF16)
        return (jnp.dot(sel, rexp, preferred_element_type=F32).astype(BF16),
                seen + jnp.sum(eq, axis=1, keepdims=True))

    def past_tile(ct, seen):
        src = pl.ds(pl.multiple_of(ct * LANES, LANES), LANES)
        kept, seen = select(sc_ref[:, src], seen)
        keep_ref[:, pl.ds(pl.multiple_of(ct * xw, xw), xw)] = kept
        return seen

    seen = lax.fori_loop(0, past // LANES, past_tile, jnp.zeros((rows, 1), F32))
    for ct in range(nk // LANES):
        kept, seen = select(sc_new[:, ct * LANES:(ct + 1) * LANES], seen)
        keep_new_ref[:, ct * xw:(ct + 1) * xw] = kept


def _sample_select(scores, qi, ki_new_bf, wi, n_tok, n_rows, top_k):
    rows, past = scores.shape
    nk = ki_new_bf.shape[0]
    rexp = (jnp.arange(LANES * KV_HEADS)[None, :] // KV_HEADS == jnp.arange(LANES)[:, None]).astype(BF16)
    kern = functools.partial(_sample_select_kernel, n_tok=n_tok, n_rows=n_rows, top_k=top_k)
    full = lambda shape: pl.BlockSpec(shape, lambda i: (0,) * len(shape))
    return pl.pallas_call(
        kern, grid=(1,),
        in_specs=[full((rows, past)), full((IDX_HEADS, rows, IDX_DIM)), full((nk, IDX_DIM)), full((rows, IDX_HEADS)),
                  full(rexp.shape)],
        out_specs=[full((rows, past * KV_HEADS)), full((rows, nk * KV_HEADS))],
        out_shape=[jax.ShapeDtypeStruct((rows, past * KV_HEADS), BF16),
                   jax.ShapeDtypeStruct((rows, nk * KV_HEADS), BF16)],
        compiler_params=_cparams(("arbitrary",)), name="sample_select")(scores, qi, ki_new_bf, wi, rexp)


def _sample_attn_kernel(pt_ref, q_ref, z_ref, keep_ref, keep_new_ref, knew_ref, vnew_ref, *rest, n_pages, n_tok, page):
    k_refs, v_refs = rest[:n_pages], rest[n_pages:2 * n_pages]
    o_ref, m_ref, l_ref, acc_ref, s_ref = rest[2 * n_pages:]
    step = pl.program_id(1)
    rows = n_tok * ATT_HEADS
    cols = page * KV_HEADS
    rep = ATT_HEADS // KV_HEADS
    nl = cols // LANES

    @pl.when(step == 0)
    def _():
        m_ref[...] = jnp.full(m_ref.shape, MASKED, F32)
        l_ref[...] = jnp.zeros(l_ref.shape, F32)
        acc_ref[...] = jnp.zeros(acc_ref.shape, F32)

    q = q_ref[...]
    rr = lax.broadcasted_iota(I32, (rows, cols), 0)
    cc = lax.broadcasted_iota(I32, (rows, cols), 1)
    head_bias = jnp.where((cc % KV_HEADS) == ((rr % ATT_HEADS) // rep), 0.0, MASKED)

    def update(blocks):
        mx = None
        for i, (load_k, _, load_keep) in enumerate(blocks):
            kp = load_k().astype(BF16)
            kb = (load_keep().astype(F32) - 1.0) * (-MASKED)
            kb_rows = jnp.concatenate(
                [jnp.broadcast_to(kb[t:t + 1, :], (ATT_HEADS, cols)) for t in range(n_tok)], axis=0)
            s = _nt_dot(q, kp) + (kb_rows + head_bias)
            s_ref[:, i * cols:(i + 1) * cols] = s
            part = s[:, :LANES]
            for c in range(1, nl):
                part = jnp.maximum(part, s[:, c * LANES:(c + 1) * LANES])
            mx = part if mx is None else jnp.maximum(mx, part)
        m_old = m_ref[...]
        m_new = jnp.maximum(m_old, jnp.max(mx, axis=1, keepdims=True))
        alpha = jnp.exp2(m_old - m_new)
        m_t = _tile_lanes(m_new, nl)
        lsum = jnp.zeros((rows, LANES), F32)
        pv = jnp.zeros((rows, HEAD_DIM), F32)
        for i, (_, load_v, _) in enumerate(blocks):
            p = jnp.exp2(s_ref[:, i * cols:(i + 1) * cols] - m_t)
            lsum = lsum + _lane_tile_sum(p)
            pv = pv + jnp.dot(p.astype(BF16), load_v().astype(BF16), preferred_element_type=F32)
        l_ref[...] = alpha * l_ref[...] + lsum
        acc_ref[...] = alpha * acc_ref[...] + pv
        m_ref[...] = m_new

    update([(lambda k=k: k_refs[k][...], lambda k=k: v_refs[k][...], lambda k=k: keep_ref[:, k * cols:(k + 1) * cols])
            for k in range(n_pages)])

    @pl.when(step == pl.num_programs(1) - 1)
    def _():
        update([(lambda c=c: knew_ref[c * cols:(c + 1) * cols, :], lambda c=c: vnew_ref[c * cols:(c + 1) * cols, :],
                 lambda c=c: keep_new_ref[:, c * cols:(c + 1) * cols]) for c in range(knew_ref.shape[0] // cols)])
        lsum = jnp.sum(l_ref[...], axis=1, keepdims=True)
        o_ref[...] = (acc_ref[...] / lsum * z_ref[...].astype(F32)).astype(o_ref.dtype)


def _sample_attention(q_rows, z_rows, keep, keep_new, k_new, v_new, cache_k, cache_v, page_table, n_tok):
    bsz, n_pages_total = page_table.shape
    cols = cache_k.shape[1]
    page = cols // KV_HEADS
    g = PAGES_PER_STEP if n_pages_total % PAGES_PER_STEP == 0 else 1
    rows = n_tok * ATT_HEADS
    per_b = lambda shape: pl.BlockSpec((None,) + shape, lambda b, s, pt: (b,) + (0,) * len(shape))
    const = lambda shape: pl.BlockSpec(shape, lambda b, s, pt: (0,) * len(shape))
    page_spec = lambda k: pl.BlockSpec((None, cols, HEAD_DIM), lambda b, s, pt: (pt[b, s * g + k], 0, 0))
    grid_spec = pltpu.PrefetchScalarGridSpec(
        num_scalar_prefetch=1, grid=(bsz, n_pages_total // g),
        in_specs=[per_b((rows, HEAD_DIM)), per_b((rows, HEAD_DIM)),
                  pl.BlockSpec((None, n_tok, g * cols), lambda b, s, pt: (b, 0, s)),
                  per_b((n_tok, keep_new.shape[-1])), const(k_new.shape), const(v_new.shape)]
                 + [page_spec(k) for k in range(g)] * 2,
        out_specs=per_b((rows, HEAD_DIM)),
        scratch_shapes=[pltpu.VMEM((rows, LANES), F32), pltpu.VMEM((rows, LANES), F32),
                        pltpu.VMEM((rows, HEAD_DIM), F32), pltpu.VMEM((rows, g * cols), F32)])
    kern = functools.partial(_sample_attn_kernel, n_pages=g, n_tok=n_tok, page=page)
    return pl.pallas_call(
        kern, grid_spec=grid_spec,
        out_shape=jax.ShapeDtypeStruct((bsz, rows, HEAD_DIM), BF16),
        compiler_params=_cparams(("parallel", "arbitrary")), name="sample_attention",
    )(page_table, q_rows, z_rows, keep, keep_new, k_new, v_new, *([cache_k] * g), *([cache_v] * g))


def _ssd_kernel(*refs, t_rows, q_rows, has_state):
    if has_state:
        (xbc_ref, z_ref, dt_ref, conv0_ref, s0_ref, cw_ref, cb_ref, alog_ref, dskip_ref, ng_ref, e64_ref,
         y_ref, sfin_ref, cnew_ref, xp_ref, act_ref, dtp_ref, acst_ref, acsc_ref, st_ref) = refs
    else:
        (xbc_ref, z_ref, dt_ref, conv0_ref, cw_ref, cb_ref, alog_ref, dskip_ref, ng_ref, e64_ref,
         y_ref, sfin_ref, cnew_ref, xp_ref, act_ref, dtp_ref, acst_ref, acsc_ref, st_ref) = refs
        s0_ref = None
    c = pl.program_id(1)
    q = q_rows
    halo = SUBLANES
    n_heads = dt_ref.shape[-1]
    d_inner = n_heads * SSD_HEAD_DIM
    gw = d_inner // SSD_GROUPS
    hpg = n_heads // SSD_GROUPS
    conv_dim = xbc_ref.shape[-1]

    @pl.when(c == 0)
    def _():
        xp_ref[0:halo, :] = conv0_ref[...]
        if t_rows < q:
            xp_ref[halo:, :] = jnp.zeros((q, conv_dim), F32)
            dtp_ref[...] = jnp.zeros(dtp_ref.shape, F32)
        for g in range(SSD_GROUPS):
            if has_state:
                st_ref[g] = s0_ref[g * hpg:(g + 1) * hpg].reshape(gw, D_STATE).T
            else:
                st_ref[g] = jnp.zeros((D_STATE, gw), F32)

    xp_ref[halo:halo + t_rows, :] = xbc_ref[...]
    dtp_ref[0:t_rows, :] = dt_ref[...]

    cblk = LANES
    for cbi in range(conv_dim // cblk):
        cs = slice(cbi * cblk, (cbi + 1) * cblk)
        acc = jnp.broadcast_to(cb_ref[:, cs], (q, cblk))
        for tap in range(CONV_W):
            lo = halo - (CONV_W - 1) + tap
            acc = acc + xp_ref[lo:lo + q, cs] * cw_ref[tap:tap + 1, cs]
        act_ref[:, cs] = _silu(acc)

    dt = dtp_ref[...]
    a = -jnp.exp(alog_ref[...])
    ri = lax.broadcasted_iota(I32, (q, q), 0)
    ci = lax.broadcasted_iota(I32, (q, q), 1)
    tril = ri >= ci
    a_cs = _dot_exact_lhs(jnp.where(tril, 1.0, 0.0).astype(BF16), dt * a)
    a_last = a_cs[q - 1:q, :]
    acst_ref[...] = a_cs.T
    for h in range(n_heads):
        acsc_ref[h] = jnp.broadcast_to(a_cs[:, h:h + 1], (q, LANES))
    dt_b = dt.astype(BF16)
    dte_b = jnp.exp(a_last - a_cs).astype(BF16)
    ea_hi, ea_mid, _ = _split3(jnp.exp(a_cs))
    pad = jnp.zeros((SUBLANES - 2, n_heads), F32)
    row_pieces = _split3(jnp.concatenate([jnp.exp(a_last), dskip_ref[...], pad], axis=0))
    mxu = functools.partial(jnp.dot, preferred_element_type=F32)
    lane = lax.broadcasted_iota(I32, (1, LANES), 1)
    lo_half = lane < SSD_HEAD_DIM

    def group_body(g, carry):
        xs = act_ref[:, pl.ds(pl.multiple_of(g * gw, gw), gw)]
        bm = act_ref[:, pl.ds(pl.multiple_of(d_inner + g * D_STATE, D_STATE), D_STATE)]
        cm = act_ref[:, pl.ds(pl.multiple_of(d_inner + SSD_GROUPS * D_STATE + g * D_STATE, D_STATE), D_STATE)]
        e64g = e64_ref[:, pl.ds(pl.multiple_of(g * gw, gw), gw)]
        dt_x, dte_x = mxu(dt_b, e64g), mxu(dte_b, e64g)
        ea_x = mxu(ea_hi, e64g) + mxu(ea_mid, e64g)
        rows_x = mxu(row_pieces[0], e64g) + mxu(row_pieces[1], e64g) + mxu(row_pieces[2], e64g)
        cdec_x, dskip_x = rows_x[0:1], rows_x[1:2]
        xdt = xs * dt_x
        xdt_b = xdt.astype(BF16)
        bm_b, cm_b = bm.astype(BF16), cm.astype(BF16)
        cb = _nt_dot(cm_b, bm_b)
        st = st_ref[g]
        y_off = jnp.dot(cm_b, st.astype(BF16), preferred_element_type=F32) * ea_x
        y_parts = []
        for pr in range(hpg // 2):
            xpair = xdt_b[:, pr * LANES:(pr + 1) * LANES]
            halves = (jnp.where(lo_half, xpair, jnp.zeros_like(xpair)), jnp.where(lo_half, jnp.zeros_like(xpair), xpair))
            yp = jnp.zeros((q, LANES), F32)
            for s in range(2):
                hl = 2 * pr + s
                seg = acsc_ref[g * hpg + hl][:, :q] - acst_ref[pl.ds(g * hpg + hl, 1), :]
                lmat = jnp.where(tril, jnp.exp(jnp.where(tril, seg, 0.0)), 0.0)
                yp = yp + jnp.dot((cb * lmat).astype(BF16), halves[s], preferred_element_type=F32)
            y_parts.append(yp)
        y = jnp.concatenate(y_parts, axis=1) + y_off + dskip_x * xs
        st_ref[g] = cdec_x * st + jnp.dot(bm.T.astype(BF16), (xdt * dte_x).astype(BF16), preferred_element_type=F32)
        gsl = pl.ds(pl.multiple_of(g * gw, gw), gw)
        yz = y[0:t_rows] * z_ref[:, gsl].astype(F32)
        ms = jnp.mean(yz * yz, axis=-1, keepdims=True)
        y_ref[:, gsl] = (yz * lax.rsqrt(ms + EPS) * ng_ref[:, gsl]).astype(y_ref.dtype)
        return carry

    lax.fori_loop(0, SSD_GROUPS, group_body, 0)

    @pl.when(c == pl.num_programs(1) - 1)
    def _():
        cnew_ref[...] = xp_ref[halo + t_rows - (CONV_W - 1):halo + t_rows, :]
        for g in range(SSD_GROUPS):
            sfin_ref[g * hpg:(g + 1) * hpg] = st_ref[g].T.reshape(hpg, SSD_HEAD_DIM, D_STATE)

    if t_rows == q:
        @pl.when(c < pl.num_programs(1) - 1)
        def _():
            xp_ref[0:halo, :] = xp_ref[q:q + halo, :]


def _ssd(xbc, z, dt, conv0, s0, conv_w, conv_b, a_log, d_skip, norm_g, bsz, seq):
    conv_dim = xbc.shape[-1]
    n_heads = dt.shape[-1]
    d_inner = n_heads * SSD_HEAD_DIM
    t_rows = min(CHUNK, seq)
    n_chunks = seq // t_rows
    q_rows = CHUNK if t_rows == CHUNK else _round_up(t_rows, BF16_ROWS)
    has_state = s0 is not None
    hpg = n_heads // SSD_GROUPS
    e64 = (jnp.arange(d_inner)[None, :] // SSD_HEAD_DIM == jnp.arange(n_heads)[:, None]).astype(BF16)
    x3 = lambda a: a.reshape(bsz, seq, a.shape[-1])
    rows = lambda w: pl.BlockSpec((None, t_rows, w), lambda b, c: (b, c, 0))
    const = lambda shape: pl.BlockSpec(shape, lambda b, c: (0,) * len(shape))
    state_spec = pl.BlockSpec((None, n_heads, SSD_HEAD_DIM, D_STATE), lambda b, c: (b, 0, 0, 0))
    args = [x3(xbc), x3(z), x3(dt), conv0]
    in_specs = [rows(conv_dim), rows(d_inner), rows(n_heads),
                pl.BlockSpec((None, SUBLANES, conv_dim), lambda b, c: (b, 0, 0))]
    if has_state:
        args.append(s0)
        in_specs.append(state_spec)
    args += [conv_w, conv_b.reshape(1, conv_dim), a_log.reshape(1, n_heads), d_skip.reshape(1, n_heads),
             norm_g.reshape(1, d_inner), e64]
    in_specs += [const((CONV_W, conv_dim)), const((1, conv_dim)), const((1, n_heads)), const((1, n_heads)),
                 const((1, d_inner)), const(e64.shape)]
    kern = functools.partial(_ssd_kernel, t_rows=t_rows, q_rows=q_rows, has_state=has_state)
    y, sfin, cnew = pl.pallas_call(
        kern, grid=(bsz, n_chunks), in_specs=in_specs,
        out_specs=[rows(d_inner), state_spec,
                   pl.BlockSpec((None, CONV_W - 1, conv_dim), lambda b, c: (b, 0, 0))],
        out_shape=[jax.ShapeDtypeStruct((bsz, seq, d_inner), BF16),
                   jax.ShapeDtypeStruct((bsz, n_heads, SSD_HEAD_DIM, D_STATE), F32),
                   jax.ShapeDtypeStruct((bsz, CONV_W - 1, conv_dim), F32)],
        scratch_shapes=[pltpu.VMEM((SUBLANES + q_rows, conv_dim), F32),
                        pltpu.VMEM((q_rows, conv_dim), F32),
                        pltpu.VMEM((q_rows, n_heads), F32),
                        pltpu.VMEM((n_heads, q_rows), F32),
                        pltpu.VMEM((n_heads, q_rows, LANES), F32),
                        pltpu.VMEM((SSD_GROUPS, D_STATE, hpg * SSD_HEAD_DIM), F32)],
        compiler_params=_cparams(("parallel", "arbitrary")), name="ssd")(*args)
    return y.reshape(bsz * seq, d_inner), sfin, cnew


def _merge_kernel(ya_ref, ys_ref, wa_ref, *rest):
    ws_refs, (ga_ref, gs_ref, o_ref) = rest[:-3], rest[-3:]
    kb = wa_ref.shape[0]
    ya = jnp.dot(ya_ref[...], wa_ref[...], preferred_element_type=F32)
    yb = jnp.dot(ys_ref[:, :kb], ws_refs[0][...], preferred_element_type=F32)
    for i in range(1, len(ws_refs)):
        yb = yb + jnp.dot(ys_ref[:, i * kb:(i + 1) * kb], ws_refs[i][...], preferred_element_type=F32)
    o_ref[...] = (ga_ref[...].astype(F32) * ya + gs_ref[...].astype(F32) * yb).astype(o_ref.dtype)


def _merge(y_att, y_ssd, wb, gates, tm):
    m, att_w = y_att.shape
    d_inner = y_ssd.shape[1]
    d = wb.shape[1]
    tn = 512
    nj = d // tn
    n_ssd = d_inner // att_w
    w_blk = lambda r: pl.BlockSpec((att_w, tn), lambda i, j: (r, j))
    return pl.pallas_call(
        _merge_kernel, grid=(m // tm, nj),
        in_specs=[pl.BlockSpec((tm, att_w), lambda i, j: (i, 0)), pl.BlockSpec((tm, d_inner), lambda i, j: (i, 0))]
                 + [w_blk(r) for r in range(1 + n_ssd)]
                 + [pl.BlockSpec((tm, tn), lambda i, j: (i, j)), pl.BlockSpec((tm, tn), lambda i, j: (i, j + nj))],
        out_specs=pl.BlockSpec((tm, tn), lambda i, j: (i, j)),
        out_shape=jax.ShapeDtypeStruct((m, d), BF16),
        compiler_params=_cparams(("parallel", "parallel")), name="merge",
    )(y_att, y_ssd, *([wb] * (1 + n_ssd)), gates, gates)


def _out_kernel(m_ref, w_ref, x_ref, g_ref, o_ref):
    h = x_ref[...] + jnp.dot(m_ref[...], w_ref[...], preferred_element_type=F32)
    ms = jnp.mean(h * h, axis=-1, keepdims=True)
    o_ref[...] = h * lax.rsqrt(ms + EPS) * g_ref[...]


def _out_proj(merged, w_out, x, final_g, tm):
    m, d = x.shape
    return pl.pallas_call(
        _out_kernel, grid=(m // tm,),
        in_specs=[pl.BlockSpec((tm, d), lambda i: (i, 0)), pl.BlockSpec((d, d), lambda i: (0, 0)),
                  pl.BlockSpec((tm, d), lambda i: (i, 0)), pl.BlockSpec((1, d), lambda i: (0, 0))],
        out_specs=pl.BlockSpec((tm, d), lambda i: (i, 0)),
        out_shape=jax.ShapeDtypeStruct((m, d), F32),
        compiler_params=_cparams(("parallel",)), name="out_proj")(merged, w_out, x, final_g.reshape(1, d))


def _round_up(x, n):
    return (x + n - 1) // n * n


def kernel(x_prompt, x_sample, cache_k, cache_v, cache_idx_k, state_ssm, state_conv, page_table, norm_g, w_in, conv_w,
           conv_b, dt_bias, a_log, d_skip, ssd_norm_g, idx_ln_w, idx_ln_b, w_branch, w_out, final_norm_g):
    assert w_in.shape[0] == 1, "single-layer trunk"
    bp, seq, d = x_prompt.shape
    bs, n_tok, _ = x_sample.shape
    n_phys, page = cache_k.shape[1], cache_k.shape[2]
    past = page_table.shape[1] * page
    att_w, kv_w = ATT_HEADS * HEAD_DIM, KV_HEADS * HEAD_DIM
    d_inner = 2 * d
    conv_dim = d_inner + 2 * SSD_GROUPS * D_STATE
    top_k_p = max(1, min(TOPK_MAX, seq // 4))
    top_k_s = max(1, min(TOPK_MAX, (past + n_tok) // 4))

    wb = w_branch[0].astype(BF16)
    w_out_b = w_out[0].astype(BF16)
    w_in0 = jnp.swapaxes(w_in[0], 0, 1)

    mp = bp * seq
    tm_p = min(1024, seq)
    xp = x_prompt.reshape(mp, d)
    xn_p = _rmsnorm_bf16(xp, norm_g[0], min(512, seq))
    pp = _projections(xn_p, w_in0, jnp.arange(seq, dtype=I32), seq // tm_p, tm_p, idx_ln_w[0], idx_ln_b[0], dt_bias[0])
    y_att_p = _prompt_attention(pp, bp, seq, top_k_p)
    y_ssd_p, ssm_p, conv_p = _ssd(pp["xbc"], pp["z_ssd"], pp["dt"], jnp.zeros((bp, SUBLANES, conv_dim), F32), None,
                                  conv_w[0], conv_b[0], a_log[0], d_skip[0], ssd_norm_g[0], bp, seq)
    merged_p = _merge(y_att_p, y_ssd_p, wb, pp["gates"], min(512, seq))
    y_p = _out_proj(merged_p, w_out_b, xp, final_norm_g, min(512, seq))

    ms_rows = bs * n_tok
    rows_pad = _round_up(ms_rows, LANES)
    xs = x_sample.reshape(ms_rows, d)
    xs_pad = jnp.pad(xs, ((0, rows_pad - ms_rows), (0, 0))) if rows_pad != ms_rows else xs
    pos_s = past + (jnp.arange(rows_pad, dtype=I32) % n_tok)
    xn_s = _rmsnorm_bf16(xs_pad, norm_g[0], rows_pad)
    ps = _projections(xn_s, w_in0, pos_s, 1, rows_pad, idx_ln_w[0], idx_ln_b[0], dt_bias[0])
    real = lambda a: a[:ms_rows]

    qi_rows = jnp.transpose(ps["qi"][:, :ms_rows], (1, 0, 2)).reshape(bs, n_tok * IDX_HEADS, IDX_DIM)
    wi_col = real(ps["wi"]).reshape(bs, n_tok * IDX_HEADS, 1)
    scores = _sample_scores(qi_rows, wi_col, jnp.swapaxes(cache_idx_k[0], 1, 2), page_table, n_tok)
    scores = scores.reshape(ms_rows, past)
    if rows_pad != ms_rows:
        scores = jnp.pad(scores, ((0, rows_pad - ms_rows), (0, 0)))
    keep, keep_new = _sample_select(scores, ps["qi"], ps["ki_bf"], ps["wi"], n_tok, ms_rows, top_k_s)
    keep = real(keep).reshape(bs, n_tok, past * KV_HEADS)
    keep_new = real(keep_new).reshape(bs, n_tok, rows_pad * KV_HEADS)
    q_rows = jnp.transpose(ps["q"][:, :ms_rows], (1, 0, 2)).reshape(bs, n_tok * ATT_HEADS, HEAD_DIM)
    z_rows = real(ps["z_att"]).reshape(bs, n_tok * ATT_HEADS, HEAD_DIM)
    k_new = ps["k"].reshape(rows_pad * KV_HEADS, HEAD_DIM)
    v_new = ps["v"].reshape(rows_pad * KV_HEADS, HEAD_DIM)
    ck = cache_k[0].reshape(n_phys, page * KV_HEADS, HEAD_DIM)
    cv = cache_v[0].reshape(n_phys, page * KV_HEADS, HEAD_DIM)
    y_att_s = _sample_attention(q_rows, z_rows, keep, keep_new, k_new, v_new, ck, cv, page_table, n_tok)
    y_att_s = y_att_s.reshape(ms_rows, att_w)

    conv0_s = jnp.pad(state_conv[0], ((0, 0), (SUBLANES - (CONV_W - 1), 0), (0, 0)))
    y_ssd_s, ssm_s, conv_s = _ssd(real(ps["xbc"]), real(ps["z_ssd"]), real(ps["dt"]), conv0_s, state_ssm[0],
                                  conv_w[0], conv_b[0], a_log[0], d_skip[0], ssd_norm_g[0], bs, n_tok)
    if rows_pad != ms_rows:
        padr = lambda a: jnp.pad(a, ((0, rows_pad - ms_rows), (0, 0)))
        y_att_s, y_ssd_s = padr(y_att_s), padr(y_ssd_s)
    merged_s = _merge(y_att_s, y_ssd_s, wb, ps["gates"], rows_pad)
    y_s = real(_out_proj(merged_s, w_out_b, xs_pad, final_norm_g, rows_pad))

    kv5 = lambda a, b, t: a.reshape(1, b, t, KV_HEADS, HEAD_DIM)
    return (y_p.reshape(bp, seq, d), y_s.reshape(bs, n_tok, d),
            kv5(pp["k"], bp, seq), kv5(pp["v"], bp, seq), pp["ki"].reshape(1, bp, seq, IDX_DIM),
            ssm_p[None], conv_p[None],
            kv5(real(ps["k"]), bs, n_tok), kv5(real(ps["v"]), bs, n_tok), real(ps["ki"]).reshape(1, bs, n_tok, IDX_DIM),
            ssm_s[None], conv_s[None])
```

```python
import functools

import numpy as np
import jax
import jax.numpy as jnp
from jax import lax
from jax.experimental import pallas as pl
from jax.experimental.pallas import tpu as pltpu

F32, BF16, I32 = jnp.float32, jnp.bfloat16, jnp.int32

ATT_HEADS = 16
KV_HEADS = 4
HEAD_DIM = 128
IDX_HEADS = 16
IDX_DIM = 64
TOPK_MAX = 256
ROPE_THETA = 10000.0
SSD_HEAD_DIM = 64
SSD_GROUPS = 8
D_STATE = 128
CONV_W = 4
CHUNK = 128
EPS = 1e-6

LANES = 128
SUBLANES = 8
BF16_ROWS = 16
VMEM_LIMIT = 56 * 1024 * 1024
INT_MIN = -2 ** 31
KEY_OF_MOST_NEGATIVE_FLOAT = INT_MIN + 0x00800000
MASKED = -1e30
LOG2E = 1.4426950408889634
PAGES_PER_STEP = 32
SCORE_PAGES_PER_STEP = 32
GROUPS_PER_TRIP = 8


def _cparams(sem):
    return pltpu.CompilerParams(dimension_semantics=sem, vmem_limit_bytes=VMEM_LIMIT)


def _nt_dot(a, b):
    return lax.dot_general(a, b, (((1,), (1,)), ((), ())), preferred_element_type=F32)


def _tile_lanes(x, n):
    return x if n == 1 else jnp.concatenate([x] * n, axis=1)


def _split3(x):
    hi = x.astype(BF16)
    r1 = x - hi.astype(F32)
    mid = r1.astype(BF16)
    lo = (r1 - mid.astype(F32)).astype(BF16)
    return hi, mid, lo


def _dot_exact_rhs(x, e):
    hi, mid, lo = _split3(x)
    d = functools.partial(jnp.dot, preferred_element_type=F32)
    return d(hi, e) + d(mid, e) + d(lo, e)


def _dot_exact_lhs(e, x):
    hi, mid, lo = _split3(x)
    d = functools.partial(jnp.dot, preferred_element_type=F32)
    return d(e, hi) + d(e, mid) + d(e, lo)


def _norm_kernel(x_ref, g_ref, o_ref):
    x = x_ref[...]
    ms = jnp.mean(x * x, axis=-1, keepdims=True)
    o_ref[...] = (x * lax.rsqrt(ms + EPS) * g_ref[...]).astype(o_ref.dtype)


def _rmsnorm_bf16(x, g, tm):
    m, d = x.shape
    return pl.pallas_call(
        _norm_kernel, grid=(m // tm,),
        in_specs=[pl.BlockSpec((tm, d), lambda i: (i, 0)), pl.BlockSpec((1, d), lambda i: (0, 0))],
        out_specs=pl.BlockSpec((tm, d), lambda i: (i, 0)),
        out_shape=jax.ShapeDtypeStruct((m, d), BF16),
        compiler_params=_cparams(("parallel",)), name="rmsnorm")(x, g.reshape(1, d))


def _mm_kernel(x_ref, w_ref, *rest, epilogue, n_aux):
    acc = _nt_dot(x_ref[...], w_ref[...].astype(BF16))
    epilogue(acc, rest[:n_aux], rest[n_aux:])


def _matmul(x, wt, row0, n, epilogue, aux, aux_specs, out_shapes, out_specs, tm, tn, name):
    m, k = x.shape
    kern = functools.partial(_mm_kernel, epilogue=epilogue, n_aux=len(aux))
    return pl.pallas_call(
        kern, grid=(m // tm, n // tn),
        in_specs=[pl.BlockSpec((tm, k), lambda i, j: (i, 0)),
                  pl.BlockSpec((pl.Element(tn), pl.Element(k)),
                               lambda i, j: (pl.multiple_of(row0 + j * tn, SUBLANES), 0))] + list(aux_specs),
        out_specs=out_specs, out_shape=out_shapes,
        compiler_params=_cparams(("parallel", "parallel")), name=name)(x, wt, *aux)


def _rope_half(x, cos, sin_signed, half):
    if 2 * half == LANES:
        partner = pltpu.roll(x, half, 1)
    else:
        partner = jnp.concatenate([x[:, half:], x[:, :half]], axis=1)
    return x * cos + partner * sin_signed


def _store_heads(outs, c, r, head_major, token_major_first):
    n_heads = outs[0].shape[0] // r.shape[0] if token_major_first else None
    for idx, o in enumerate(outs):
        if token_major_first and idx == 0:
            o[pl.ds(c, r.shape[0], stride=n_heads), :] = r.astype(o.dtype)
        elif head_major:
            o[c] = r.astype(o.dtype)
        else:
            o[:, c * HEAD_DIM:(c + 1) * HEAD_DIM] = r.astype(o.dtype)


def _epi_rope128(acc, aux, outs, post_scale=None, head_major=False, token_major_first=False):
    cos, sin = aux[0][...], aux[1][...]
    for c in range(acc.shape[1] // HEAD_DIM):
        r = _rope_half(acc[:, c * HEAD_DIM:(c + 1) * HEAD_DIM], cos, sin, HEAD_DIM // 2)
        if post_scale is not None:
            r = r * post_scale
        _store_heads(outs, c, r, head_major, token_major_first)


def _epi_heads(acc, aux, outs):
    for c in range(acc.shape[1] // HEAD_DIM):
        _store_heads(outs, c, acc[:, c * HEAD_DIM:(c + 1) * HEAD_DIM], False, True)


def _epi_store(acc, aux, outs, fn=None):
    val = acc if fn is None else fn(acc)
    for o in outs:
        o[...] = val.astype(o.dtype)


def _epi_qi(acc, aux, outs):
    cos, sin = aux[0][...], aux[1][...]
    for hh in range(acc.shape[1] // IDX_DIM):
        x = acc[:, hh * IDX_DIM:(hh + 1) * IDX_DIM]
        outs[0][hh] = _rope_half(x, cos, sin, IDX_DIM // 2).astype(outs[0].dtype)


def _epi_kiwi(acc, aux, outs, idx_scale):
    lnw, lnb, cos, sin = (a[...] for a in aux)
    ki = acc[:, :IDX_DIM]
    mu = jnp.mean(ki, axis=-1, keepdims=True)
    kc = ki - mu
    y = kc * lax.rsqrt(jnp.mean(kc * kc, axis=-1, keepdims=True) + EPS) * lnw + lnb
    r = _rope_half(y, cos, sin, IDX_DIM // 2)
    outs[0][...] = r
    outs[1][...] = r.astype(BF16)
    outs[2][...] = acc[:, IDX_DIM:] * idx_scale


def _epi_softplus(acc, aux, outs):
    x = acc + aux[0][...]
    outs[0][...] = jnp.maximum(x, 0.0) + jnp.log1p(jnp.exp(-jnp.abs(x)))


def _silu(x):
    return x * jax.nn.sigmoid(x)


def _rope_tables(pos, d):
    inv = ROPE_THETA ** (-jnp.arange(0, d, 2, dtype=F32) / d)
    ang = pos.astype(F32)[:, None] * inv[None, :]
    cos, sin = jnp.cos(ang), jnp.sin(ang)
    return jnp.concatenate([cos, cos], axis=-1), jnp.concatenate([-sin, sin], axis=-1)


def _projections(xn, wt, pos_rows, n_pos_blocks, tm, idx_ln_w, idx_ln_b, dt_bias):
    m, d = xn.shape
    att_w, kv_w = ATT_HEADS * HEAD_DIM, KV_HEADS * HEAD_DIM
    d_inner = 2 * d
    conv_dim = d_inner + 2 * SSD_GROUPS * D_STATE
    ssd_heads = d_inner // SSD_HEAD_DIM
    splits = (att_w, kv_w, kv_w, att_w, IDX_HEADS * IDX_DIM, IDX_DIM, IDX_HEADS, d_inner, conv_dim,
              ssd_heads, d, d)
    assert sum(splits) == wt.shape[0]
    off = np.concatenate([[0], np.cumsum(splits)]).astype(int)
    assert all(o % BF16_ROWS == 0 for o in off), "segment rows of the weight must start on packed-row boundaries"
    seg = lambda a, b: (wt, int(off[a]), int(off[b] - off[a]))

    cos128, sin128 = _rope_tables(pos_rows, HEAD_DIM)
    cos64, sin64 = _rope_tables(pos_rows, IDX_DIM)
    pos_map = lambda i, j: (i % n_pos_blocks, 0)
    t128 = [pl.BlockSpec((tm, HEAD_DIM), pos_map)] * 2
    t64 = [pl.BlockSpec((tm, IDX_DIM), pos_map)] * 2
    tile = lambda tn: pl.BlockSpec((tm, tn), lambda i, j: (i, j))
    sds = lambda n, dt: jax.ShapeDtypeStruct((m, n), dt)
    tn = 512
    tw = 1024

    (q,) = _matmul(xn, *seg(0, 1), functools.partial(_epi_rope128, post_scale=HEAD_DIM ** -0.5 * LOG2E, head_major=True),
                   [cos128, sin128], t128, [jax.ShapeDtypeStruct((ATT_HEADS, m, HEAD_DIM), BF16)],
                   [pl.BlockSpec((tw // HEAD_DIM, tm, HEAD_DIM), lambda i, j: (j, i, 0))], tm, tw, "proj_q")
    kv_shapes = [jax.ShapeDtypeStruct((m * KV_HEADS, HEAD_DIM), F32), sds(kv_w, BF16)]
    kv_specs = [pl.BlockSpec((tm * KV_HEADS, HEAD_DIM), lambda i, j: (i, 0)), tile(kv_w)]
    k, k_bf = _matmul(xn, *seg(1, 2), functools.partial(_epi_rope128, token_major_first=True), [cos128, sin128], t128,
                      kv_shapes, kv_specs, tm, kv_w, "proj_k")
    v, v_bf = _matmul(xn, *seg(2, 3), _epi_heads, [], [], kv_shapes, kv_specs, tm, kv_w, "proj_v")
    (z_att,) = _matmul(xn, *seg(3, 4), functools.partial(_epi_store, fn=_silu), [], [],
                       [sds(att_w, BF16)], [tile(tw)], tm, tw, "proj_zatt")
    qi_tn = 4 * IDX_DIM
    (qi,) = _matmul(xn, *seg(4, 5), _epi_qi, [cos64, sin64], t64,
                    [jax.ShapeDtypeStruct((IDX_HEADS, m, IDX_DIM), BF16)],
                    [pl.BlockSpec((4, tm, IDX_DIM), lambda i, j: (j, i, 0))], tm, qi_tn, "proj_qi")
    kiwi_n = IDX_DIM + IDX_HEADS
    idx_scale = IDX_HEADS ** -0.5 * IDX_DIM ** -0.5
    row1 = lambda n: pl.BlockSpec((1, n), lambda i, j: (0, 0))
    ki, ki_bf, wi = _matmul(
        xn, *seg(5, 7), functools.partial(_epi_kiwi, idx_scale=idx_scale),
        [idx_ln_w.reshape(1, IDX_DIM), idx_ln_b.reshape(1, IDX_DIM), cos64, sin64],
        [row1(IDX_DIM), row1(IDX_DIM)] + t64,
        [sds(IDX_DIM, F32), sds(IDX_DIM, BF16), sds(IDX_HEADS, F32)],
        [pl.BlockSpec((tm, IDX_DIM), lambda i, j: (i, 0))] * 2 + [pl.BlockSpec((tm, IDX_HEADS), lambda i, j: (i, 0))],
        tm, kiwi_n, "proj_kiwi")
    (z_ssd,) = _matmul(xn, *seg(7, 8), functools.partial(_epi_store, fn=_silu), [], [],
                       [sds(d_inner, BF16)], [tile(tw)], tm, tw, "proj_zssd")
    (xbc,) = _matmul(xn, *seg(8, 9), _epi_store, [], [], [sds(conv_dim, F32)], [tile(tw)], tm, tw, "proj_xbc")
    (dt,) = _matmul(xn, *seg(9, 10), _epi_softplus, [dt_bias.reshape(1, ssd_heads)], [row1(ssd_heads)],
                    [sds(ssd_heads, F32)], [pl.BlockSpec((tm, ssd_heads), lambda i, j: (i, 0))],
                    tm, ssd_heads, "proj_dt")
    (gates,) = _matmul(xn, *seg(10, 12), functools.partial(_epi_store, fn=jax.nn.sigmoid), [], [],
                       [sds(2 * d, BF16)], [tile(tw)], tm, tw, "proj_gates")
    return dict(q=q, k=k, k_bf=k_bf, v=v, v_bf=v_bf, z_att=z_att, qi=qi, ki=ki, ki_bf=ki_bf, wi=wi,
                z_ssd=z_ssd, xbc=xbc, dt=dt, gates=gates)


def _key_to_float(key):
    return pltpu.bitcast(jnp.where(key < 0, key ^ jnp.int32(0x7FFFFFFF), key), F32)


def _kth_largest(count_ge, shape, top_k):
    kf = jnp.float32(top_k)
    zero = jnp.zeros(shape, I32)
    c0 = jnp.broadcast_to(count_ge(_key_to_float(zero)), shape)
    t0 = jnp.where(c0 >= kf, zero, jnp.full(shape, INT_MIN, I32))
    n0 = jnp.where(c0 >= kf, c0, 0.0)

    def body(i, tn):
        t, n = tn
        cand = t | lax.shift_left(jnp.int32(1), jnp.int32(30) - i)
        c = jnp.broadcast_to(count_ge(_key_to_float(cand)), shape)
        return jnp.where(c >= kf, cand, t), jnp.where(c >= kf, c, n)

    t, n = lax.fori_loop(0, 31, body, (t0, n0))
    return _key_to_float(jnp.maximum(t, jnp.int32(KEY_OF_MOST_NEGATIVE_FLOAT))), n


def _lane_tile_sum(x):
    part = x[:, :LANES]
    for c in range(1, x.shape[1] // LANES):
        part = part + x[:, c * LANES:(c + 1) * LANES]
    return part


def _attn_kernel(q_ref, qi_ref, wi_ref, z_ref, k_ref, v_ref, ki_ref, o_ref,
                 score_ref, wexp_ref, bias_ref, s_ref, m_ref, l_ref, acc_ref, *, tq, tk, tkc, top_k):
    qt = pl.program_id(1)
    n_kc = ((qt + 1) * tq + tk - 1) // tk
    nl = tk // LANES
    row_pos = qt * tq + lax.broadcasted_iota(I32, (tq, 1), 0)

    wi = wi_ref[...]
    for h in range(IDX_HEADS):
        wexp_ref[h] = jnp.broadcast_to(wi[:, h:h + 1], (tq, LANES))

    def score_chunk(kc, carry):
        off = pl.multiple_of(kc * tk, tk)
        ki = ki_ref[pl.ds(off, tk), :]
        d = _nt_dot(qi_ref[...].reshape(IDX_HEADS * tq, IDX_DIM), ki)
        r = jnp.maximum(d, 0.0) * _tile_lanes(wexp_ref[...].reshape(IDX_HEADS * tq, LANES), nl)
        sc = r[0:tq]
        for h in range(1, IDX_HEADS):
            sc = sc + r[h * tq:(h + 1) * tq]
        kpos = off + lax.broadcasted_iota(I32, (1, tk), 1)
        score_ref[:, pl.ds(off, tk)] = jnp.where(kpos <= row_pos, sc, -jnp.inf)
        return carry

    lax.fori_loop(0, n_kc, score_chunk, 0)

    def count(cand, strict=False):
        cand_t = _tile_lanes(cand, nl)

        def body(kc, cnt):
            sc = score_ref[:, pl.ds(pl.multiple_of(kc * tk, tk), tk)]
            hit = jnp.where((sc > cand_t) if strict else (sc >= cand_t), 1.0, 0.0)
            return cnt + _lane_tile_sum(hit)

        cnt = lax.fori_loop(0, n_kc, body, jnp.zeros((tq, LANES), F32))
        return jnp.sum(cnt, axis=1, keepdims=True)

    thr1, n_ge = _kth_largest(count, (tq, LANES), top_k)
    thr = _tile_lanes(thr1, nl)
    has_ties = jnp.max(n_ge) > jnp.float32(top_k)

    @pl.when(jnp.logical_not(has_ties))
    def _():
        def bias_chunk(kc, carry):
            off = pl.multiple_of(kc * tk, tk)
            bias_ref[:, pl.ds(off, tk)] = jnp.where(score_ref[:, pl.ds(off, tk)] >= thr, 0.0, MASKED)
            return carry

        lax.fori_loop(0, n_kc, bias_chunk, 0)

    @pl.when(has_ties)
    def _():
        room = jnp.float32(top_k) - count(thr1, strict=True)
        ri = lax.broadcasted_iota(I32, (tk, tk), 0)
        ci = lax.broadcasted_iota(I32, (tk, tk), 1)
        prefix = jnp.where(ri <= ci, 1.0, 0.0).astype(BF16)

        def bias_chunk(kc, seen):
            off = pl.multiple_of(kc * tk, tk)
            key = score_ref[:, pl.ds(off, tk)]
            eq = jnp.where(key == thr, 1.0, 0.0)
            rank = seen + jnp.dot(eq.astype(BF16), prefix, preferred_element_type=F32)
            keep = (key > thr) | ((key == thr) & (rank <= room))
            bias_ref[:, pl.ds(off, tk)] = jnp.where(keep, 0.0, MASKED)
            return seen + jnp.sum(eq, axis=1, keepdims=True)

        lax.fori_loop(0, n_kc, bias_chunk, jnp.zeros((tq, 1), F32))

    n_kcc = ((qt + 1) * tq + tkc - 1) // tkc
    nlc = tkc // LANES

    def bias_tail(kc, carry):
        bias_ref[:, pl.ds(pl.multiple_of(kc * tk, tk), tk)] = jnp.full((tq, tk), MASKED, F32)
        return carry

    lax.fori_loop(n_kc, n_kcc * (tkc // tk), bias_tail, 0)
    rep = ATT_HEADS // KV_HEADS

    rows = rep * tq

    def logits_chunk(g, ks):
        slot = g % 2
        gs = slice(g * HEAD_DIM, (g + 1) * HEAD_DIM)
        qg = q_ref[g * rep:(g + 1) * rep].reshape(rows, HEAD_DIM)
        bias = bias_ref[:, ks]
        s = jnp.concatenate([bias] * rep, axis=0) + _nt_dot(qg, k_ref[ks, gs])
        s_ref[slot, :, ks] = s
        part = s[:, :LANES]
        for c in range(1, nlc):
            part = jnp.maximum(part, s[:, c * LANES:(c + 1) * LANES])
        m_ref[slot] = jnp.maximum(m_ref[slot], part)

    def pv_chunk(g, ks):
        slot = g % 2
        gs = slice(g * HEAD_DIM, (g + 1) * HEAD_DIM)
        p = jnp.exp2(s_ref[slot, :, ks] - _tile_lanes(m_ref[slot], nlc))
        l_ref[...] += _lane_tile_sum(p)
        acc_ref[...] += jnp.dot(p.astype(BF16), v_ref[ks, gs], preferred_element_type=F32)

    for stage in range(KV_HEADS + 1):
        g_logits = stage if stage < KV_HEADS else None
        g_pv = stage - 1 if stage > 0 else None
        if g_logits is not None:
            m_ref[g_logits % 2] = jnp.full((rows, LANES), MASKED, F32)
        if g_pv is not None:
            slot = g_pv % 2
            m_ref[slot] = jnp.broadcast_to(jnp.max(m_ref[slot], axis=1, keepdims=True), (rows, LANES))
            l_ref[...] = jnp.zeros((rows, LANES), F32)
            acc_ref[...] = jnp.zeros((rows, HEAD_DIM), F32)

        def stage_chunk(kc, carry, g_logits=g_logits, g_pv=g_pv):
            ks = pl.ds(pl.multiple_of(kc * tkc, tkc), tkc)
            if g_logits is not None:
                logits_chunk(g_logits, ks)
            if g_pv is not None:
                pv_chunk(g_pv, ks)
            return carry

        lax.fori_loop(0, n_kcc, stage_chunk, 0)
        if g_pv is not None:
            o = acc_ref[...] / jnp.sum(l_ref[...], axis=1, keepdims=True)
            for r in range(rep):
                hs = slice((g_pv * rep + r) * HEAD_DIM, (g_pv * rep + r + 1) * HEAD_DIM)
                o_ref[:, hs] = (o[r * tq:(r + 1) * tq] * z_ref[:, hs].astype(F32)).astype(o_ref.dtype)


def _prompt_attention(p, batch, seq, top_k):
    m = batch * seq
    tq = min(128, seq)
    tk = min(512, seq)
    nq = seq // tq
    att_w, kv_w = ATT_HEADS * HEAD_DIM, KV_HEADS * HEAD_DIM
    rep = ATT_HEADS // KV_HEADS
    rows = lambda w: pl.BlockSpec((tq, w), lambda b, t: (b * nq + t, 0))
    whole = lambda w: pl.BlockSpec((seq, w), lambda b, t: (b, 0))
    tkc = min(1024, seq)
    kern = functools.partial(_attn_kernel, tq=tq, tk=tk, tkc=tkc, top_k=top_k)
    return pl.pallas_call(
        kern, grid=(batch, nq),
        in_specs=[pl.BlockSpec((ATT_HEADS, tq, HEAD_DIM), lambda b, t: (0, b * nq + t, 0)),
                  pl.BlockSpec((IDX_HEADS, tq, IDX_DIM), lambda b, t: (0, b * nq + t, 0)),
                  rows(IDX_HEADS), rows(att_w), whole(kv_w), whole(kv_w), whole(IDX_DIM)],
        out_specs=rows(att_w),
        out_shape=jax.ShapeDtypeStruct((m, att_w), BF16),
        scratch_shapes=[pltpu.VMEM((tq, seq), F32),
                        pltpu.VMEM((IDX_HEADS, tq, LANES), F32),
                        pltpu.VMEM((tq, seq), F32),
                        pltpu.VMEM((2, rep * tq, seq), F32),
                        pltpu.VMEM((2, rep * tq, LANES), F32),
                        pltpu.VMEM((rep * tq, LANES), F32),
                        pltpu.VMEM((rep * tq, HEAD_DIM), F32)],
        compiler_params=_cparams(("parallel", "arbitrary")), name="prompt_attention",
    )(p["q"], p["qi"], p["wi"], p["z_att"], p["k_bf"], p["v_bf"], p["ki_bf"])


def _sample_score_kernel(pt_ref, qi_ref, wi_ref, *rest, n_pages, n_tok):
    page_refs, o_ref = rest[:n_pages], rest[n_pages]
    qi = qi_ref[...]
    w = jnp.broadcast_to(wi_ref[...], (qi.shape[0], LANES))
    for k in range(n_pages):
        kit = page_refs[k][...].astype(BF16)
        page = kit.shape[1]
        r = jnp.maximum(jnp.dot(qi, kit, preferred_element_type=F32), 0.0) * _tile_lanes(w, page // LANES)
        sc = jnp.sum(r.reshape(n_tok, IDX_HEADS, page), axis=1)
        o_ref[:, k * page:(k + 1) * page] = sc


def _sample_scores(qi_rows, wi_col, cache_idx_kt, page_table, n_tok):
    bsz, n_pages_total = page_table.shape
    page = cache_idx_kt.shape[2]
    g = SCORE_PAGES_PER_STEP if n_pages_total % SCORE_PAGES_PER_STEP == 0 else 1
    rows = n_tok * IDX_HEADS
    page_spec = lambda k: pl.BlockSpec((None, IDX_DIM, page), lambda b, s, pt: (pt[b, s * g + k], 0, 0))
    grid_spec = pltpu.PrefetchScalarGridSpec(
        num_scalar_prefetch=1, grid=(bsz, n_pages_total // g),
        in_specs=[pl.BlockSpec((None, rows, IDX_DIM), lambda b, s, pt: (b, 0, 0)),
                  pl.BlockSpec((None, rows, 1), lambda b, s, pt: (b, 0, 0))] + [page_spec(k) for k in range(g)],
        out_specs=pl.BlockSpec((None, n_tok, g * page), lambda b, s, pt: (b, 0, s)))
    kern = functools.partial(_sample_score_kernel, n_pages=g, n_tok=n_tok)
    return pl.pallas_call(
        kern, grid_spec=grid_spec,
        out_shape=jax.ShapeDtypeStruct((bsz, n_tok, n_pages_total * page), F32),
        compiler_params=_cparams(("parallel", "arbitrary")), name="sample_scores",
    )(page_table, qi_rows, wi_col, *([cache_idx_kt] * g))


def _sample_select_kernel(sc_ref, qi_ref, ki_ref, wi_ref, rexp_ref, keep_ref, keep_new_ref, *, n_tok, n_rows, top_k):
    rows, past = sc_ref.shape
    nk = ki_ref.shape[0]
    wi = wi_ref[...]
    ki = ki_ref[...]
    sc_new = jnp.zeros((rows, nk), F32)
    for h in range(IDX_HEADS):
        w = jnp.broadcast_to(wi[:, h:h + 1], (rows, LANES))
        sc_new = sc_new + jnp.maximum(_nt_dot(qi_ref[h], ki), 0.0) * _tile_lanes(w, nk // LANES)
    r = lax.broadcasted_iota(I32, (rows, nk), 0)
    c = lax.broadcasted_iota(I32, (rows, nk), 1)
    ok = (r // n_tok == c // n_tok) & (c % n_tok <= r % n_tok) & (r < n_rows) & (c < n_rows)
    sc_new = jnp.where(ok, sc_new, -jnp.inf)

    def count(cand, strict=False):
        hit = lambda sc, n: jnp.where((sc > _tile_lanes(cand, n)) if strict else (sc >= _tile_lanes(cand, n)), 1.0, 0.0)
        return jnp.sum(_lane_tile_sum(hit(sc_ref[...], past // LANES)) + _lane_tile_sum(hit(sc_new, nk // LANES)),
                       axis=1, keepdims=True)

    thr, _ = _kth_largest(count, (rows, LANES), top_k)
    rexp = rexp_ref[...]
    xw = rexp.shape[1]
    room = jnp.float32(top_k) - count(thr, strict=True)
    ri = lax.broadcasted_iota(I32, (LANES, LANES), 0)
    ci = lax.broadcasted_iota(I32, (LANES, LANES), 1)
    prefix = jnp.where(ri <= ci, 1.0, 0.0).astype(BF16)

    def select(key_tile, seen):
        eq = jnp.where(key_tile == thr, 1.0, 0.0)
        rank = seen + jnp.dot(eq.astype(BF16), prefix, preferred_element_type=F32)
        keep = (key_tile > thr) | ((key_tile == thr) & (rank <= room))
        sel = jnp.where(keep, 1.0, 0.0).astype(BF16)
        return (jnp.dot(sel, rexp, preferred_element_type=F32).astype(BF16),
                seen + jnp.sum(eq, axis=1, keepdims=True))

    def past_tile(ct, seen):
        src = pl.ds(pl.multiple_of(ct * LANES, LANES), LANES)
        kept, seen = select(sc_ref[:, src], seen)
        keep_ref[:, pl.ds(pl.multiple_of(ct * xw, xw), xw)] = kept
        return seen

    seen = lax.fori_loop(0, past // LANES, past_tile, jnp.zeros((rows, 1), F32))
    for ct in range(nk // LANES):
        kept, seen = select(sc_new[:, ct * LANES:(ct + 1) * LANES], seen)
        keep_new_ref[:, ct * xw:(ct + 1) * xw] = kept


def _sample_select(scores, qi, ki_new_bf, wi, n_tok, n_rows, top_k):
    rows, past = scores.shape
    nk = ki_new_bf.shape[0]
    rexp = (jnp.arange(LANES * KV_HEADS)[None, :] // KV_HEADS == jnp.arange(LANES)[:, None]).astype(BF16)
    kern = functools.partial(_sample_select_kernel, n_tok=n_tok, n_rows=n_rows, top_k=top_k)
    full = lambda shape: pl.BlockSpec(shape, lambda i: (0,) * len(shape))
    return pl.pallas_call(
        kern, grid=(1,),
        in_specs=[full((rows, past)), full((IDX_HEADS, rows, IDX_DIM)), full((nk, IDX_DIM)), full((rows, IDX_HEADS)),
                  full(rexp.shape)],
        out_specs=[full((rows, past * KV_HEADS)), full((rows, nk * KV_HEADS))],
        out_shape=[jax.ShapeDtypeStruct((rows, past * KV_HEADS), BF16),
                   jax.ShapeDtypeStruct((rows, nk * KV_HEADS), BF16)],
        compiler_params=_cparams(("arbitrary",)), name="sample_select")(scores, qi, ki_new_bf, wi, rexp)


def _sample_attn_kernel(pt_ref, q_ref, z_ref, keep_ref, keep_new_ref, knew_ref, vnew_ref, *rest, n_pages, n_tok, page):
    k_refs, v_refs = rest[:n_pages], rest[n_pages:2 * n_pages]
    o_ref, m_ref, l_ref, acc_ref, s_ref = rest[2 * n_pages:]
    step = pl.program_id(1)
    rows = n_tok * ATT_HEADS
    cols = page * KV_HEADS
    rep = ATT_HEADS // KV_HEADS
    nl = cols // LANES

    @pl.when(step == 0)
    def _():
        m_ref[...] = jnp.full(m_ref.shape, MASKED, F32)
        l_ref[...] = jnp.zeros(l_ref.shape, F32)
        acc_ref[...] = jnp.zeros(acc_ref.shape, F32)

    q = q_ref[...]
    rr = lax.broadcasted_iota(I32, (rows, cols), 0)
    cc = lax.broadcasted_iota(I32, (rows, cols), 1)
    head_bias = jnp.where((cc % KV_HEADS) == ((rr % ATT_HEADS) // rep), 0.0, MASKED)

    def update(blocks):
        mx = None
        for i, (load_k, _, load_keep) in enumerate(blocks):
            kp = load_k().astype(BF16)
            kb = (load_keep().astype(F32) - 1.0) * (-MASKED)
            kb_rows = jnp.concatenate(
                [jnp.broadcast_to(kb[t:t + 1, :], (ATT_HEADS, cols)) for t in range(n_tok)], axis=0)
            s = _nt_dot(q, kp) + (kb_rows + head_bias)
            s_ref[:, i * cols:(i + 1) * cols] = s
            part = s[:, :LANES]
            for c in range(1, nl):
                part = jnp.maximum(part, s[:, c * LANES:(c + 1) * LANES])
            mx = part if mx is None else jnp.maximum(mx, part)
        m_old = m_ref[...]
        m_new = jnp.maximum(m_old, jnp.max(mx, axis=1, keepdims=True))
        alpha = jnp.exp2(m_old - m_new)
        m_t = _tile_lanes(m_new, nl)
        lsum = jnp.zeros((rows, LANES), F32)
        pv = jnp.zeros((rows, HEAD_DIM), F32)
        for i, (_, load_v, _) in enumerate(blocks):
            p = jnp.exp2(s_ref[:, i * cols:(i + 1) * cols] - m_t)
            lsum = lsum + _lane_tile_sum(p)
            pv = pv + jnp.dot(p.astype(BF16), load_v().astype(BF16), preferred_element_type=F32)
        l_ref[...] = alpha * l_ref[...] + lsum
        acc_ref[...] = alpha * acc_ref[...] + pv
        m_ref[...] = m_new

    update([(lambda k=k: k_refs[k][...], lambda k=k: v_refs[k][...], lambda k=k: keep_ref[:, k * cols:(k + 1) * cols])
            for k in range(n_pages)])

    @pl.when(step == pl.num_programs(1) - 1)
    def _():
        update([(lambda c=c: knew_ref[c * cols:(c + 1) * cols, :], lambda c=c: vnew_ref[c * cols:(c + 1) * cols, :],
                 lambda c=c: keep_new_ref[:, c * cols:(c + 1) * cols]) for c in range(knew_ref.shape[0] // cols)])
        lsum = jnp.sum(l_ref[...], axis=1, keepdims=True)
        o_ref[...] = (acc_ref[...] / lsum * z_ref[...].astype(F32)).astype(o_ref.dtype)


def _sample_attention(q_rows, z_rows, keep, keep_new, k_new, v_new, cache_k, cache_v, page_table, n_tok):
    bsz, n_pages_total = page_table.shape
    cols = cache_k.shape[1]
    page = cols // KV_HEADS
    g = PAGES_PER_STEP if n_pages_total % PAGES_PER_STEP == 0 else 1
    rows = n_tok * ATT_HEADS
    per_b = lambda shape: pl.BlockSpec((None,) + shape, lambda b, s, pt: (b,) + (0,) * len(shape))
    const = lambda shape: pl.BlockSpec(shape, lambda b, s, pt: (0,) * len(shape))
    page_spec = lambda k: pl.BlockSpec((None, cols, HEAD_DIM), lambda b, s, pt: (pt[b, s * g + k], 0, 0))
    grid_spec = pltpu.PrefetchScalarGridSpec(
        num_scalar_prefetch=1, grid=(bsz, n_pages_total // g),
        in_specs=[per_b((rows, HEAD_DIM)), per_b((rows, HEAD_DIM)),
                  pl.BlockSpec((None, n_tok, g * cols), lambda b, s, pt: (b, 0, s)),
                  per_b((n_tok, keep_new.shape[-1])), const(k_new.shape), const(v_new.shape)]
                 + [page_spec(k) for k in range(g)] * 2,
        out_specs=per_b((rows, HEAD_DIM)),
        scratch_shapes=[pltpu.VMEM((rows, LANES), F32), pltpu.VMEM((rows, LANES), F32),
                        pltpu.VMEM((rows, HEAD_DIM), F32), pltpu.VMEM((rows, g * cols), F32)])
    kern = functools.partial(_sample_attn_kernel, n_pages=g, n_tok=n_tok, page=page)
    return pl.pallas_call(
        kern, grid_spec=grid_spec,
        out_shape=jax.ShapeDtypeStruct((bsz, rows, HEAD_DIM), BF16),
        compiler_params=_cparams(("parallel", "arbitrary")), name="sample_attention",
    )(page_table, q_rows, z_rows, keep, keep_new, k_new, v_new, *([cache_k] * g), *([cache_v] * g))


def _ssd_kernel(*refs, t_rows, q_rows, has_state):
    if has_state:
        (xbc_ref, z_ref, dt_ref, conv0_ref, s0_ref, cw_ref, cb_ref, alog_ref, dskip_ref, ng_ref, e64_ref,
         y_ref, sfin_ref, cnew_ref, xp_ref, act_ref, dtp_ref, acst_ref, acsc_ref, st_ref) = refs
    else:
        (xbc_ref, z_ref, dt_ref, conv0_ref, cw_ref, cb_ref, alog_ref, dskip_ref, ng_ref, e64_ref,
         y_ref, sfin_ref, cnew_ref, xp_ref, act_ref, dtp_ref, acst_ref, acsc_ref, st_ref) = refs
        s0_ref = None
    c = pl.program_id(1)
    q = q_rows
    halo = SUBLANES
    n_heads = dt_ref.shape[-1]
    d_inner = n_heads * SSD_HEAD_DIM
    gw = d_inner // SSD_GROUPS
    hpg = n_heads // SSD_GROUPS
    conv_dim = xbc_ref.shape[-1]

    @pl.when(c == 0)
    def _():
        xp_ref[0:halo, :] = conv0_ref[...]
        if t_rows < q:
            xp_ref[halo:, :] = jnp.zeros((q, conv_dim), F32)
            dtp_ref[...] = jnp.zeros(dtp_ref.shape, F32)
        for g in range(SSD_GROUPS):
            if has_state:
                st_ref[g] = s0_ref[g * hpg:(g + 1) * hpg].reshape(gw, D_STATE).T
            else:
                st_ref[g] = jnp.zeros((D_STATE, gw), F32)

    xp_ref[halo:halo + t_rows, :] = xbc_ref[...]
    dtp_ref[0:t_rows, :] = dt_ref[...]

    cblk = LANES
    for cbi in range(conv_dim // cblk):
        cs = slice(cbi * cblk, (cbi + 1) * cblk)
        acc = jnp.broadcast_to(cb_ref[:, cs], (q, cblk))
        for tap in range(CONV_W):
            lo = halo - (CONV_W - 1) + tap
            acc = acc + xp_ref[lo:lo + q, cs] * cw_ref[tap:tap + 1, cs]
        act_ref[:, cs] = _silu(acc)

    dt = dtp_ref[...]
    a = -jnp.exp(alog_ref[...])
    ri = lax.broadcasted_iota(I32, (q, q), 0)
    ci = lax.broadcasted_iota(I32, (q, q), 1)
    tril = ri >= ci
    a_cs = _dot_exact_lhs(jnp.where(tril, 1.0, 0.0).astype(BF16), dt * a)
    a_last = a_cs[q - 1:q, :]
    acst_ref[...] = a_cs.T
    for h in range(n_heads):
        acsc_ref[h] = jnp.broadcast_to(a_cs[:, h:h + 1], (q, LANES))
    dt_b = dt.astype(BF16)
    dte_b = jnp.exp(a_last - a_cs).astype(BF16)
    ea_hi, ea_mid, _ = _split3(jnp.exp(a_cs))
    pad = jnp.zeros((SUBLANES - 2, n_heads), F32)
    row_pieces = _split3(jnp.concatenate([jnp.exp(a_last), dskip_ref[...], pad], axis=0))
    mxu = functools.partial(jnp.dot, preferred_element_type=F32)
    lane = lax.broadcasted_iota(I32, (1, LANES), 1)
    lo_half = lane < SSD_HEAD_DIM

    def group_body(g, carry):
        xs = act_ref[:, pl.ds(pl.multiple_of(g * gw, gw), gw)]
        bm = act_ref[:, pl.ds(pl.multiple_of(d_inner + g * D_STATE, D_STATE), D_STATE)]
        cm = act_ref[:, pl.ds(pl.multiple_of(d_inner + SSD_GROUPS * D_STATE + g * D_STATE, D_STATE), D_STATE)]
        e64g = e64_ref[:, pl.ds(pl.multiple_of(g * gw, gw), gw)]
        dt_x, dte_x = mxu(dt_b, e64g), mxu(dte_b, e64g)
        ea_x = mxu(ea_hi, e64g) + mxu(ea_mid, e64g)
        rows_x = mxu(row_pieces[0], e64g) + mxu(row_pieces[1], e64g) + mxu(row_pieces[2], e64g)
        cdec_x, dskip_x = rows_x[0:1], rows_x[1:2]
        xdt = xs * dt_x
        xdt_b = xdt.astype(BF16)
        bm_b, cm_b = bm.astype(BF16), cm.astype(BF16)
        cb = _nt_dot(cm_b, bm_b)
        st = st_ref[g]
        y_off = jnp.dot(cm_b, st.astype(BF16), preferred_element_type=F32) * ea_x
        y_parts = []
        for pr in range(hpg // 2):
            xpair = xdt_b[:, pr * LANES:(pr + 1) * LANES]
            halves = (jnp.where(lo_half, xpair, jnp.zeros_like(xpair)), jnp.where(lo_half, jnp.zeros_like(xpair), xpair))
            yp = jnp.zeros((q, LANES), F32)
            for s in range(2):
                hl = 2 * pr + s
                seg = acsc_ref[g * hpg + hl][:, :q] - acst_ref[pl.ds(g * hpg + hl, 1), :]
                lmat = jnp.where(tril, jnp.exp(jnp.where(tril, seg, 0.0)), 0.0)
                yp = yp + jnp.dot((cb * lmat).astype(BF16), halves[s], preferred_element_type=F32)
            y_parts.append(yp)
        y = jnp.concatenate(y_parts, axis=1) + y_off + dskip_x * xs
        st_ref[g] = cdec_x * st + jnp.dot(bm.T.astype(BF16), (xdt * dte_x).astype(BF16), preferred_element_type=F32)
        gsl = pl.ds(pl.multiple_of(g * gw, gw), gw)
        yz = y[0:t_rows] * z_ref[:, gsl].astype(F32)
        ms = jnp.mean(yz * yz, axis=-1, keepdims=True)
        y_ref[:, gsl] = (yz * lax.rsqrt(ms + EPS) * ng_ref[:, gsl]).astype(y_ref.dtype)
        return carry

    def group_batch(i, carry):
        for u in range(GROUPS_PER_TRIP):
            group_body(GROUPS_PER_TRIP * i + u, carry)
        return carry

    lax.fori_loop(0, SSD_GROUPS // GROUPS_PER_TRIP, group_batch, 0)

    @pl.when(c == pl.num_programs(1) - 1)
    def _():
        cnew_ref[...] = xp_ref[halo + t_rows - (CONV_W - 1):halo + t_rows, :]
        for g in range(SSD_GROUPS):
            sfin_ref[g * hpg:(g + 1) * hpg] = st_ref[g].T.reshape(hpg, SSD_HEAD_DIM, D_STATE)

    if t_rows == q:
        @pl.when(c < pl.num_programs(1) - 1)
        def _():
            xp_ref[0:halo, :] = xp_ref[q:q + halo, :]


def _ssd(xbc, z, dt, conv0, s0, conv_w, conv_b, a_log, d_skip, norm_g, bsz, seq):
    conv_dim = xbc.shape[-1]
    n_heads = dt.shape[-1]
    d_inner = n_heads * SSD_HEAD_DIM
    t_rows = min(CHUNK, seq)
    n_chunks = seq // t_rows
    q_rows = CHUNK if t_rows == CHUNK else _round_up(t_rows, BF16_ROWS)
    has_state = s0 is not None
    hpg = n_heads // SSD_GROUPS
    e64 = (jnp.arange(d_inner)[None, :] // SSD_HEAD_DIM == jnp.arange(n_heads)[:, None]).astype(BF16)
    x3 = lambda a: a.reshape(bsz, seq, a.shape[-1])
    rows = lambda w: pl.BlockSpec((None, t_rows, w), lambda b, c: (b, c, 0))
    const = lambda shape: pl.BlockSpec(shape, lambda b, c: (0,) * len(shape))
    state_spec = pl.BlockSpec((None, n_heads, SSD_HEAD_DIM, D_STATE), lambda b, c: (b, 0, 0, 0))
    args = [x3(xbc), x3(z), x3(dt), conv0]
    in_specs = [rows(conv_dim), rows(d_inner), rows(n_heads),
                pl.BlockSpec((None, SUBLANES, conv_dim), lambda b, c: (b, 0, 0))]
    if has_state:
        args.append(s0)
        in_specs.append(state_spec)
    args += [conv_w, conv_b.reshape(1, conv_dim), a_log.reshape(1, n_heads), d_skip.reshape(1, n_heads),
             norm_g.reshape(1, d_inner), e64]
    in_specs += [const((CONV_W, conv_dim)), const((1, conv_dim)), const((1, n_heads)), const((1, n_heads)),
                 const((1, d_inner)), const(e64.shape)]
    kern = functools.partial(_ssd_kernel, t_rows=t_rows, q_rows=q_rows, has_state=has_state)
    y, sfin, cnew = pl.pallas_call(
        kern, grid=(bsz, n_chunks), in_specs=in_specs,
        out_specs=[rows(d_inner), state_spec,
                   pl.BlockSpec((None, CONV_W - 1, conv_dim), lambda b, c: (b, 0, 0))],
        out_shape=[jax.ShapeDtypeStruct((bsz, seq, d_inner), BF16),
                   jax.ShapeDtypeStruct((bsz, n_heads, SSD_HEAD_DIM, D_STATE), F32),
                   jax.ShapeDtypeStruct((bsz, CONV_W - 1, conv_dim), F32)],
        scratch_shapes=[pltpu.VMEM((SUBLANES + q_rows, conv_dim), F32),
                        pltpu.VMEM((q_rows, conv_dim), F32),
                        pltpu.VMEM((q_rows, n_heads), F32),
                        pltpu.VMEM((n_heads, q_rows), F32),
                        pltpu.VMEM((n_heads, q_rows, LANES), F32),
                        pltpu.VMEM((SSD_GROUPS, D_STATE, hpg * SSD_HEAD_DIM), F32)],
        compiler_params=_cparams(("parallel", "arbitrary")), name="ssd")(*args)
    return y.reshape(bsz * seq, d_inner), sfin, cnew


def _merge_kernel(ya_ref, ys_ref, wa_ref, *rest):
    ws_refs, (ga_ref, gs_ref, o_ref) = rest[:-3], rest[-3:]
    kb = wa_ref.shape[0]
    ya = jnp.dot(ya_ref[...], wa_ref[...], preferred_element_type=F32)
    yb = jnp.dot(ys_ref[:, :kb], ws_refs[0][...], preferred_element_type=F32)
    for i in range(1, len(ws_refs)):
        yb = yb + jnp.dot(ys_ref[:, i * kb:(i + 1) * kb], ws_refs[i][...], preferred_element_type=F32)
    o_ref[...] = (ga_ref[...].astype(F32) * ya + gs_ref[...].astype(F32) * yb).astype(o_ref.dtype)


def _merge(y_att, y_ssd, wb, gates, tm):
    m, att_w = y_att.shape
    d_inner = y_ssd.shape[1]
    d = wb.shape[1]
    tn = 512
    nj = d // tn
    n_ssd = d_inner // att_w
    w_blk = lambda r: pl.BlockSpec((att_w, tn), lambda i, j: (r, j))
    return pl.pallas_call(
        _merge_kernel, grid=(m // tm, nj),
        in_specs=[pl.BlockSpec((tm, att_w), lambda i, j: (i, 0)), pl.BlockSpec((tm, d_inner), lambda i, j: (i, 0))]
                 + [w_blk(r) for r in range(1 + n_ssd)]
                 + [pl.BlockSpec((tm, tn), lambda i, j: (i, j)), pl.BlockSpec((tm, tn), lambda i, j: (i, j + nj))],
        out_specs=pl.BlockSpec((tm, tn), lambda i, j: (i, j)),
        out_shape=jax.ShapeDtypeStruct((m, d), BF16),
        compiler_params=_cparams(("parallel", "parallel")), name="merge",
    )(y_att, y_ssd, *([wb] * (1 + n_ssd)), gates, gates)


def _out_kernel(m_ref, w_ref, x_ref, g_ref, o_ref):
    h = x_ref[...] + jnp.dot(m_ref[...], w_ref[...], preferred_element_type=F32)
    ms = jnp.mean(h * h, axis=-1, keepdims=True)
    o_ref[...] = h * lax.rsqrt(ms + EPS) * g_ref[...]


def _out_proj(merged, w_out, x, final_g, tm):
    m, d = x.shape
    return pl.pallas_call(
        _out_kernel, grid=(m // tm,),
        in_specs=[pl.BlockSpec((tm, d), lambda i: (i, 0)), pl.BlockSpec((d, d), lambda i: (0, 0)),
                  pl.BlockSpec((tm, d), lambda i: (i, 0)), pl.BlockSpec((1, d), lambda i: (0, 0))],
        out_specs=pl.BlockSpec((tm, d), lambda i: (i, 0)),
        out_shape=jax.ShapeDtypeStruct((m, d), F32),
        compiler_params=_cparams(("parallel",)), name="out_proj")(merged, w_out, x, final_g.reshape(1, d))


def _round_up(x, n):
    return (x + n - 1) // n * n


def kernel(x_prompt, x_sample, cache_k, cache_v, cache_idx_k, state_ssm, state_conv, page_table, norm_g, w_in, conv_w,
           conv_b, dt_bias, a_log, d_skip, ssd_norm_g, idx_ln_w, idx_ln_b, w_branch, w_out, final_norm_g):
    assert w_in.shape[0] == 1, "single-layer trunk"
    bp, seq, d = x_prompt.shape
    bs, n_tok, _ = x_sample.shape
    n_phys, page = cache_k.shape[1], cache_k.shape[2]
    past = page_table.shape[1] * page
    att_w, kv_w = ATT_HEADS * HEAD_DIM, KV_HEADS * HEAD_DIM
    d_inner = 2 * d
    conv_dim = d_inner + 2 * SSD_GROUPS * D_STATE
    top_k_p = max(1, min(TOPK_MAX, seq // 4))
    top_k_s = max(1, min(TOPK_MAX, (past + n_tok) // 4))

    wb = w_branch[0].astype(BF16)
    w_out_b = w_out[0].astype(BF16)
    w_in0 = jnp.swapaxes(w_in[0], 0, 1)

    mp = bp * seq
    tm_p = min(1024, seq)
    xp = x_prompt.reshape(mp, d)
    xn_p = _rmsnorm_bf16(xp, norm_g[0], min(512, seq))
    pp = _projections(xn_p, w_in0, jnp.arange(seq, dtype=I32), seq // tm_p, tm_p, idx_ln_w[0], idx_ln_b[0], dt_bias[0])
    y_att_p = _prompt_attention(pp, bp, seq, top_k_p)
    y_ssd_p, ssm_p, conv_p = _ssd(pp["xbc"], pp["z_ssd"], pp["dt"], jnp.zeros((bp, SUBLANES, conv_dim), F32), None,
                                  conv_w[0], conv_b[0], a_log[0], d_skip[0], ssd_norm_g[0], bp, seq)
    merged_p = _merge(y_att_p, y_ssd_p, wb, pp["gates"], min(1024, seq))
    y_p = _out_proj(merged_p, w_out_b, xp, final_norm_g, min(512, seq))

    ms_rows = bs * n_tok
    rows_pad = _round_up(ms_rows, LANES)
    xs = x_sample.reshape(ms_rows, d)
    xs_pad = jnp.pad(xs, ((0, rows_pad - ms_rows), (0, 0))) if rows_pad != ms_rows else xs
    pos_s = past + (jnp.arange(rows_pad, dtype=I32) % n_tok)
    xn_s = _rmsnorm_bf16(xs_pad, norm_g[0], rows_pad)
    ps = _projections(xn_s, w_in0, pos_s, 1, rows_pad, idx_ln_w[0], idx_ln_b[0], dt_bias[0])
    real = lambda a: a[:ms_rows]

    qi_rows = jnp.transpose(ps["qi"][:, :ms_rows], (1, 0, 2)).reshape(bs, n_tok * IDX_HEADS, IDX_DIM)
    wi_col = real(ps["wi"]).reshape(bs, n_tok * IDX_HEADS, 1)
    scores = _sample_scores(qi_rows, wi_col, jnp.swapaxes(cache_idx_k[0], 1, 2), page_table, n_tok)
    scores = scores.reshape(ms_rows, past)
    if rows_pad != ms_rows:
        scores = jnp.pad(scores, ((0, rows_pad - ms_rows), (0, 0)))
    keep, keep_new = _sample_select(scores, ps["qi"], ps["ki_bf"], ps["wi"], n_tok, ms_rows, top_k_s)
    keep = real(keep).reshape(bs, n_tok, past * KV_HEADS)
    keep_new = real(keep_new).reshape(bs, n_tok, rows_pad * KV_HEADS)
    q_rows = jnp.transpose(ps["q"][:, :ms_rows], (1, 0, 2)).reshape(bs, n_tok * ATT_HEADS, HEAD_DIM)
    z_rows = real(ps["z_att"]).reshape(bs, n_tok * ATT_HEADS, HEAD_DIM)
    k_new, v_new = ps["k"], ps["v"]
    ck = cache_k[0].reshape(n_phys, page * KV_HEADS, HEAD_DIM)
    cv = cache_v[0].reshape(n_phys, page * KV_HEADS, HEAD_DIM)
    y_att_s = _sample_attention(q_rows, z_rows, keep, keep_new, k_new, v_new, ck, cv, page_table, n_tok)
    y_att_s = y_att_s.reshape(ms_rows, att_w)

    conv0_s = jnp.pad(state_conv[0], ((0, 0), (SUBLANES - (CONV_W - 1), 0), (0, 0)))
    y_ssd_s, ssm_s, conv_s = _ssd(real(ps["xbc"]), real(ps["z_ssd"]), real(ps["dt"]), conv0_s, state_ssm[0],
                                  conv_w[0], conv_b[0], a_log[0], d_skip[0], ssd_norm_g[0], bs, n_tok)
    if rows_pad != ms_rows:
        padr = lambda a: jnp.pad(a, ((0, rows_pad - ms_rows), (0, 0)))
        y_att_s, y_ssd_s = padr(y_att_s), padr(y_ssd_s)
    merged_s = _merge(y_att_s, y_ssd_s, wb, ps["gates"], rows_pad)
    y_s = real(_out_proj(merged_s, w_out_b, xs_pad, final_norm_g, rows_pad))

    kv5 = lambda a, b, t: a[:b * t * KV_HEADS].reshape(1, b, t, KV_HEADS, HEAD_DIM)
    return (y_p.reshape(bp, seq, d), y_s.reshape(bs, n_tok, d),
            kv5(pp["k"], bp, seq), kv5(pp["v"], bp, seq), pp["ki"].reshape(1, bp, seq, IDX_DIM),
            ssm_p[None], conv_p[None],
            kv5(ps["k"], bs, n_tok), kv5(ps["v"], bs, n_tok), real(ps["ki"]).reshape(1, bs, n_tok, IDX_DIM),
            ssm_s[None], conv_s[None])
```

```python
import functools

import numpy as np
import jax
import jax.numpy as jnp
from jax import lax
from jax.experimental import pallas as pl
from jax.experimental.pallas import tpu as pltpu

F32, BF16, I32 = jnp.float32, jnp.bfloat16, jnp.int32

ATT_HEADS = 16
KV_HEADS = 4
HEAD_DIM = 128
IDX_HEADS = 16
IDX_DIM = 64
TOPK_MAX = 256
ROPE_THETA = 10000.0
SSD_HEAD_DIM = 64
SSD_GROUPS = 8
D_STATE = 128
CONV_W = 4
CHUNK = 128
EPS = 1e-6

LANES = 128
SUBLANES = 8
BF16_ROWS = 16
VMEM_LIMIT = 56 * 1024 * 1024
INT_MIN = -2 ** 31
KEY_OF_MOST_NEGATIVE_FLOAT = INT_MIN + 0x00800000
MASKED = -1e30
LOG2E = 1.4426950408889634
PAGES_PER_STEP = 32
SCORE_PAGES_PER_STEP = 32
GROUPS_PER_TRIP = 8
IDX_HEADS_PER_DOT = 8


def _cparams(sem):
    return pltpu.CompilerParams(dimension_semantics=sem, vmem_limit_bytes=VMEM_LIMIT)


def _nt_dot(a, b):
    return lax.dot_general(a, b, (((1,), (1,)), ((), ())), preferred_element_type=F32)


def _tile_lanes(x, n):
    return x if n == 1 else jnp.concatenate([x] * n, axis=1)


def _split3(x):
    hi = x.astype(BF16)
    r1 = x - hi.astype(F32)
    mid = r1.astype(BF16)
    lo = (r1 - mid.astype(F32)).astype(BF16)
    return hi, mid, lo


def _dot_exact_rhs(x, e):
    hi, mid, lo = _split3(x)
    d = functools.partial(jnp.dot, preferred_element_type=F32)
    return d(hi, e) + d(mid, e) + d(lo, e)


def _dot_exact_lhs(e, x):
    hi, mid, lo = _split3(x)
    d = functools.partial(jnp.dot, preferred_element_type=F32)
    return d(e, hi) + d(e, mid) + d(e, lo)


def _norm_kernel(x_ref, g_ref, o_ref):
    x = x_ref[...]
    ms = jnp.mean(x * x, axis=-1, keepdims=True)
    o_ref[...] = (x * lax.rsqrt(ms + EPS) * g_ref[...]).astype(o_ref.dtype)


def _rmsnorm_bf16(x, g, tm):
    m, d = x.shape
    return pl.pallas_call(
        _norm_kernel, grid=(m // tm,),
        in_specs=[pl.BlockSpec((tm, d), lambda i: (i, 0)), pl.BlockSpec((1, d), lambda i: (0, 0))],
        out_specs=pl.BlockSpec((tm, d), lambda i: (i, 0)),
        out_shape=jax.ShapeDtypeStruct((m, d), BF16),
        compiler_params=_cparams(("parallel",)), name="rmsnorm")(x, g.reshape(1, d))


def _mm_kernel(x_ref, w_ref, *rest, epilogue, n_aux):
    acc = _nt_dot(x_ref[...], w_ref[...].astype(BF16))
    epilogue(acc, rest[:n_aux], rest[n_aux:])


def _matmul(x, wt, row0, n, epilogue, aux, aux_specs, out_shapes, out_specs, tm, tn, name):
    m, k = x.shape
    kern = functools.partial(_mm_kernel, epilogue=epilogue, n_aux=len(aux))
    return pl.pallas_call(
        kern, grid=(m // tm, n // tn),
        in_specs=[pl.BlockSpec((tm, k), lambda i, j: (i, 0)),
                  pl.BlockSpec((pl.Element(tn), pl.Element(k)),
                               lambda i, j: (pl.multiple_of(row0 + j * tn, SUBLANES), 0))] + list(aux_specs),
        out_specs=out_specs, out_shape=out_shapes,
        compiler_params=_cparams(("parallel", "parallel")), name=name)(x, wt, *aux)


def _rope_half(x, cos, sin_signed, half):
    if 2 * half == LANES:
        partner = pltpu.roll(x, half, 1)
    else:
        partner = jnp.concatenate([x[:, half:], x[:, :half]], axis=1)
    return x * cos + partner * sin_signed


def _store_heads(outs, c, r, head_major, token_major_first):
    n_heads = outs[0].shape[0] // r.shape[0] if token_major_first else None
    for idx, o in enumerate(outs):
        if token_major_first and idx == 0:
            o[pl.ds(c, r.shape[0], stride=n_heads), :] = r.astype(o.dtype)
        elif head_major:
            o[c] = r.astype(o.dtype)
        else:
            o[:, c * HEAD_DIM:(c + 1) * HEAD_DIM] = r.astype(o.dtype)


def _epi_rope128(acc, aux, outs, post_scale=None, head_major=False, token_major_first=False):
    cos, sin = aux[0][...], aux[1][...]
    for c in range(acc.shape[1] // HEAD_DIM):
        r = _rope_half(acc[:, c * HEAD_DIM:(c + 1) * HEAD_DIM], cos, sin, HEAD_DIM // 2)
        if post_scale is not None:
            r = r * post_scale
        _store_heads(outs, c, r, head_major, token_major_first)


def _epi_heads(acc, aux, outs):
    for c in range(acc.shape[1] // HEAD_DIM):
        _store_heads(outs, c, acc[:, c * HEAD_DIM:(c + 1) * HEAD_DIM], False, True)


def _epi_store(acc, aux, outs, fn=None):
    val = acc if fn is None else fn(acc)
    for o in outs:
        o[...] = val.astype(o.dtype)


def _epi_qi(acc, aux, outs):
    cos, sin = aux[0][...], aux[1][...]
    for hh in range(acc.shape[1] // IDX_DIM):
        x = acc[:, hh * IDX_DIM:(hh + 1) * IDX_DIM]
        outs[0][hh] = _rope_half(x, cos, sin, IDX_DIM // 2).astype(outs[0].dtype)


def _epi_kiwi(acc, aux, outs, idx_scale):
    lnw, lnb, cos, sin = (a[...] for a in aux)
    ki = acc[:, :IDX_DIM]
    mu = jnp.mean(ki, axis=-1, keepdims=True)
    kc = ki - mu
    y = kc * lax.rsqrt(jnp.mean(kc * kc, axis=-1, keepdims=True) + EPS) * lnw + lnb
    r = _rope_half(y, cos, sin, IDX_DIM // 2)
    outs[0][...] = r
    outs[1][...] = r.astype(BF16)
    outs[2][...] = acc[:, IDX_DIM:] * idx_scale


def _epi_softplus(acc, aux, outs):
    x = acc + aux[0][...]
    outs[0][...] = jnp.maximum(x, 0.0) + jnp.log1p(jnp.exp(-jnp.abs(x)))


def _silu(x):
    return x * jax.nn.sigmoid(x)


def _rope_tables(pos, d):
    inv = ROPE_THETA ** (-jnp.arange(0, d, 2, dtype=F32) / d)
    ang = pos.astype(F32)[:, None] * inv[None, :]
    cos, sin = jnp.cos(ang), jnp.sin(ang)
    return jnp.concatenate([cos, cos], axis=-1), jnp.concatenate([-sin, sin], axis=-1)


def _projections(xn, wt, pos_rows, n_pos_blocks, tm, idx_ln_w, idx_ln_b, dt_bias):
    m, d = xn.shape
    att_w, kv_w = ATT_HEADS * HEAD_DIM, KV_HEADS * HEAD_DIM
    d_inner = 2 * d
    conv_dim = d_inner + 2 * SSD_GROUPS * D_STATE
    ssd_heads = d_inner // SSD_HEAD_DIM
    splits = (att_w, kv_w, kv_w, att_w, IDX_HEADS * IDX_DIM, IDX_DIM, IDX_HEADS, d_inner, conv_dim,
              ssd_heads, d, d)
    assert sum(splits) == wt.shape[0]
    off = np.concatenate([[0], np.cumsum(splits)]).astype(int)
    assert all(o % BF16_ROWS == 0 for o in off), "segment rows of the weight must start on packed-row boundaries"
    seg = lambda a, b: (wt, int(off[a]), int(off[b] - off[a]))

    cos128, sin128 = _rope_tables(pos_rows, HEAD_DIM)
    cos64, sin64 = _rope_tables(pos_rows, IDX_DIM)
    pos_map = lambda i, j: (i % n_pos_blocks, 0)
    t128 = [pl.BlockSpec((tm, HEAD_DIM), pos_map)] * 2
    t64 = [pl.BlockSpec((tm, IDX_DIM), pos_map)] * 2
    tile = lambda tn: pl.BlockSpec((tm, tn), lambda i, j: (i, j))
    sds = lambda n, dt: jax.ShapeDtypeStruct((m, n), dt)
    tn = 512
    tw = 1024

    (q,) = _matmul(xn, *seg(0, 1), functools.partial(_epi_rope128, post_scale=HEAD_DIM ** -0.5 * LOG2E, head_major=True),
                   [cos128, sin128], t128, [jax.ShapeDtypeStruct((ATT_HEADS, m, HEAD_DIM), BF16)],
                   [pl.BlockSpec((tw // HEAD_DIM, tm, HEAD_DIM), lambda i, j: (j, i, 0))], tm, tw, "proj_q")
    kv_shapes = [jax.ShapeDtypeStruct((m * KV_HEADS, HEAD_DIM), F32), sds(kv_w, BF16)]
    kv_specs = [pl.BlockSpec((tm * KV_HEADS, HEAD_DIM), lambda i, j: (i, 0)), tile(kv_w)]
    k, k_bf = _matmul(xn, *seg(1, 2), functools.partial(_epi_rope128, token_major_first=True), [cos128, sin128], t128,
                      kv_shapes, kv_specs, tm, kv_w, "proj_k")
    v, v_bf = _matmul(xn, *seg(2, 3), _epi_heads, [], [], kv_shapes, kv_specs, tm, kv_w, "proj_v")
    (z_att,) = _matmul(xn, *seg(3, 4), functools.partial(_epi_store, fn=_silu), [], [],
                       [sds(att_w, BF16)], [tile(tw)], tm, tw, "proj_zatt")
    qi_tn = 4 * IDX_DIM
    (qi,) = _matmul(xn, *seg(4, 5), _epi_qi, [cos64, sin64], t64,
                    [jax.ShapeDtypeStruct((IDX_HEADS, m, IDX_DIM), BF16)],
                    [pl.BlockSpec((4, tm, IDX_DIM), lambda i, j: (j, i, 0))], tm, qi_tn, "proj_qi")
    kiwi_n = IDX_DIM + IDX_HEADS
    idx_scale = IDX_HEADS ** -0.5 * IDX_DIM ** -0.5
    row1 = lambda n: pl.BlockSpec((1, n), lambda i, j: (0, 0))
    ki, ki_bf, wi = _matmul(
        xn, *seg(5, 7), functools.partial(_epi_kiwi, idx_scale=idx_scale),
        [idx_ln_w.reshape(1, IDX_DIM), idx_ln_b.reshape(1, IDX_DIM), cos64, sin64],
        [row1(IDX_DIM), row1(IDX_DIM)] + t64,
        [sds(IDX_DIM, F32), sds(IDX_DIM, BF16), sds(IDX_HEADS, F32)],
        [pl.BlockSpec((tm, IDX_DIM), lambda i, j: (i, 0))] * 2 + [pl.BlockSpec((tm, IDX_HEADS), lambda i, j: (i, 0))],
        tm, kiwi_n, "proj_kiwi")
    (z_ssd,) = _matmul(xn, *seg(7, 8), functools.partial(_epi_store, fn=_silu), [], [],
                       [sds(d_inner, BF16)], [tile(tw)], tm, tw, "proj_zssd")
    (xbc,) = _matmul(xn, *seg(8, 9), _epi_store, [], [], [sds(conv_dim, F32)], [tile(tw)], tm, tw, "proj_xbc")
    (dt,) = _matmul(xn, *seg(9, 10), _epi_softplus, [dt_bias.reshape(1, ssd_heads)], [row1(ssd_heads)],
                    [sds(ssd_heads, F32)], [pl.BlockSpec((tm, ssd_heads), lambda i, j: (i, 0))],
                    tm, ssd_heads, "proj_dt")
    (gates,) = _matmul(xn, *seg(10, 12), functools.partial(_epi_store, fn=jax.nn.sigmoid), [], [],
                       [sds(2 * d, BF16)], [tile(tw)], tm, tw, "proj_gates")
    return dict(q=q, k=k, k_bf=k_bf, v=v, v_bf=v_bf, z_att=z_att, qi=qi, ki=ki, ki_bf=ki_bf, wi=wi,
                z_ssd=z_ssd, xbc=xbc, dt=dt, gates=gates)


def _key_to_float(key):
    return pltpu.bitcast(jnp.where(key < 0, key ^ jnp.int32(0x7FFFFFFF), key), F32)


def _kth_largest(count_ge, shape, top_k):
    kf = jnp.float32(top_k)

    def pick(t, n, cands):
        for cand, c in zip(cands, count_ge([_key_to_float(cand) for cand in cands])):
            c = jnp.broadcast_to(c, shape)
            t, n = jnp.where(c >= kf, cand, t), jnp.where(c >= kf, c, n)
        return t, n

    t, n = pick(jnp.full(shape, INT_MIN, I32), jnp.zeros(shape, F32), [jnp.zeros(shape, I32)])

    def body(i, tn):
        t, n = tn
        return pick(t, n, [t | lax.shift_left(jnp.int32(1), jnp.int32(30) - i)])

    t, n = lax.fori_loop(0, 31, body, (t, n))
    return _key_to_float(jnp.maximum(t, jnp.int32(KEY_OF_MOST_NEGATIVE_FLOAT))), n


def _lane_tile_sum(x):
    part = x[:, :LANES]
    for c in range(1, x.shape[1] // LANES):
        part = part + x[:, c * LANES:(c + 1) * LANES]
    return part


def _attn_kernel(q_ref, qi_ref, wi_ref, z_ref, k_ref, v_ref, ki_ref, o_ref,
                 score_ref, wexp_ref, bias_ref, s_ref, m_ref, l_ref, acc_ref, *, tq, tk, tkc, hps, top_k):
    qt = pl.program_id(1)
    n_kc = ((qt + 1) * tq + tk - 1) // tk
    nl = tk // LANES
    row_pos = qt * tq + lax.broadcasted_iota(I32, (tq, 1), 0)

    wi = wi_ref[...]
    for h in range(IDX_HEADS):
        wexp_ref[h] = jnp.broadcast_to(wi[:, h:h + 1], (tq, LANES))

    def score_chunk(kc, carry):
        off = pl.multiple_of(kc * tk, tk)
        ki = ki_ref[pl.ds(off, tk), :]
        sc = None
        for h0 in range(0, IDX_HEADS, IDX_HEADS_PER_DOT):
            hs = slice(h0, h0 + IDX_HEADS_PER_DOT)
            d = _nt_dot(qi_ref[hs].reshape(IDX_HEADS_PER_DOT * tq, IDX_DIM), ki)
            r = jnp.maximum(d, 0.0) * _tile_lanes(wexp_ref[hs].reshape(IDX_HEADS_PER_DOT * tq, LANES), nl)
            for h in range(IDX_HEADS_PER_DOT):
                sc = r[h * tq:(h + 1) * tq] if sc is None else sc + r[h * tq:(h + 1) * tq]
        kpos = off + lax.broadcasted_iota(I32, (1, tk), 1)
        score_ref[:, pl.ds(off, tk)] = jnp.where(kpos <= row_pos, sc, -jnp.inf)
        return carry

    lax.fori_loop(0, n_kc, score_chunk, 0)

    def count(cands, strict=False):
        cands_t = [_tile_lanes(c, nl) for c in cands]

        def body(kc, cnts):
            sc = score_ref[:, pl.ds(pl.multiple_of(kc * tk, tk), tk)]
            return tuple(cnt + _lane_tile_sum(jnp.where((sc > c) if strict else (sc >= c), 1.0, 0.0))
                         for cnt, c in zip(cnts, cands_t))

        cnts = lax.fori_loop(0, n_kc, body, tuple(jnp.zeros((tq, LANES), F32) for _ in cands))
        return [jnp.sum(cnt, axis=1, keepdims=True) for cnt in cnts]

    thr1, n_ge = _kth_largest(count, (tq, LANES), top_k)
    thr = _tile_lanes(thr1, nl)
    has_ties = jnp.max(n_ge) > jnp.float32(top_k)

    @pl.when(jnp.logical_not(has_ties))
    def _():
        def bias_chunk(kc, carry):
            off = pl.multiple_of(kc * tk, tk)
            bias_ref[:, pl.ds(off, tk)] = jnp.where(score_ref[:, pl.ds(off, tk)] >= thr, 0.0, MASKED)
            return carry

        lax.fori_loop(0, n_kc, bias_chunk, 0)

    @pl.when(has_ties)
    def _():
        room = jnp.float32(top_k) - count([thr1], strict=True)[0]
        ri = lax.broadcasted_iota(I32, (tk, tk), 0)
        ci = lax.broadcasted_iota(I32, (tk, tk), 1)
        prefix = jnp.where(ri <= ci, 1.0, 0.0).astype(BF16)

        def bias_chunk(kc, seen):
            off = pl.multiple_of(kc * tk, tk)
            key = score_ref[:, pl.ds(off, tk)]
            eq = jnp.where(key == thr, 1.0, 0.0)
            rank = seen + jnp.dot(eq.astype(BF16), prefix, preferred_element_type=F32)
            keep = (key > thr) | ((key == thr) & (rank <= room))
            bias_ref[:, pl.ds(off, tk)] = jnp.where(keep, 0.0, MASKED)
            return seen + jnp.sum(eq, axis=1, keepdims=True)

        lax.fori_loop(0, n_kc, bias_chunk, jnp.zeros((tq, 1), F32))

    n_kcc = ((qt + 1) * tq + tkc - 1) // tkc
    nlc = tkc // LANES

    def bias_tail(kc, carry):
        bias_ref[:, pl.ds(pl.multiple_of(kc * tk, tk), tk)] = jnp.full((tq, tk), MASKED, F32)
        return carry

    lax.fori_loop(n_kc, n_kcc * (tkc // tk), bias_tail, 0)
    rep = ATT_HEADS // KV_HEADS

    rows = hps * tq
    n_stacks = ATT_HEADS // hps
    kv_lanes = lambda g: slice((g * hps // rep) * HEAD_DIM, (g * hps // rep + 1) * HEAD_DIM)

    def logits_chunk(g, ks):
        slot = g % 2
        gs = kv_lanes(g)
        qg = q_ref[g * hps:(g + 1) * hps].reshape(rows, HEAD_DIM)
        bias = bias_ref[:, ks]
        s = jnp.concatenate([bias] * hps, axis=0) + _nt_dot(qg, k_ref[ks, gs])
        s_ref[slot, :, ks] = s
        part = s[:, :LANES]
        for c in range(1, nlc):
            part = jnp.maximum(part, s[:, c * LANES:(c + 1) * LANES])
        m_ref[slot] = jnp.maximum(m_ref[slot], part)

    def pv_chunk(g, ks):
        slot = g % 2
        gs = kv_lanes(g)
        p = jnp.exp2(s_ref[slot, :, ks] - _tile_lanes(m_ref[slot], nlc))
        l_ref[...] += _lane_tile_sum(p)
        acc_ref[...] += jnp.dot(p.astype(BF16), v_ref[ks, gs], preferred_element_type=F32)

    for stage in range(n_stacks + 1):
        g_logits = stage if stage < n_stacks else None
        g_pv = stage - 1 if stage > 0 else None
        if g_logits is not None:
            m_ref[g_logits % 2] = jnp.full((rows, LANES), MASKED, F32)
        if g_pv is not None:
            slot = g_pv % 2
            m_ref[slot] = jnp.broadcast_to(jnp.max(m_ref[slot], axis=1, keepdims=True), (rows, LANES))
            l_ref[...] = jnp.zeros((rows, LANES), F32)
            acc_ref[...] = jnp.zeros((rows, HEAD_DIM), F32)

        def stage_chunk(kc, carry, g_logits=g_logits, g_pv=g_pv):
            ks = pl.ds(pl.multiple_of(kc * tkc, tkc), tkc)
            if g_logits is not None:
                logits_chunk(g_logits, ks)
            if g_pv is not None:
                pv_chunk(g_pv, ks)
            return carry

        lax.fori_loop(0, n_kcc, stage_chunk, 0)
        if g_pv is not None:
            o = acc_ref[...] / jnp.sum(l_ref[...], axis=1, keepdims=True)
            for r in range(hps):
                hs = slice((g_pv * hps + r) * HEAD_DIM, (g_pv * hps + r + 1) * HEAD_DIM)
                o_ref[:, hs] = (o[r * tq:(r + 1) * tq] * z_ref[:, hs].astype(F32)).astype(o_ref.dtype)


def _prompt_attention(p, batch, seq, top_k):
    m = batch * seq
    tq = min(256, seq)
    tk = min(512, seq)
    nq = seq // tq
    att_w, kv_w = ATT_HEADS * HEAD_DIM, KV_HEADS * HEAD_DIM
    hps = 2
    rows = lambda w: pl.BlockSpec((tq, w), lambda b, t: (b * nq + t, 0))
    whole = lambda w: pl.BlockSpec((seq, w), lambda b, t: (b, 0), pipeline_mode=pl.Buffered(1))
    tkc = min(1024, seq)
    kern = functools.partial(_attn_kernel, tq=tq, tk=tk, tkc=tkc, hps=hps, top_k=top_k)
    return pl.pallas_call(
        kern, grid=(batch, nq),
        in_specs=[pl.BlockSpec((ATT_HEADS, tq, HEAD_DIM), lambda b, t: (0, b * nq + t, 0)),
                  pl.BlockSpec((IDX_HEADS, tq, IDX_DIM), lambda b, t: (0, b * nq + t, 0)),
                  rows(IDX_HEADS), rows(att_w), whole(kv_w), whole(kv_w), whole(IDX_DIM)],
        out_specs=rows(att_w),
        out_shape=jax.ShapeDtypeStruct((m, att_w), BF16),
        scratch_shapes=[pltpu.VMEM((tq, seq), F32),
                        pltpu.VMEM((IDX_HEADS, tq, LANES), F32),
                        pltpu.VMEM((tq, seq), F32),
                        pltpu.VMEM((2, hps * tq, seq), F32),
                        pltpu.VMEM((2, hps * tq, LANES), F32),
                        pltpu.VMEM((hps * tq, LANES), F32),
                        pltpu.VMEM((hps * tq, HEAD_DIM), F32)],
        compiler_params=_cparams(("parallel", "arbitrary")), name="prompt_attention",
    )(p["q"], p["qi"], p["wi"], p["z_att"], p["k_bf"], p["v_bf"], p["ki_bf"])


def _sample_score_kernel(pt_ref, qi_ref, wi_ref, *rest, n_pages, n_tok):
    page_refs, o_ref = rest[:n_pages], rest[n_pages]
    qi = qi_ref[...]
    w = jnp.broadcast_to(wi_ref[...], (qi.shape[0], LANES))
    for k in range(n_pages):
        kit = page_refs[k][...].astype(BF16)
        page = kit.shape[1]
        r = jnp.maximum(jnp.dot(qi, kit, preferred_element_type=F32), 0.0) * _tile_lanes(w, page // LANES)
        sc = jnp.sum(r.reshape(n_tok, IDX_HEADS, page), axis=1)
        o_ref[:, k * page:(k + 1) * page] = sc


def _sample_scores(qi_rows, wi_col, cache_idx_kt, page_table, n_tok):
    bsz, n_pages_total = page_table.shape
    page = cache_idx_kt.shape[2]
    g = SCORE_PAGES_PER_STEP if n_pages_total % SCORE_PAGES_PER_STEP == 0 else 1
    rows = n_tok * IDX_HEADS
    page_spec = lambda k: pl.BlockSpec((None, IDX_DIM, page), lambda b, s, pt: (pt[b, s * g + k], 0, 0))
    grid_spec = pltpu.PrefetchScalarGridSpec(
        num_scalar_prefetch=1, grid=(bsz, n_pages_total // g),
        in_specs=[pl.BlockSpec((None, rows, IDX_DIM), lambda b, s, pt: (b, 0, 0)),
                  pl.BlockSpec((None, rows, 1), lambda b, s, pt: (b, 0, 0))] + [page_spec(k) for k in range(g)],
        out_specs=pl.BlockSpec((None, n_tok, g * page), lambda b, s, pt: (b, 0, s)))
    kern = functools.partial(_sample_score_kernel, n_pages=g, n_tok=n_tok)
    return pl.pallas_call(
        kern, grid_spec=grid_spec,
        out_shape=jax.ShapeDtypeStruct((bsz, n_tok, n_pages_total * page), F32),
        compiler_params=_cparams(("parallel", "arbitrary")), name="sample_scores",
    )(page_table, qi_rows, wi_col, *([cache_idx_kt] * g))


def _sample_select_kernel(sc_ref, qi_ref, ki_ref, wi_ref, rexp_ref, keep_ref, keep_new_ref, *, n_tok, n_rows, top_k):
    rows, past = sc_ref.shape
    nk = ki_ref.shape[0]
    wi = wi_ref[...]
    ki = ki_ref[...]
    sc_new = jnp.zeros((rows, nk), F32)
    for h in range(IDX_HEADS):
        w = jnp.broadcast_to(wi[:, h:h + 1], (rows, LANES))
        sc_new = sc_new + jnp.maximum(_nt_dot(qi_ref[h], ki), 0.0) * _tile_lanes(w, nk // LANES)
    r = lax.broadcasted_iota(I32, (rows, nk), 0)
    c = lax.broadcasted_iota(I32, (rows, nk), 1)
    ok = (r // n_tok == c // n_tok) & (c % n_tok <= r % n_tok) & (r < n_rows) & (c < n_rows)
    sc_new = jnp.where(ok, sc_new, -jnp.inf)

    def count(cands, strict=False):
        def one(cand):
            hit = lambda sc, n: jnp.where((sc > _tile_lanes(cand, n)) if strict else (sc >= _tile_lanes(cand, n)), 1.0, 0.0)
            return jnp.sum(_lane_tile_sum(hit(sc_ref[...], past // LANES)) + _lane_tile_sum(hit(sc_new, nk // LANES)),
                           axis=1, keepdims=True)

        return [one(cand) for cand in cands]

    thr, _ = _kth_largest(count, (rows, LANES), top_k)
    rexp = rexp_ref[...]
    xw = rexp.shape[1]
    room = jnp.float32(top_k) - count([thr], strict=True)[0]
    ri = lax.broadcasted_iota(I32, (LANES, LANES), 0)
    ci = lax.broadcasted_iota(I32, (LANES, LANES), 1)
    prefix = jnp.where(ri <= ci, 1.0, 0.0).astype(BF16)

    def select(key_tile, seen):
        eq = jnp.where(key_tile == thr, 1.0, 0.0)
        rank = seen + jnp.dot(eq.astype(BF16), prefix, preferred_element_type=F32)
        keep = (key_tile > thr) | ((key_tile == thr) & (rank <= room))
        sel = jnp.where(keep, 1.0, 0.0).astype(BF16)
        return (jnp.dot(sel, rexp, preferred_element_type=F32).astype(BF16),
                seen + jnp.sum(eq, axis=1, keepdims=True))

    def past_tile(ct, seen):
        src = pl.ds(pl.multiple_of(ct * LANES, LANES), LANES)
        kept, seen = select(sc_ref[:, src], seen)
        keep_ref[:, pl.ds(pl.multiple_of(ct * xw, xw), xw)] = kept
        return seen

    seen = lax.fori_loop(0, past // LANES, past_tile, jnp.zeros((rows, 1), F32))
    for ct in range(nk // LANES):
        kept, seen = select(sc_new[:, ct * LANES:(ct + 1) * LANES], seen)
        keep_new_ref[:, ct * xw:(ct + 1) * xw] = kept


def _sample_select(scores, qi, ki_new_bf, wi, n_tok, n_rows, top_k):
    rows, past = scores.shape
    nk = ki_new_bf.shape[0]
    rexp = (jnp.arange(LANES * KV_HEADS)[None, :] // KV_HEADS == jnp.arange(LANES)[:, None]).astype(BF16)
    kern = functools.partial(_sample_select_kernel, n_tok=n_tok, n_rows=n_rows, top_k=top_k)
    full = lambda shape: pl.BlockSpec(shape, lambda i: (0,) * len(shape))
    return pl.pallas_call(
        kern, grid=(1,),
        in_specs=[full((rows, past)), full((IDX_HEADS, rows, IDX_DIM)), full((nk, IDX_DIM)), full((rows, IDX_HEADS)),
                  full(rexp.shape)],
        out_specs=[full((rows, past * KV_HEADS)), full((rows, nk * KV_HEADS))],
        out_shape=[jax.ShapeDtypeStruct((rows, past * KV_HEADS), BF16),
                   jax.ShapeDtypeStruct((rows, nk * KV_HEADS), BF16)],
        compiler_params=_cparams(("arbitrary",)), name="sample_select")(scores, qi, ki_new_bf, wi, rexp)


def _sample_attn_kernel(pt_ref, q_ref, z_ref, keep_ref, keep_new_ref, knew_ref, vnew_ref, *rest, n_pages, n_tok, page):
    k_refs, v_refs = rest[:n_pages], rest[n_pages:2 * n_pages]
    o_ref, m_ref, l_ref, acc_ref, s_ref = rest[2 * n_pages:]
    step = pl.program_id(1)
    rows = n_tok * ATT_HEADS
    cols = page * KV_HEADS
    rep = ATT_HEADS // KV_HEADS
    nl = cols // LANES

    @pl.when(step == 0)
    def _():
        m_ref[...] = jnp.full(m_ref.shape, MASKED, F32)
        l_ref[...] = jnp.zeros(l_ref.shape, F32)
        acc_ref[...] = jnp.zeros(acc_ref.shape, F32)

    q = q_ref[...]
    rr = lax.broadcasted_iota(I32, (rows, cols), 0)
    cc = lax.broadcasted_iota(I32, (rows, cols), 1)
    head_bias = jnp.where((cc % KV_HEADS) == ((rr % ATT_HEADS) // rep), 0.0, MASKED)

    def update(blocks):
        mx = None
        for i, (load_k, _, load_keep) in enumerate(blocks):
            kp = load_k().astype(BF16)
            kb = (load_keep().astype(F32) - 1.0) * (-MASKED)
            kb_rows = jnp.concatenate(
                [jnp.broadcast_to(kb[t:t + 1, :], (ATT_HEADS, cols)) for t in range(n_tok)], axis=0)
            s = _nt_dot(q, kp) + (kb_rows + head_bias)
            s_ref[:, i * cols:(i + 1) * cols] = s
            part = s[:, :LANES]
            for c in range(1, nl):
                part = jnp.maximum(part, s[:, c * LANES:(c + 1) * LANES])
            mx = part if mx is None else jnp.maximum(mx, part)
        m_old = m_ref[...]
        m_new = jnp.maximum(m_old, jnp.max(mx, axis=1, keepdims=True))
        alpha = jnp.exp2(m_old - m_new)
        m_t = _tile_lanes(m_new, nl)
        lsum = jnp.zeros((rows, LANES), F32)
        pv = jnp.zeros((rows, HEAD_DIM), F32)
        for i, (_, load_v, _) in enumerate(blocks):
            p = jnp.exp2(s_ref[:, i * cols:(i + 1) * cols] - m_t)
            lsum = lsum + _lane_tile_sum(p)
            pv = pv + jnp.dot(p.astype(BF16), load_v().astype(BF16), preferred_element_type=F32)
        l_ref[...] = alpha * l_ref[...] + lsum
        acc_ref[...] = alpha * acc_ref[...] + pv
        m_ref[...] = m_new

    update([(lambda k=k: k_refs[k][...], lambda k=k: v_refs[k][...], lambda k=k: keep_ref[:, k * cols:(k + 1) * cols])
            for k in range(n_pages)])

    @pl.when(step == pl.num_programs(1) - 1)
    def _():
        update([(lambda c=c: knew_ref[c * cols:(c + 1) * cols, :], lambda c=c: vnew_ref[c * cols:(c + 1) * cols, :],
                 lambda c=c: keep_new_ref[:, c * cols:(c + 1) * cols]) for c in range(knew_ref.shape[0] // cols)])
        lsum = jnp.sum(l_ref[...], axis=1, keepdims=True)
        o_ref[...] = (acc_ref[...] / lsum * z_ref[...].astype(F32)).astype(o_ref.dtype)


def _sample_attention(q_rows, z_rows, keep, keep_new, k_new, v_new, cache_k, cache_v, page_table, n_tok):
    bsz, n_pages_total = page_table.shape
    cols = cache_k.shape[1]
    page = cols // KV_HEADS
    g = PAGES_PER_STEP if n_pages_total % PAGES_PER_STEP == 0 else 1
    rows = n_tok * ATT_HEADS
    per_b = lambda shape: pl.BlockSpec((None,) + shape, lambda b, s, pt: (b,) + (0,) * len(shape))
    const = lambda shape: pl.BlockSpec(shape, lambda b, s, pt: (0,) * len(shape))
    page_spec = lambda k: pl.BlockSpec((None, cols, HEAD_DIM), lambda b, s, pt: (pt[b, s * g + k], 0, 0))
    grid_spec = pltpu.PrefetchScalarGridSpec(
        num_scalar_prefetch=1, grid=(bsz, n_pages_total // g),
        in_specs=[per_b((rows, HEAD_DIM)), per_b((rows, HEAD_DIM)),
                  pl.BlockSpec((None, n_tok, g * cols), lambda b, s, pt: (b, 0, s)),
                  per_b((n_tok, keep_new.shape[-1])), const(k_new.shape), const(v_new.shape)]
                 + [page_spec(k) for k in range(g)] * 2,
        out_specs=per_b((rows, HEAD_DIM)),
        scratch_shapes=[pltpu.VMEM((rows, LANES), F32), pltpu.VMEM((rows, LANES), F32),
                        pltpu.VMEM((rows, HEAD_DIM), F32), pltpu.VMEM((rows, g * cols), F32)])
    kern = functools.partial(_sample_attn_kernel, n_pages=g, n_tok=n_tok, page=page)
    return pl.pallas_call(
        kern, grid_spec=grid_spec,
        out_shape=jax.ShapeDtypeStruct((bsz, rows, HEAD_DIM), BF16),
        compiler_params=_cparams(("parallel", "arbitrary")), name="sample_attention",
    )(page_table, q_rows, z_rows, keep, keep_new, k_new, v_new, *([cache_k] * g), *([cache_v] * g))


def _ssd_kernel(*refs, t_rows, q_rows, has_state):
    if has_state:
        (xbc_ref, z_ref, dt_ref, conv0_ref, s0_ref, cw_ref, cb_ref, alog_ref, dskip_ref, ng_ref, e64_ref,
         y_ref, sfin_ref, cnew_ref, xp_ref, act_ref, dtp_ref, acst_ref, acsc_ref, st_ref) = refs
    else:
        (xbc_ref, z_ref, dt_ref, conv0_ref, cw_ref, cb_ref, alog_ref, dskip_ref, ng_ref, e64_ref,
         y_ref, sfin_ref, cnew_ref, xp_ref, act_ref, dtp_ref, acst_ref, acsc_ref, st_ref) = refs
        s0_ref = None
    c = pl.program_id(1)
    q = q_rows
    halo = SUBLANES
    n_heads = dt_ref.shape[-1]
    d_inner = n_heads * SSD_HEAD_DIM
    gw = d_inner // SSD_GROUPS
    hpg = n_heads // SSD_GROUPS
    conv_dim = xbc_ref.shape[-1]

    @pl.when(c == 0)
    def _():
        xp_ref[0:halo, :] = conv0_ref[...]
        if t_rows < q:
            xp_ref[halo:, :] = jnp.zeros((q, conv_dim), F32)
            dtp_ref[...] = jnp.zeros(dtp_ref.shape, F32)
        for g in range(SSD_GROUPS):
            if has_state:
                st_ref[g] = s0_ref[g * hpg:(g + 1) * hpg].reshape(gw, D_STATE).T
            else:
                st_ref[g] = jnp.zeros((D_STATE, gw), F32)

    xp_ref[halo:halo + t_rows, :] = xbc_ref[...]
    dtp_ref[0:t_rows, :] = dt_ref[...]

    cblk = LANES
    for cbi in range(conv_dim // cblk):
        cs = slice(cbi * cblk, (cbi + 1) * cblk)
        acc = jnp.broadcast_to(cb_ref[:, cs], (q, cblk))
        for tap in range(CONV_W):
            lo = halo - (CONV_W - 1) + tap
            acc = acc + xp_ref[lo:lo + q, cs] * cw_ref[tap:tap + 1, cs]
        act_ref[:, cs] = _silu(acc)

    dt = dtp_ref[...]
    a = -jnp.exp(alog_ref[...])
    ri = lax.broadcasted_iota(I32, (q, q), 0)
    ci = lax.broadcasted_iota(I32, (q, q), 1)
    tril = ri >= ci
    a_cs = _dot_exact_lhs(jnp.where(tril, 1.0, 0.0).astype(BF16), dt * a)
    a_last = a_cs[q - 1:q, :]
    acst_ref[...] = a_cs.T
    for h in range(n_heads):
        acsc_ref[h] = jnp.broadcast_to(a_cs[:, h:h + 1], (q, LANES))
    dt_b = dt.astype(BF16)
    dte_b = jnp.exp(a_last - a_cs).astype(BF16)
    ea_hi, ea_mid, _ = _split3(jnp.exp(a_cs))
    pad = jnp.zeros((SUBLANES - 2, n_heads), F32)
    row_pieces = _split3(jnp.concatenate([jnp.exp(a_last), dskip_ref[...], pad], axis=0))
    mxu = functools.partial(jnp.dot, preferred_element_type=F32)
    lane = lax.broadcasted_iota(I32, (1, LANES), 1)
    lo_half = lane < SSD_HEAD_DIM

    def group_body(g, carry):
        xs = act_ref[:, pl.ds(pl.multiple_of(g * gw, gw), gw)]
        bm = act_ref[:, pl.ds(pl.multiple_of(d_inner + g * D_STATE, D_STATE), D_STATE)]
        cm = act_ref[:, pl.ds(pl.multiple_of(d_inner + SSD_GROUPS * D_STATE + g * D_STATE, D_STATE), D_STATE)]
        e64g = e64_ref[:, pl.ds(pl.multiple_of(g * gw, gw), gw)]
        dt_x, dte_x = mxu(dt_b, e64g), mxu(dte_b, e64g)
        ea_x = mxu(ea_hi, e64g) + mxu(ea_mid, e64g)
        rows_x = mxu(row_pieces[0], e64g) + mxu(row_pieces[1], e64g) + mxu(row_pieces[2], e64g)
        cdec_x, dskip_x = rows_x[0:1], rows_x[1:2]
        xdt = xs * dt_x
        xdt_b = xdt.astype(BF16)
        bm_b, cm_b = bm.astype(BF16), cm.astype(BF16)
        cb = _nt_dot(cm_b, bm_b)
        st = st_ref[g]
        y_off = jnp.dot(cm_b, st.astype(BF16), preferred_element_type=F32) * ea_x
        y_parts = []
        for pr in range(hpg // 2):
            xpair = xdt_b[:, pr * LANES:(pr + 1) * LANES]
            halves = (jnp.where(lo_half, xpair, jnp.zeros_like(xpair)), jnp.where(lo_half, jnp.zeros_like(xpair), xpair))
            yp = jnp.zeros((q, LANES), F32)
            for s in range(2):
                hl = 2 * pr + s
                seg = acsc_ref[g * hpg + hl][:, :q] - acst_ref[pl.ds(g * hpg + hl, 1), :]
                lmat = jnp.where(tril, jnp.exp(jnp.where(tril, seg, 0.0)), 0.0)
                yp = yp + jnp.dot((cb * lmat).astype(BF16), halves[s], preferred_element_type=F32)
            y_parts.append(yp)
        y = jnp.concatenate(y_parts, axis=1) + y_off + dskip_x * xs
        st_ref[g] = cdec_x * st + jnp.dot(bm.T.astype(BF16), (xdt * dte_x).astype(BF16), preferred_element_type=F32)
        gsl = pl.ds(pl.multiple_of(g * gw, gw), gw)
        yz = y[0:t_rows] * z_ref[:, gsl].astype(F32)
        ms = jnp.mean(yz * yz, axis=-1, keepdims=True)
        y_ref[:, gsl] = (yz * lax.rsqrt(ms + EPS) * ng_ref[:, gsl]).astype(y_ref.dtype)
        return carry

    def group_batch(i, carry):
        for u in range(GROUPS_PER_TRIP):
            group_body(GROUPS_PER_TRIP * i + u, carry)
        return carry

    lax.fori_loop(0, SSD_GROUPS // GROUPS_PER_TRIP, group_batch, 0)

    @pl.when(c == pl.num_programs(1) - 1)
    def _():
        cnew_ref[...] = xp_ref[halo + t_rows - (CONV_W - 1):halo + t_rows, :]
        for g in range(SSD_GROUPS):
            sfin_ref[g * hpg:(g + 1) * hpg] = st_ref[g].T.reshape(hpg, SSD_HEAD_DIM, D_STATE)

    if t_rows == q:
        @pl.when(c < pl.num_programs(1) - 1)
        def _():
            xp_ref[0:halo, :] = xp_ref[q:q + halo, :]


def _ssd(xbc, z, dt, conv0, s0, conv_w, conv_b, a_log, d_skip, norm_g, bsz, seq):
    conv_dim = xbc.shape[-1]
    n_heads = dt.shape[-1]
    d_inner = n_heads * SSD_HEAD_DIM
    t_rows = min(CHUNK, seq)
    n_chunks = seq // t_rows
    q_rows = CHUNK if t_rows == CHUNK else _round_up(t_rows, BF16_ROWS)
    has_state = s0 is not None
    hpg = n_heads // SSD_GROUPS
    e64 = (jnp.arange(d_inner)[None, :] // SSD_HEAD_DIM == jnp.arange(n_heads)[:, None]).astype(BF16)
    x3 = lambda a: a.reshape(bsz, seq, a.shape[-1])
    rows = lambda w: pl.BlockSpec((None, t_rows, w), lambda b, c: (b, c, 0))
    const = lambda shape: pl.BlockSpec(shape, lambda b, c: (0,) * len(shape))
    state_spec = pl.BlockSpec((None, n_heads, SSD_HEAD_DIM, D_STATE), lambda b, c: (b, 0, 0, 0))
    args = [x3(xbc), x3(z), x3(dt), conv0]
    in_specs = [rows(conv_dim), rows(d_inner), rows(n_heads),
                pl.BlockSpec((None, SUBLANES, conv_dim), lambda b, c: (b, 0, 0))]
    if has_state:
        args.append(s0)
        in_specs.append(state_spec)
    args += [conv_w, conv_b.reshape(1, conv_dim), a_log.reshape(1, n_heads), d_skip.reshape(1, n_heads),
             norm_g.reshape(1, d_inner), e64]
    in_specs += [const((CONV_W, conv_dim)), const((1, conv_dim)), const((1, n_heads)), const((1, n_heads)),
                 const((1, d_inner)), const(e64.shape)]
    kern = functools.partial(_ssd_kernel, t_rows=t_rows, q_rows=q_rows, has_state=has_state)
    y, sfin, cnew = pl.pallas_call(
        kern, grid=(bsz, n_chunks), in_specs=in_specs,
        out_specs=[rows(d_inner), state_spec,
                   pl.BlockSpec((None, CONV_W - 1, conv_dim), lambda b, c: (b, 0, 0))],
        out_shape=[jax.ShapeDtypeStruct((bsz, seq, d_inner), BF16),
                   jax.ShapeDtypeStruct((bsz, n_heads, SSD_HEAD_DIM, D_STATE), F32),
                   jax.ShapeDtypeStruct((bsz, CONV_W - 1, conv_dim), F32)],
        scratch_shapes=[pltpu.VMEM((SUBLANES + q_rows, conv_dim), F32),
                        pltpu.VMEM((q_rows, conv_dim), F32),
                        pltpu.VMEM((q_rows, n_heads), F32),
                        pltpu.VMEM((n_heads, q_rows), F32),
                        pltpu.VMEM((n_heads, q_rows, LANES), F32),
                        pltpu.VMEM((SSD_GROUPS, D_STATE, hpg * SSD_HEAD_DIM), F32)],
        compiler_params=_cparams(("parallel", "arbitrary")), name="ssd")(*args)
    return y.reshape(bsz * seq, d_inner), sfin, cnew


def _merge_kernel(ya_ref, ys_ref, wa_ref, *rest):
    ws_refs, (ga_ref, gs_ref, o_ref) = rest[:-3], rest[-3:]
    kb = wa_ref.shape[0]
    ya = jnp.dot(ya_ref[...], wa_ref[...], preferred_element_type=F32)
    yb = jnp.dot(ys_ref[:, :kb], ws_refs[0][...], preferred_element_type=F32)
    for i in range(1, len(ws_refs)):
        yb = yb + jnp.dot(ys_ref[:, i * kb:(i + 1) * kb], ws_refs[i][...], preferred_element_type=F32)
    o_ref[...] = (ga_ref[...].astype(F32) * ya + gs_ref[...].astype(F32) * yb).astype(o_ref.dtype)


def _merge(y_att, y_ssd, wb, gates, tm):
    m, att_w = y_att.shape
    d_inner = y_ssd.shape[1]
    d = wb.shape[1]
    tn = 512
    nj = d // tn
    n_ssd = d_inner // att_w
    w_blk = lambda r: pl.BlockSpec((att_w, tn), lambda i, j: (r, j))
    return pl.pallas_call(
        _merge_kernel, grid=(m // tm, nj),
        in_specs=[pl.BlockSpec((tm, att_w), lambda i, j: (i, 0)), pl.BlockSpec((tm, d_inner), lambda i, j: (i, 0))]
                 + [w_blk(r) for r in range(1 + n_ssd)]
                 + [pl.BlockSpec((tm, tn), lambda i, j: (i, j)), pl.BlockSpec((tm, tn), lambda i, j: (i, j + nj))],
        out_specs=pl.BlockSpec((tm, tn), lambda i, j: (i, j)),
        out_shape=jax.ShapeDtypeStruct((m, d), BF16),
        compiler_params=_cparams(("parallel", "parallel")), name="merge",
    )(y_att, y_ssd, *([wb] * (1 + n_ssd)), gates, gates)


def _out_kernel(m_ref, w_ref, x_ref, g_ref, o_ref):
    h = x_ref[...] + jnp.dot(m_ref[...], w_ref[...], preferred_element_type=F32)
    ms = jnp.mean(h * h, axis=-1, keepdims=True)
    o_ref[...] = h * lax.rsqrt(ms + EPS) * g_ref[...]


def _out_proj(merged, w_out, x, final_g, tm):
    m, d = x.shape
    return pl.pallas_call(
        _out_kernel, grid=(m // tm,),
        in_specs=[pl.BlockSpec((tm, d), lambda i: (i, 0)), pl.BlockSpec((d, d), lambda i: (0, 0)),
                  pl.BlockSpec((tm, d), lambda i: (i, 0)), pl.BlockSpec((1, d), lambda i: (0, 0))],
        out_specs=pl.BlockSpec((tm, d), lambda i: (i, 0)),
        out_shape=jax.ShapeDtypeStruct((m, d), F32),
        compiler_params=_cparams(("parallel",)), name="out_proj")(merged, w_out, x, final_g.reshape(1, d))


def _round_up(x, n):
    return (x + n - 1) // n * n


def kernel(x_prompt, x_sample, cache_k, cache_v, cache_idx_k, state_ssm, state_conv, page_table, norm_g, w_in, conv_w,
           conv_b, dt_bias, a_log, d_skip, ssd_norm_g, idx_ln_w, idx_ln_b, w_branch, w_out, final_norm_g):
    assert w_in.shape[0] == 1, "single-layer trunk"
    bp, seq, d = x_prompt.shape
    bs, n_tok, _ = x_sample.shape
    n_phys, page = cache_k.shape[1], cache_k.shape[2]
    past = page_table.shape[1] * page
    att_w, kv_w = ATT_HEADS * HEAD_DIM, KV_HEADS * HEAD_DIM
    d_inner = 2 * d
    conv_dim = d_inner + 2 * SSD_GROUPS * D_STATE
    top_k_p = max(1, min(TOPK_MAX, seq // 4))
    top_k_s = max(1, min(TOPK_MAX, (past + n_tok) // 4))

    wb = w_branch[0].astype(BF16)
    w_out_b = w_out[0].astype(BF16)
    w_in0 = jnp.swapaxes(w_in[0], 0, 1)

    mp = bp * seq
    tm_p = min(1024, seq)
    xp = x_prompt.reshape(mp, d)
    xn_p = _rmsnorm_bf16(xp, norm_g[0], min(512, seq))
    pp = _projections(xn_p, w_in0, jnp.arange(seq, dtype=I32), seq // tm_p, tm_p, idx_ln_w[0], idx_ln_b[0], dt_bias[0])
    y_att_p = _prompt_attention(pp, bp, seq, top_k_p)
    y_ssd_p, ssm_p, conv_p = _ssd(pp["xbc"], pp["z_ssd"], pp["dt"], jnp.zeros((bp, SUBLANES, conv_dim), F32), None,
                                  conv_w[0], conv_b[0], a_log[0], d_skip[0], ssd_norm_g[0], bp, seq)
    merged_p = _merge(y_att_p, y_ssd_p, wb, pp["gates"], min(1024, seq))
    y_p = _out_proj(merged_p, w_out_b, xp, final_norm_g, min(512, seq))

    ms_rows = bs * n_tok
    rows_pad = _round_up(ms_rows, LANES)
    xs = x_sample.reshape(ms_rows, d)
    xs_pad = jnp.pad(xs, ((0, rows_pad - ms_rows), (0, 0))) if rows_pad != ms_rows else xs
    pos_s = past + (jnp.arange(rows_pad, dtype=I32) % n_tok)
    xn_s = _rmsnorm_bf16(xs_pad, norm_g[0], rows_pad)
    ps = _projections(xn_s, w_in0, pos_s, 1, rows_pad, idx_ln_w[0], idx_ln_b[0], dt_bias[0])
    real = lambda a: a[:ms_rows]

    qi_rows = jnp.transpose(ps["qi"][:, :ms_rows], (1, 0, 2)).reshape(bs, n_tok * IDX_HEADS, IDX_DIM)
    wi_col = real(ps["wi"]).reshape(bs, n_tok * IDX_HEADS, 1)
    scores = _sample_scores(qi_rows, wi_col, jnp.swapaxes(cache_idx_k[0], 1, 2), page_table, n_tok)
    scores = scores.reshape(ms_rows, past)
    if rows_pad != ms_rows:
        scores = jnp.pad(scores, ((0, rows_pad - ms_rows), (0, 0)))
    keep, keep_new = _sample_select(scores, ps["qi"], ps["ki_bf"], ps["wi"], n_tok, ms_rows, top_k_s)
    keep = real(keep).reshape(bs, n_tok, past * KV_HEADS)
    keep_new = real(keep_new).reshape(bs, n_tok, rows_pad * KV_HEADS)
    q_rows = jnp.transpose(ps["q"][:, :ms_rows], (1, 0, 2)).reshape(bs, n_tok * ATT_HEADS, HEAD_DIM)
    z_rows = real(ps["z_att"]).reshape(bs, n_tok * ATT_HEADS, HEAD_DIM)
    k_new, v_new = ps["k"], ps["v"]
    ck = cache_k[0].reshape(n_phys, page * KV_HEADS, HEAD_DIM)
    cv = cache_v[0].reshape(n_phys, page * KV_HEADS, HEAD_DIM)
    y_att_s = _sample_attention(q_rows, z_rows, keep, keep_new, k_new, v_new, ck, cv, page_table, n_tok)
    y_att_s = y_att_s.reshape(ms_rows, att_w)

    conv0_s = jnp.pad(state_conv[0], ((0, 0), (SUBLANES - (CONV_W - 1), 0), (0, 0)))
    y_ssd_s, ssm_s, conv_s = _ssd(real(ps["xbc"]), real(ps["z_ssd"]), real(ps["dt"]), conv0_s, state_ssm[0],
                                  conv_w[0], conv_b[0], a_log[0], d_skip[0], ssd_norm_g[0], bs, n_tok)
    if rows_pad != ms_rows:
        padr = lambda a: jnp.pad(a, ((0, rows_pad - ms_rows), (0, 0)))
        y_att_s, y_ssd_s = padr(y_att_s), padr(y_ssd_s)
    merged_s = _merge(y_att_s, y_ssd_s, wb, ps["gates"], rows_pad)
    y_s = real(_out_proj(merged_s, w_out_b, xs_pad, final_norm_g, rows_pad))

    kv5 = lambda a, b, t: a[:b * t * KV_HEADS].reshape(1, b, t, KV_HEADS, HEAD_DIM)
    return (y_p.reshape(bp, seq, d), y_s.reshape(bs, n_tok, d),
            kv5(pp["k"], bp, seq), kv5(pp["v"], bp, seq), pp["ki"].reshape(1, bp, seq, IDX_DIM),
            ssm_p[None], conv_p[None],
            kv5(ps["k"], bs, n_tok), kv5(ps["v"], bs, n_tok), real(ps["ki"]).reshape(1, bs, n_tok, IDX_DIM),
            ssm_s[None], conv_s[None])
```

```python
import functools

import numpy as np
import jax
import jax.numpy as jnp
from jax import lax
from jax.experimental import pallas as pl
from jax.experimental.pallas import tpu as pltpu

F32, BF16, I32 = jnp.float32, jnp.bfloat16, jnp.int32

ATT_HEADS = 16
KV_HEADS = 4
HEAD_DIM = 128
IDX_HEADS = 16
IDX_DIM = 64
TOPK_MAX = 256
ROPE_THETA = 10000.0
SSD_HEAD_DIM = 64
SSD_GROUPS = 8
D_STATE = 128
CONV_W = 4
CHUNK = 128
EPS = 1e-6

LANES = 128
SUBLANES = 8
BF16_ROWS = 16
VMEM_LIMIT = 56 * 1024 * 1024
INT_MIN = -2 ** 31
KEY_OF_MOST_NEGATIVE_FLOAT = INT_MIN + 0x00800000
MASKED = -1e30
LOG2E = 1.4426950408889634
PAGES_PER_STEP = 32
SCORE_PAGES_PER_STEP = 32
GROUPS_PER_TRIP = 8
IDX_HEADS_PER_DOT = 8


def _cparams(sem):
    return pltpu.CompilerParams(dimension_semantics=sem, vmem_limit_bytes=VMEM_LIMIT)


def _nt_dot(a, b):
    return lax.dot_general(a, b, (((1,), (1,)), ((), ())), preferred_element_type=F32)


def _tile_lanes(x, n):
    return x if n == 1 else jnp.concatenate([x] * n, axis=1)


def _split3(x):
    hi = x.astype(BF16)
    r1 = x - hi.astype(F32)
    mid = r1.astype(BF16)
    lo = (r1 - mid.astype(F32)).astype(BF16)
    return hi, mid, lo


def _dot_exact_rhs(x, e):
    hi, mid, lo = _split3(x)
    d = functools.partial(jnp.dot, preferred_element_type=F32)
    return d(hi, e) + d(mid, e) + d(lo, e)


def _dot_exact_lhs(e, x):
    hi, mid, lo = _split3(x)
    d = functools.partial(jnp.dot, preferred_element_type=F32)
    return d(e, hi) + d(e, mid) + d(e, lo)


def _norm_kernel(x_ref, g_ref, o_ref):
    x = x_ref[...]
    ms = jnp.mean(x * x, axis=-1, keepdims=True)
    o_ref[...] = (x * lax.rsqrt(ms + EPS) * g_ref[...]).astype(o_ref.dtype)


def _rmsnorm_bf16(x, g, tm):
    m, d = x.shape
    return pl.pallas_call(
        _norm_kernel, grid=(m // tm,),
        in_specs=[pl.BlockSpec((tm, d), lambda i: (i, 0)), pl.BlockSpec((1, d), lambda i: (0, 0))],
        out_specs=pl.BlockSpec((tm, d), lambda i: (i, 0)),
        out_shape=jax.ShapeDtypeStruct((m, d), BF16),
        compiler_params=_cparams(("parallel",)), name="rmsnorm")(x, g.reshape(1, d))


def _mm_kernel(x_ref, w_ref, *rest, epilogue, n_aux):
    acc = _nt_dot(x_ref[...], w_ref[...].astype(BF16))
    epilogue(acc, rest[:n_aux], rest[n_aux:])


def _matmul(x, wt, row0, n, epilogue, aux, aux_specs, out_shapes, out_specs, tm, tn, name):
    m, k = x.shape
    kern = functools.partial(_mm_kernel, epilogue=epilogue, n_aux=len(aux))
    return pl.pallas_call(
        kern, grid=(m // tm, n // tn),
        in_specs=[pl.BlockSpec((tm, k), lambda i, j: (i, 0)),
                  pl.BlockSpec((pl.Element(tn), pl.Element(k)),
                               lambda i, j: (pl.multiple_of(row0 + j * tn, SUBLANES), 0))] + list(aux_specs),
        out_specs=out_specs, out_shape=out_shapes,
        compiler_params=_cparams(("parallel", "parallel")), name=name)(x, wt, *aux)


def _rope_half(x, cos, sin_signed, half):
    if 2 * half == LANES:
        partner = pltpu.roll(x, half, 1)
    else:
        partner = jnp.concatenate([x[:, half:], x[:, :half]], axis=1)
    return x * cos + partner * sin_signed


def _store_heads(outs, c, r, head_major, token_major_first):
    n_heads = outs[0].shape[0] // r.shape[0] if token_major_first else None
    for idx, o in enumerate(outs):
        if token_major_first and idx == 0:
            o[pl.ds(c, r.shape[0], stride=n_heads), :] = r.astype(o.dtype)
        elif head_major:
            o[c] = r.astype(o.dtype)
        else:
            o[:, c * HEAD_DIM:(c + 1) * HEAD_DIM] = r.astype(o.dtype)


def _epi_rope128(acc, aux, outs, post_scale=None, head_major=False, token_major_first=False):
    cos, sin = aux[0][...], aux[1][...]
    for c in range(acc.shape[1] // HEAD_DIM):
        r = _rope_half(acc[:, c * HEAD_DIM:(c + 1) * HEAD_DIM], cos, sin, HEAD_DIM // 2)
        if post_scale is not None:
            r = r * post_scale
        _store_heads(outs, c, r, head_major, token_major_first)


def _epi_heads(acc, aux, outs):
    for c in range(acc.shape[1] // HEAD_DIM):
        _store_heads(outs, c, acc[:, c * HEAD_DIM:(c + 1) * HEAD_DIM], False, True)


def _epi_store(acc, aux, outs, fn=None):
    val = acc if fn is None else fn(acc)
    for o in outs:
        o[...] = val.astype(o.dtype)


def _epi_qi(acc, aux, outs):
    cos, sin = aux[0][...], aux[1][...]
    for hh in range(acc.shape[1] // IDX_DIM):
        x = acc[:, hh * IDX_DIM:(hh + 1) * IDX_DIM]
        outs[0][hh] = _rope_half(x, cos, sin, IDX_DIM // 2).astype(outs[0].dtype)


def _epi_kiwi(acc, aux, outs, idx_scale):
    lnw, lnb, cos, sin = (a[...] for a in aux)
    ki = acc[:, :IDX_DIM]
    mu = jnp.mean(ki, axis=-1, keepdims=True)
    kc = ki - mu
    y = kc * lax.rsqrt(jnp.mean(kc * kc, axis=-1, keepdims=True) + EPS) * lnw + lnb
    r = _rope_half(y, cos, sin, IDX_DIM // 2)
    outs[0][...] = r
    outs[1][...] = r.astype(BF16)
    outs[2][...] = acc[:, IDX_DIM:] * idx_scale


def _epi_softplus(acc, aux, outs):
    x = acc + aux[0][...]
    outs[0][...] = jnp.maximum(x, 0.0) + jnp.log1p(jnp.exp(-jnp.abs(x)))


def _silu(x):
    return x * jax.nn.sigmoid(x)


def _rope_tables(pos, d):
    inv = ROPE_THETA ** (-jnp.arange(0, d, 2, dtype=F32) / d)
    ang = pos.astype(F32)[:, None] * inv[None, :]
    cos, sin = jnp.cos(ang), jnp.sin(ang)
    return jnp.concatenate([cos, cos], axis=-1), jnp.concatenate([-sin, sin], axis=-1)


def _projections(xn, wt, pos_rows, n_pos_blocks, tm, idx_ln_w, idx_ln_b, dt_bias):
    m, d = xn.shape
    att_w, kv_w = ATT_HEADS * HEAD_DIM, KV_HEADS * HEAD_DIM
    d_inner = 2 * d
    conv_dim = d_inner + 2 * SSD_GROUPS * D_STATE
    ssd_heads = d_inner // SSD_HEAD_DIM
    splits = (att_w, kv_w, kv_w, att_w, IDX_HEADS * IDX_DIM, IDX_DIM, IDX_HEADS, d_inner, conv_dim,
              ssd_heads, d, d)
    assert sum(splits) == wt.shape[0]
    off = np.concatenate([[0], np.cumsum(splits)]).astype(int)
    assert all(o % BF16_ROWS == 0 for o in off), "segment rows of the weight must start on packed-row boundaries"
    seg = lambda a, b: (wt, int(off[a]), int(off[b] - off[a]))

    cos128, sin128 = _rope_tables(pos_rows, HEAD_DIM)
    cos64, sin64 = _rope_tables(pos_rows, IDX_DIM)
    pos_map = lambda i, j: (i % n_pos_blocks, 0)
    t128 = [pl.BlockSpec((tm, HEAD_DIM), pos_map)] * 2
    t64 = [pl.BlockSpec((tm, IDX_DIM), pos_map)] * 2
    tile = lambda tn: pl.BlockSpec((tm, tn), lambda i, j: (i, j))
    sds = lambda n, dt: jax.ShapeDtypeStruct((m, n), dt)
    tn = 512
    tw = 1024

    (q,) = _matmul(xn, *seg(0, 1), functools.partial(_epi_rope128, post_scale=HEAD_DIM ** -0.5 * LOG2E, head_major=True),
                   [cos128, sin128], t128, [jax.ShapeDtypeStruct((ATT_HEADS, m, HEAD_DIM), BF16)],
                   [pl.BlockSpec((tw // HEAD_DIM, tm, HEAD_DIM), lambda i, j: (j, i, 0))], tm, tw, "proj_q")
    kv_shapes = [jax.ShapeDtypeStruct((m * KV_HEADS, HEAD_DIM), F32), sds(kv_w, BF16)]
    kv_specs = [pl.BlockSpec((tm * KV_HEADS, HEAD_DIM), lambda i, j: (i, 0)), tile(kv_w)]
    k, k_bf = _matmul(xn, *seg(1, 2), functools.partial(_epi_rope128, token_major_first=True), [cos128, sin128], t128,
                      kv_shapes, kv_specs, tm, kv_w, "proj_k")
    v, v_bf = _matmul(xn, *seg(2, 3), _epi_heads, [], [], kv_shapes, kv_specs, tm, kv_w, "proj_v")
    (z_att,) = _matmul(xn, *seg(3, 4), functools.partial(_epi_store, fn=_silu), [], [],
                       [sds(att_w, BF16)], [tile(tw)], tm, tw, "proj_zatt")
    qi_tn = 4 * IDX_DIM
    (qi,) = _matmul(xn, *seg(4, 5), _epi_qi, [cos64, sin64], t64,
                    [jax.ShapeDtypeStruct((IDX_HEADS, m, IDX_DIM), BF16)],
                    [pl.BlockSpec((4, tm, IDX_DIM), lambda i, j: (j, i, 0))], tm, qi_tn, "proj_qi")
    kiwi_n = IDX_DIM + IDX_HEADS
    idx_scale = IDX_HEADS ** -0.5 * IDX_DIM ** -0.5
    row1 = lambda n: pl.BlockSpec((1, n), lambda i, j: (0, 0))
    ki, ki_bf, wi = _matmul(
        xn, *seg(5, 7), functools.partial(_epi_kiwi, idx_scale=idx_scale),
        [idx_ln_w.reshape(1, IDX_DIM), idx_ln_b.reshape(1, IDX_DIM), cos64, sin64],
        [row1(IDX_DIM), row1(IDX_DIM)] + t64,
        [sds(IDX_DIM, F32), sds(IDX_DIM, BF16), sds(IDX_HEADS, F32)],
        [pl.BlockSpec((tm, IDX_DIM), lambda i, j: (i, 0))] * 2 + [pl.BlockSpec((tm, IDX_HEADS), lambda i, j: (i, 0))],
        tm, kiwi_n, "proj_kiwi")
    (z_ssd,) = _matmul(xn, *seg(7, 8), functools.partial(_epi_store, fn=_silu), [], [],
                       [sds(d_inner, BF16)], [tile(tw)], tm, tw, "proj_zssd")
    (xbc,) = _matmul(xn, *seg(8, 9), _epi_store, [], [], [sds(conv_dim, F32)], [tile(tw)], tm, tw, "proj_xbc")
    (dt,) = _matmul(xn, *seg(9, 10), _epi_softplus, [dt_bias.reshape(1, ssd_heads)], [row1(ssd_heads)],
                    [sds(ssd_heads, F32)], [pl.BlockSpec((tm, ssd_heads), lambda i, j: (i, 0))],
                    tm, ssd_heads, "proj_dt")
    (gates,) = _matmul(xn, *seg(10, 12), functools.partial(_epi_store, fn=jax.nn.sigmoid), [], [],
                       [sds(2 * d, BF16)], [tile(tw)], tm, tw, "proj_gates")
    return dict(q=q, k=k, k_bf=k_bf, v=v, v_bf=v_bf, z_att=z_att, qi=qi, ki=ki, ki_bf=ki_bf, wi=wi,
                z_ssd=z_ssd, xbc=xbc, dt=dt, gates=gates)


def _key_to_float(key):
    return pltpu.bitcast(jnp.where(key < 0, key ^ jnp.int32(0x7FFFFFFF), key), F32)


def _kth_largest(count_ge, shape, top_k):
    kf = jnp.float32(top_k)

    def pick(t, n, cands):
        for cand, c in zip(cands, count_ge([_key_to_float(cand) for cand in cands])):
            c = jnp.broadcast_to(c, shape)
            t, n = jnp.where(c >= kf, cand, t), jnp.where(c >= kf, c, n)
        return t, n

    t, n = pick(jnp.full(shape, INT_MIN, I32), jnp.zeros(shape, F32), [jnp.zeros(shape, I32)])

    def body(i, tn):
        t, n = tn
        return pick(t, n, [t | lax.shift_left(jnp.int32(1), jnp.int32(30) - i)])

    t, n = lax.fori_loop(0, 31, body, (t, n))
    return _key_to_float(jnp.maximum(t, jnp.int32(KEY_OF_MOST_NEGATIVE_FLOAT))), n


def _lane_tile_sum(x):
    part = x[:, :LANES]
    for c in range(1, x.shape[1] // LANES):
        part = part + x[:, c * LANES:(c + 1) * LANES]
    return part


def _attn_kernel(q_ref, qi_ref, wi_ref, z_ref, k_ref, v_ref, ki_ref, o_ref,
                 score_ref, wexp_ref, bias_ref, s_ref, m_ref, l_ref, acc_ref, cand_ref, *, tq, tk, tkc, hps, top_k):
    qt = pl.program_id(1)
    n_kc = ((qt + 1) * tq + tk - 1) // tk
    nl = tk // LANES
    row_pos = qt * tq + lax.broadcasted_iota(I32, (tq, 1), 0)

    wi = wi_ref[...]
    for h in range(IDX_HEADS):
        wexp_ref[h] = jnp.broadcast_to(wi[:, h:h + 1], (tq, LANES))

    def score_chunk(kc, carry):
        off = pl.multiple_of(kc * tk, tk)
        ki = ki_ref[pl.ds(off, tk), :]
        sc = None
        for h0 in range(0, IDX_HEADS, IDX_HEADS_PER_DOT):
            hs = slice(h0, h0 + IDX_HEADS_PER_DOT)
            d = _nt_dot(qi_ref[hs].reshape(IDX_HEADS_PER_DOT * tq, IDX_DIM), ki)
            r = jnp.maximum(d, 0.0) * _tile_lanes(wexp_ref[hs].reshape(IDX_HEADS_PER_DOT * tq, LANES), nl)
            for h in range(IDX_HEADS_PER_DOT):
                sc = r[h * tq:(h + 1) * tq] if sc is None else sc + r[h * tq:(h + 1) * tq]
        kpos = off + lax.broadcasted_iota(I32, (1, tk), 1)
        score_ref[:, pl.ds(off, tk)] = jnp.where(kpos <= row_pos, sc, -jnp.inf)
        return carry

    lax.fori_loop(0, n_kc, score_chunk, 0)

    def count(cands, strict=False):
        for i, c in enumerate(cands):
            cand_ref[i] = c

        def body(kc, cnts):
            sc = score_ref[:, pl.ds(pl.multiple_of(kc * tk, tk), tk)]
            hit = lambda c: jnp.where((sc > c) if strict else (sc >= c), 1.0, 0.0)
            return tuple(cnt + _lane_tile_sum(hit(_tile_lanes(cand_ref[i], nl))) for i, cnt in enumerate(cnts))

        cnts = lax.fori_loop(0, n_kc, body, tuple(jnp.zeros((tq, LANES), F32) for _ in cands))
        return [jnp.sum(cnt, axis=1, keepdims=True) for cnt in cnts]

    thr1, n_ge = _kth_largest(count, (tq, LANES), top_k)
    thr = _tile_lanes(thr1, nl)
    has_ties = jnp.max(n_ge) > jnp.float32(top_k)

    @pl.when(jnp.logical_not(has_ties))
    def _():
        def bias_chunk(kc, carry):
            off = pl.multiple_of(kc * tk, tk)
            bias_ref[:, pl.ds(off, tk)] = jnp.where(score_ref[:, pl.ds(off, tk)] >= thr, 0.0, MASKED)
            return carry

        lax.fori_loop(0, n_kc, bias_chunk, 0)

    @pl.when(has_ties)
    def _():
        room = jnp.float32(top_k) - count([thr1], strict=True)[0]
        ri = lax.broadcasted_iota(I32, (tk, tk), 0)
        ci = lax.broadcasted_iota(I32, (tk, tk), 1)
        prefix = jnp.where(ri <= ci, 1.0, 0.0).astype(BF16)

        def bias_chunk(kc, seen):
            off = pl.multiple_of(kc * tk, tk)
            key = score_ref[:, pl.ds(off, tk)]
            eq = jnp.where(key == thr, 1.0, 0.0)
            rank = seen + jnp.dot(eq.astype(BF16), prefix, preferred_element_type=F32)
            keep = (key > thr) | ((key == thr) & (rank <= room))
            bias_ref[:, pl.ds(off, tk)] = jnp.where(keep, 0.0, MASKED)
            return seen + jnp.sum(eq, axis=1, keepdims=True)

        lax.fori_loop(0, n_kc, bias_chunk, jnp.zeros((tq, 1), F32))

    n_kcc = ((qt + 1) * tq + tkc - 1) // tkc
    nlc = tkc // LANES

    def bias_tail(kc, carry):
        bias_ref[:, pl.ds(pl.multiple_of(kc * tk, tk), tk)] = jnp.full((tq, tk), MASKED, F32)
        return carry

    lax.fori_loop(n_kc, n_kcc * (tkc // tk), bias_tail, 0)
    rep = ATT_HEADS // KV_HEADS

    rows = hps * tq
    n_stacks = ATT_HEADS // hps
    kv_lanes = lambda g: slice((g * hps // rep) * HEAD_DIM, (g * hps // rep + 1) * HEAD_DIM)

    def logits_chunk(g, ks):
        slot = g % 2
        gs = kv_lanes(g)
        qg = q_ref[g * hps:(g + 1) * hps].reshape(rows, HEAD_DIM)
        bias = bias_ref[:, ks]
        s = jnp.concatenate([bias] * hps, axis=0) + _nt_dot(qg, k_ref[ks, gs])
        s_ref[slot, :, ks] = s
        part = s[:, :LANES]
        for c in range(1, nlc):
            part = jnp.maximum(part, s[:, c * LANES:(c + 1) * LANES])
        m_ref[slot] = jnp.maximum(m_ref[slot], part)

    def pv_chunk(g, ks):
        slot = g % 2
        gs = kv_lanes(g)
        p = jnp.exp2(s_ref[slot, :, ks] - _tile_lanes(m_ref[slot], nlc))
        l_ref[...] += _lane_tile_sum(p)
        acc_ref[...] += jnp.dot(p.astype(BF16), v_ref[ks, gs], preferred_element_type=F32)

    for stage in range(n_stacks + 1):
        g_logits = stage if stage < n_stacks else None
        g_pv = stage - 1 if stage > 0 else None
        if g_logits is not None:
            m_ref[g_logits % 2] = jnp.full((rows, LANES), MASKED, F32)
        if g_pv is not None:
            slot = g_pv % 2
            m_ref[slot] = jnp.broadcast_to(jnp.max(m_ref[slot], axis=1, keepdims=True), (rows, LANES))
            l_ref[...] = jnp.zeros((rows, LANES), F32)
            acc_ref[...] = jnp.zeros((rows, HEAD_DIM), F32)

        def stage_chunk(kc, carry, g_logits=g_logits, g_pv=g_pv):
            ks = pl.ds(pl.multiple_of(kc * tkc, tkc), tkc)
            if g_logits is not None:
                logits_chunk(g_logits, ks)
            if g_pv is not None:
                pv_chunk(g_pv, ks)
            return carry

        lax.fori_loop(0, n_kcc, stage_chunk, 0)
        if g_pv is not None:
            o = acc_ref[...] / jnp.sum(l_ref[...], axis=1, keepdims=True)
            for r in range(hps):
                hs = slice((g_pv * hps + r) * HEAD_DIM, (g_pv * hps + r + 1) * HEAD_DIM)
                o_ref[:, hs] = (o[r * tq:(r + 1) * tq] * z_ref[:, hs].astype(F32)).astype(o_ref.dtype)


def _prompt_attention(p, batch, seq, top_k):
    m = batch * seq
    tq = min(256, seq)
    tk = min(512, seq)
    nq = seq // tq
    att_w, kv_w = ATT_HEADS * HEAD_DIM, KV_HEADS * HEAD_DIM
    hps = 2
    rows = lambda w: pl.BlockSpec((tq, w), lambda b, t: (b * nq + t, 0))
    whole = lambda w: pl.BlockSpec((seq, w), lambda b, t: (b, 0), pipeline_mode=pl.Buffered(1))
    tkc = min(1024, seq)
    kern = functools.partial(_attn_kernel, tq=tq, tk=tk, tkc=tkc, hps=hps, top_k=top_k)
    return pl.pallas_call(
        kern, grid=(batch, nq),
        in_specs=[pl.BlockSpec((ATT_HEADS, tq, HEAD_DIM), lambda b, t: (0, b * nq + t, 0)),
                  pl.BlockSpec((IDX_HEADS, tq, IDX_DIM), lambda b, t: (0, b * nq + t, 0)),
                  rows(IDX_HEADS), rows(att_w), whole(kv_w), whole(kv_w), whole(IDX_DIM)],
        out_specs=rows(att_w),
        out_shape=jax.ShapeDtypeStruct((m, att_w), BF16),
        scratch_shapes=[pltpu.VMEM((tq, seq), F32),
                        pltpu.VMEM((IDX_HEADS, tq, LANES), F32),
                        pltpu.VMEM((tq, seq), F32),
                        pltpu.VMEM((2, hps * tq, seq), F32),
                        pltpu.VMEM((2, hps * tq, LANES), F32),
                        pltpu.VMEM((hps * tq, LANES), F32),
                        pltpu.VMEM((hps * tq, HEAD_DIM), F32),
                        pltpu.VMEM((1, tq, LANES), F32)],
        compiler_params=_cparams(("parallel", "arbitrary")), name="prompt_attention",
    )(p["q"], p["qi"], p["wi"], p["z_att"], p["k_bf"], p["v_bf"], p["ki_bf"])


def _sample_score_kernel(pt_ref, qi_ref, wi_ref, *rest, n_pages, n_tok):
    page_refs, o_ref = rest[:n_pages], rest[n_pages]
    qi = qi_ref[...]
    w = jnp.broadcast_to(wi_ref[...], (qi.shape[0], LANES))
    for k in range(n_pages):
        kit = page_refs[k][...].astype(BF16)
        page = kit.shape[1]
        r = jnp.maximum(jnp.dot(qi, kit, preferred_element_type=F32), 0.0) * _tile_lanes(w, page // LANES)
        sc = jnp.sum(r.reshape(n_tok, IDX_HEADS, page), axis=1)
        o_ref[:, k * page:(k + 1) * page] = sc


def _sample_scores(qi_rows, wi_col, cache_idx_kt, page_table, n_tok):
    bsz, n_pages_total = page_table.shape
    page = cache_idx_kt.shape[2]
    g = SCORE_PAGES_PER_STEP if n_pages_total % SCORE_PAGES_PER_STEP == 0 else 1
    rows = n_tok * IDX_HEADS
    page_spec = lambda k: pl.BlockSpec((None, IDX_DIM, page), lambda b, s, pt: (pt[b, s * g + k], 0, 0))
    grid_spec = pltpu.PrefetchScalarGridSpec(
        num_scalar_prefetch=1, grid=(bsz, n_pages_total // g),
        in_specs=[pl.BlockSpec((None, rows, IDX_DIM), lambda b, s, pt: (b, 0, 0)),
                  pl.BlockSpec((None, rows, 1), lambda b, s, pt: (b, 0, 0))] + [page_spec(k) for k in range(g)],
        out_specs=pl.BlockSpec((None, n_tok, g * page), lambda b, s, pt: (b, 0, s)))
    kern = functools.partial(_sample_score_kernel, n_pages=g, n_tok=n_tok)
    return pl.pallas_call(
        kern, grid_spec=grid_spec,
        out_shape=jax.ShapeDtypeStruct((bsz, n_tok, n_pages_total * page), F32),
        compiler_params=_cparams(("parallel", "arbitrary")), name="sample_scores",
    )(page_table, qi_rows, wi_col, *([cache_idx_kt] * g))


def _sample_select_kernel(sc_ref, qi_ref, ki_ref, wi_ref, rexp_ref, keep_ref, keep_new_ref, *, n_tok, n_rows, top_k):
    rows, past = sc_ref.shape
    nk = ki_ref.shape[0]
    wi = wi_ref[...]
    ki = ki_ref[...]
    sc_new = jnp.zeros((rows, nk), F32)
    for h in range(IDX_HEADS):
        w = jnp.broadcast_to(wi[:, h:h + 1], (rows, LANES))
        sc_new = sc_new + jnp.maximum(_nt_dot(qi_ref[h], ki), 0.0) * _tile_lanes(w, nk // LANES)
    r = lax.broadcasted_iota(I32, (rows, nk), 0)
    c = lax.broadcasted_iota(I32, (rows, nk), 1)
    ok = (r // n_tok == c // n_tok) & (c % n_tok <= r % n_tok) & (r < n_rows) & (c < n_rows)
    sc_new = jnp.where(ok, sc_new, -jnp.inf)

    def count(cands, strict=False):
        def one(cand):
            hit = lambda sc, n: jnp.where((sc > _tile_lanes(cand, n)) if strict else (sc >= _tile_lanes(cand, n)), 1.0, 0.0)
            return jnp.sum(_lane_tile_sum(hit(sc_ref[...], past // LANES)) + _lane_tile_sum(hit(sc_new, nk // LANES)),
                           axis=1, keepdims=True)

        return [one(cand) for cand in cands]

    thr, _ = _kth_largest(count, (rows, LANES), top_k)
    rexp = rexp_ref[...]
    xw = rexp.shape[1]
    room = jnp.float32(top_k) - count([thr], strict=True)[0]
    ri = lax.broadcasted_iota(I32, (LANES, LANES), 0)
    ci = lax.broadcasted_iota(I32, (LANES, LANES), 1)
    prefix = jnp.where(ri <= ci, 1.0, 0.0).astype(BF16)

    def select(key_tile, seen):
        eq = jnp.where(key_tile == thr, 1.0, 0.0)
        rank = seen + jnp.dot(eq.astype(BF16), prefix, preferred_element_type=F32)
        keep = (key_tile > thr) | ((key_tile == thr) & (rank <= room))
        sel = jnp.where(keep, 1.0, 0.0).astype(BF16)
        return (jnp.dot(sel, rexp, preferred_element_type=F32).astype(BF16),
                seen + jnp.sum(eq, axis=1, keepdims=True))

    def past_tile(ct, seen):
        src = pl.ds(pl.multiple_of(ct * LANES, LANES), LANES)
        kept, seen = select(sc_ref[:, src], seen)
        keep_ref[:, pl.ds(pl.multiple_of(ct * xw, xw), xw)] = kept
        return seen

    seen = lax.fori_loop(0, past // LANES, past_tile, jnp.zeros((rows, 1), F32))
    for ct in range(nk // LANES):
        kept, seen = select(sc_new[:, ct * LANES:(ct + 1) * LANES], seen)
        keep_new_ref[:, ct * xw:(ct + 1) * xw] = kept


def _sample_select(scores, qi, ki_new_bf, wi, n_tok, n_rows, top_k):
    rows, past = scores.shape
    nk = ki_new_bf.shape[0]
    rexp = (jnp.arange(LANES * KV_HEADS)[None, :] // KV_HEADS == jnp.arange(LANES)[:, None]).astype(BF16)
    kern = functools.partial(_sample_select_kernel, n_tok=n_tok, n_rows=n_rows, top_k=top_k)
    full = lambda shape: pl.BlockSpec(shape, lambda i: (0,) * len(shape))
    return pl.pallas_call(
        kern, grid=(1,),
        in_specs=[full((rows, past)), full((IDX_HEADS, rows, IDX_DIM)), full((nk, IDX_DIM)), full((rows, IDX_HEADS)),
                  full(rexp.shape)],
        out_specs=[full((rows, past * KV_HEADS)), full((rows, nk * KV_HEADS))],
        out_shape=[jax.ShapeDtypeStruct((rows, past * KV_HEADS), BF16),
                   jax.ShapeDtypeStruct((rows, nk * KV_HEADS), BF16)],
        compiler_params=_cparams(("arbitrary",)), name="sample_select")(scores, qi, ki_new_bf, wi, rexp)


def _sample_attn_kernel(pt_ref, q_ref, z_ref, keep_ref, keep_new_ref, knew_ref, vnew_ref, *rest, n_pages, n_tok, page):
    k_refs, v_refs = rest[:n_pages], rest[n_pages:2 * n_pages]
    o_ref, m_ref, l_ref, acc_ref, s_ref = rest[2 * n_pages:]
    step = pl.program_id(1)
    rows = n_tok * ATT_HEADS
    cols = page * KV_HEADS
    rep = ATT_HEADS // KV_HEADS
    nl = cols // LANES

    @pl.when(step == 0)
    def _():
        m_ref[...] = jnp.full(m_ref.shape, MASKED, F32)
        l_ref[...] = jnp.zeros(l_ref.shape, F32)
        acc_ref[...] = jnp.zeros(acc_ref.shape, F32)

    q = q_ref[...]
    rr = lax.broadcasted_iota(I32, (rows, cols), 0)
    cc = lax.broadcasted_iota(I32, (rows, cols), 1)
    head_bias = jnp.where((cc % KV_HEADS) == ((rr % ATT_HEADS) // rep), 0.0, MASKED)

    def update(blocks):
        mx = None
        for i, (load_k, _, load_keep) in enumerate(blocks):
            kp = load_k().astype(BF16)
            kb = (load_keep().astype(F32) - 1.0) * (-MASKED)
            kb_rows = jnp.concatenate(
                [jnp.broadcast_to(kb[t:t + 1, :], (ATT_HEADS, cols)) for t in range(n_tok)], axis=0)
            s = _nt_dot(q, kp) + (kb_rows + head_bias)
            s_ref[:, i * cols:(i + 1) * cols] = s
            part = s[:, :LANES]
            for c in range(1, nl):
                part = jnp.maximum(part, s[:, c * LANES:(c + 1) * LANES])
            mx = part if mx is None else jnp.maximum(mx, part)
        m_old = m_ref[...]
        m_new = jnp.maximum(m_old, jnp.max(mx, axis=1, keepdims=True))
        alpha = jnp.exp2(m_old - m_new)
        m_t = _tile_lanes(m_new, nl)
        lsum = jnp.zeros((rows, LANES), F32)
        pv = jnp.zeros((rows, HEAD_DIM), F32)
        for i, (_, load_v, _) in enumerate(blocks):
            p = jnp.exp2(s_ref[:, i * cols:(i + 1) * cols] - m_t)
            lsum = lsum + _lane_tile_sum(p)
            pv = pv + jnp.dot(p.astype(BF16), load_v().astype(BF16), preferred_element_type=F32)
        l_ref[...] = alpha * l_ref[...] + lsum
        acc_ref[...] = alpha * acc_ref[...] + pv
        m_ref[...] = m_new

    update([(lambda k=k: k_refs[k][...], lambda k=k: v_refs[k][...], lambda k=k: keep_ref[:, k * cols:(k + 1) * cols])
            for k in range(n_pages)])

    @pl.when(step == pl.num_programs(1) - 1)
    def _():
        update([(lambda c=c: knew_ref[c * cols:(c + 1) * cols, :], lambda c=c: vnew_ref[c * cols:(c + 1) * cols, :],
                 lambda c=c: keep_new_ref[:, c * cols:(c + 1) * cols]) for c in range(knew_ref.shape[0] // cols)])
        lsum = jnp.sum(l_ref[...], axis=1, keepdims=True)
        o_ref[...] = (acc_ref[...] / lsum * z_ref[...].astype(F32)).astype(o_ref.dtype)


def _sample_attention(q_rows, z_rows, keep, keep_new, k_new, v_new, cache_k, cache_v, page_table, n_tok):
    bsz, n_pages_total = page_table.shape
    cols = cache_k.shape[1]
    page = cols // KV_HEADS
    g = PAGES_PER_STEP if n_pages_total % PAGES_PER_STEP == 0 else 1
    rows = n_tok * ATT_HEADS
    per_b = lambda shape: pl.BlockSpec((None,) + shape, lambda b, s, pt: (b,) + (0,) * len(shape))
    const = lambda shape: pl.BlockSpec(shape, lambda b, s, pt: (0,) * len(shape))
    page_spec = lambda k: pl.BlockSpec((None, cols, HEAD_DIM), lambda b, s, pt: (pt[b, s * g + k], 0, 0))
    grid_spec = pltpu.PrefetchScalarGridSpec(
        num_scalar_prefetch=1, grid=(bsz, n_pages_total // g),
        in_specs=[per_b((rows, HEAD_DIM)), per_b((rows, HEAD_DIM)),
                  pl.BlockSpec((None, n_tok, g * cols), lambda b, s, pt: (b, 0, s)),
                  per_b((n_tok, keep_new.shape[-1])), const(k_new.shape), const(v_new.shape)]
                 + [page_spec(k) for k in range(g)] * 2,
        out_specs=per_b((rows, HEAD_DIM)),
        scratch_shapes=[pltpu.VMEM((rows, LANES), F32), pltpu.VMEM((rows, LANES), F32),
                        pltpu.VMEM((rows, HEAD_DIM), F32), pltpu.VMEM((rows, g * cols), F32)])
    kern = functools.partial(_sample_attn_kernel, n_pages=g, n_tok=n_tok, page=page)
    return pl.pallas_call(
        kern, grid_spec=grid_spec,
        out_shape=jax.ShapeDtypeStruct((bsz, rows, HEAD_DIM), BF16),
        compiler_params=_cparams(("parallel", "arbitrary")), name="sample_attention",
    )(page_table, q_rows, z_rows, keep, keep_new, k_new, v_new, *([cache_k] * g), *([cache_v] * g))


def _ssd_kernel(*refs, t_rows, q_rows, has_state):
    if has_state:
        (xbc_ref, z_ref, dt_ref, conv0_ref, s0_ref, cw_ref, cb_ref, alog_ref, dskip_ref, ng_ref, e64_ref,
         y_ref, sfin_ref, cnew_ref, xp_ref, act_ref, dtp_ref, acst_ref, acsc_ref, st_ref) = refs
    else:
        (xbc_ref, z_ref, dt_ref, conv0_ref, cw_ref, cb_ref, alog_ref, dskip_ref, ng_ref, e64_ref,
         y_ref, sfin_ref, cnew_ref, xp_ref, act_ref, dtp_ref, acst_ref, acsc_ref, st_ref) = refs
        s0_ref = None
    c = pl.program_id(1)
    q = q_rows
    halo = SUBLANES
    n_heads = dt_ref.shape[-1]
    d_inner = n_heads * SSD_HEAD_DIM
    gw = d_inner // SSD_GROUPS
    hpg = n_heads // SSD_GROUPS
    conv_dim = xbc_ref.shape[-1]

    @pl.when(c == 0)
    def _():
        xp_ref[0:halo, :] = conv0_ref[...]
        if t_rows < q:
            xp_ref[halo:, :] = jnp.zeros((q, conv_dim), F32)
            dtp_ref[...] = jnp.zeros(dtp_ref.shape, F32)
        for g in range(SSD_GROUPS):
            if has_state:
                st_ref[g] = s0_ref[g * hpg:(g + 1) * hpg].reshape(gw, D_STATE).T
            else:
                st_ref[g] = jnp.zeros((D_STATE, gw), F32)

    xp_ref[halo:halo + t_rows, :] = xbc_ref[...]
    dtp_ref[0:t_rows, :] = dt_ref[...]

    cblk = LANES
    for cbi in range(conv_dim // cblk):
        cs = slice(cbi * cblk, (cbi + 1) * cblk)
        acc = jnp.broadcast_to(cb_ref[:, cs], (q, cblk))
        for tap in range(CONV_W):
            lo = halo - (CONV_W - 1) + tap
            acc = acc + xp_ref[lo:lo + q, cs] * cw_ref[tap:tap + 1, cs]
        act_ref[:, cs] = _silu(acc)

    dt = dtp_ref[...]
    a = -jnp.exp(alog_ref[...])
    ri = lax.broadcasted_iota(I32, (q, q), 0)
    ci = lax.broadcasted_iota(I32, (q, q), 1)
    tril = ri >= ci
    a_cs = _dot_exact_lhs(jnp.where(tril, 1.0, 0.0).astype(BF16), dt * a)
    a_last = a_cs[q - 1:q, :]
    acst_ref[...] = a_cs.T
    for h in range(n_heads):
        acsc_ref[h] = jnp.broadcast_to(a_cs[:, h:h + 1], (q, LANES))
    dt_b = dt.astype(BF16)
    dte_b = jnp.exp(a_last - a_cs).astype(BF16)
    ea_hi, ea_mid, _ = _split3(jnp.exp(a_cs))
    pad = jnp.zeros((SUBLANES - 2, n_heads), F32)
    row_pieces = _split3(jnp.concatenate([jnp.exp(a_last), dskip_ref[...], pad], axis=0))
    mxu = functools.partial(jnp.dot, preferred_element_type=F32)
    lane = lax.broadcasted_iota(I32, (1, LANES), 1)
    lo_half = lane < SSD_HEAD_DIM

    def group_body(g, carry):
        xs = act_ref[:, pl.ds(pl.multiple_of(g * gw, gw), gw)]
        bm = act_ref[:, pl.ds(pl.multiple_of(d_inner + g * D_STATE, D_STATE), D_STATE)]
        cm = act_ref[:, pl.ds(pl.multiple_of(d_inner + SSD_GROUPS * D_STATE + g * D_STATE, D_STATE), D_STATE)]
        e64g = e64_ref[:, pl.ds(pl.multiple_of(g * gw, gw), gw)]
        dt_x, dte_x = mxu(dt_b, e64g), mxu(dte_b, e64g)
        ea_x = mxu(ea_hi, e64g) + mxu(ea_mid, e64g)
        rows_x = mxu(row_pieces[0], e64g) + mxu(row_pieces[1], e64g) + mxu(row_pieces[2], e64g)
        cdec_x, dskip_x = rows_x[0:1], rows_x[1:2]
        xdt = xs * dt_x
        xdt_b = xdt.astype(BF16)
        bm_b, cm_b = bm.astype(BF16), cm.astype(BF16)
        cb = _nt_dot(cm_b, bm_b)
        st = st_ref[g]
        y_off = jnp.dot(cm_b, st.astype(BF16), preferred_element_type=F32) * ea_x
        y_parts = []
        for pr in range(hpg // 2):
            xpair = xdt_b[:, pr * LANES:(pr + 1) * LANES]
            halves = (jnp.where(lo_half, xpair, jnp.zeros_like(xpair)), jnp.where(lo_half, jnp.zeros_like(xpair), xpair))
            yp = jnp.zeros((q, LANES), F32)
            for s in range(2):
                hl = 2 * pr + s
                seg = acsc_ref[g * hpg + hl][:, :q] - acst_ref[pl.ds(g * hpg + hl, 1), :]
                lmat = jnp.where(tril, jnp.exp(jnp.where(tril, seg, 0.0)), 0.0)
                yp = yp + jnp.dot((cb * lmat).astype(BF16), halves[s], preferred_element_type=F32)
            y_parts.append(yp)
        y = jnp.concatenate(y_parts, axis=1) + y_off + dskip_x * xs
        st_ref[g] = cdec_x * st + jnp.dot(bm.T.astype(BF16), (xdt * dte_x).astype(BF16), preferred_element_type=F32)
        gsl = pl.ds(pl.multiple_of(g * gw, gw), gw)
        yz = y[0:t_rows] * z_ref[:, gsl].astype(F32)
        ms = jnp.mean(yz * yz, axis=-1, keepdims=True)
        y_ref[:, gsl] = (yz * lax.rsqrt(ms + EPS) * ng_ref[:, gsl]).astype(y_ref.dtype)
        return carry

    def group_batch(i, carry):
        for u in range(GROUPS_PER_TRIP):
            group_body(GROUPS_PER_TRIP * i + u, carry)
        return carry

    lax.fori_loop(0, SSD_GROUPS // GROUPS_PER_TRIP, group_batch, 0)

    @pl.when(c == pl.num_programs(1) - 1)
    def _():
        cnew_ref[...] = xp_ref[halo + t_rows - (CONV_W - 1):halo + t_rows, :]
        for g in range(SSD_GROUPS):
            sfin_ref[g * hpg:(g + 1) * hpg] = st_ref[g].T.reshape(hpg, SSD_HEAD_DIM, D_STATE)

    if t_rows == q:
        @pl.when(c < pl.num_programs(1) - 1)
        def _():
            xp_ref[0:halo, :] = xp_ref[q:q + halo, :]


def _ssd(xbc, z, dt, conv0, s0, conv_w, conv_b, a_log, d_skip, norm_g, bsz, seq):
    conv_dim = xbc.shape[-1]
    n_heads = dt.shape[-1]
    d_inner = n_heads * SSD_HEAD_DIM
    t_rows = min(CHUNK, seq)
    n_chunks = seq // t_rows
    q_rows = CHUNK if t_rows == CHUNK else _round_up(t_rows, BF16_ROWS)
    has_state = s0 is not None
    hpg = n_heads // SSD_GROUPS
    e64 = (jnp.arange(d_inner)[None, :] // SSD_HEAD_DIM == jnp.arange(n_heads)[:, None]).astype(BF16)
    x3 = lambda a: a.reshape(bsz, seq, a.shape[-1])
    rows = lambda w: pl.BlockSpec((None, t_rows, w), lambda b, c: (b, c, 0))
    const = lambda shape: pl.BlockSpec(shape, lambda b, c: (0,) * len(shape))
    state_spec = pl.BlockSpec((None, n_heads, SSD_HEAD_DIM, D_STATE), lambda b, c: (b, 0, 0, 0))
    args = [x3(xbc), x3(z), x3(dt), conv0]
    in_specs = [rows(conv_dim), rows(d_inner), rows(n_heads),
                pl.BlockSpec((None, SUBLANES, conv_dim), lambda b, c: (b, 0, 0))]
    if has_state:
        args.append(s0)
        in_specs.append(state_spec)
    args += [conv_w, conv_b.reshape(1, conv_dim), a_log.reshape(1, n_heads), d_skip.reshape(1, n_heads),
             norm_g.reshape(1, d_inner), e64]
    in_specs += [const((CONV_W, conv_dim)), const((1, conv_dim)), const((1, n_heads)), const((1, n_heads)),
                 const((1, d_inner)), const(e64.shape)]
    kern = functools.partial(_ssd_kernel, t_rows=t_rows, q_rows=q_rows, has_state=has_state)
    y, sfin, cnew = pl.pallas_call(
        kern, grid=(bsz, n_chunks), in_specs=in_specs,
        out_specs=[rows(d_inner), state_spec,
                   pl.BlockSpec((None, CONV_W - 1, conv_dim), lambda b, c: (b, 0, 0))],
        out_shape=[jax.ShapeDtypeStruct((bsz, seq, d_inner), BF16),
                   jax.ShapeDtypeStruct((bsz, n_heads, SSD_HEAD_DIM, D_STATE), F32),
                   jax.ShapeDtypeStruct((bsz, CONV_W - 1, conv_dim), F32)],
        scratch_shapes=[pltpu.VMEM((SUBLANES + q_rows, conv_dim), F32),
                        pltpu.VMEM((q_rows, conv_dim), F32),
                        pltpu.VMEM((q_rows, n_heads), F32),
                        pltpu.VMEM((n_heads, q_rows), F32),
                        pltpu.VMEM((n_heads, q_rows, LANES), F32),
                        pltpu.VMEM((SSD_GROUPS, D_STATE, hpg * SSD_HEAD_DIM), F32)],
        compiler_params=_cparams(("parallel", "arbitrary")), name="ssd")(*args)
    return y.reshape(bsz * seq, d_inner), sfin, cnew


def _merge_kernel(ya_ref, ys_ref, wa_ref, *rest):
    ws_refs, (ga_ref, gs_ref, o_ref) = rest[:-3], rest[-3:]
    kb = wa_ref.shape[0]
    ya = jnp.dot(ya_ref[...], wa_ref[...], preferred_element_type=F32)
    yb = jnp.dot(ys_ref[:, :kb], ws_refs[0][...], preferred_element_type=F32)
    for i in range(1, len(ws_refs)):
        yb = yb + jnp.dot(ys_ref[:, i * kb:(i + 1) * kb], ws_refs[i][...], preferred_element_type=F32)
    o_ref[...] = (ga_ref[...].astype(F32) * ya + gs_ref[...].astype(F32) * yb).astype(o_ref.dtype)


def _merge(y_att, y_ssd, wb, gates, tm):
    m, att_w = y_att.shape
    d_inner = y_ssd.shape[1]
    d = wb.shape[1]
    tn = 512
    nj = d // tn
    n_ssd = d_inner // att_w
    w_blk = lambda r: pl.BlockSpec((att_w, tn), lambda i, j: (r, j))
    return pl.pallas_call(
        _merge_kernel, grid=(m // tm, nj),
        in_specs=[pl.BlockSpec((tm, att_w), lambda i, j: (i, 0)), pl.BlockSpec((tm, d_inner), lambda i, j: (i, 0))]
                 + [w_blk(r) for r in range(1 + n_ssd)]
                 + [pl.BlockSpec((tm, tn), lambda i, j: (i, j)), pl.BlockSpec((tm, tn), lambda i, j: (i, j + nj))],
        out_specs=pl.BlockSpec((tm, tn), lambda i, j: (i, j)),
        out_shape=jax.ShapeDtypeStruct((m, d), BF16),
        compiler_params=_cparams(("parallel", "parallel")), name="merge",
    )(y_att, y_ssd, *([wb] * (1 + n_ssd)), gates, gates)


def _out_kernel(m_ref, w_ref, x_ref, g_ref, o_ref):
    h = x_ref[...] + jnp.dot(m_ref[...], w_ref[...], preferred_element_type=F32)
    ms = jnp.mean(h * h, axis=-1, keepdims=True)
    o_ref[...] = h * lax.rsqrt(ms + EPS) * g_ref[...]


def _out_proj(merged, w_out, x, final_g, tm):
    m, d = x.shape
    return pl.pallas_call(
        _out_kernel, grid=(m // tm,),
        in_specs=[pl.BlockSpec((tm, d), lambda i: (i, 0)), pl.BlockSpec((d, d), lambda i: (0, 0)),
                  pl.BlockSpec((tm, d), lambda i: (i, 0)), pl.BlockSpec((1, d), lambda i: (0, 0))],
        out_specs=pl.BlockSpec((tm, d), lambda i: (i, 0)),
        out_shape=jax.ShapeDtypeStruct((m, d), F32),
        compiler_params=_cparams(("parallel",)), name="out_proj")(merged, w_out, x, final_g.reshape(1, d))


def _round_up(x, n):
    return (x + n - 1) // n * n


def kernel(x_prompt, x_sample, cache_k, cache_v, cache_idx_k, state_ssm, state_conv, page_table, norm_g, w_in, conv_w,
           conv_b, dt_bias, a_log, d_skip, ssd_norm_g, idx_ln_w, idx_ln_b, w_branch, w_out, final_norm_g):
    assert w_in.shape[0] == 1, "single-layer trunk"
    bp, seq, d = x_prompt.shape
    bs, n_tok, _ = x_sample.shape
    n_phys, page = cache_k.shape[1], cache_k.shape[2]
    past = page_table.shape[1] * page
    att_w, kv_w = ATT_HEADS * HEAD_DIM, KV_HEADS * HEAD_DIM
    d_inner = 2 * d
    conv_dim = d_inner + 2 * SSD_GROUPS * D_STATE
    top_k_p = max(1, min(TOPK_MAX, seq // 4))
    top_k_s = max(1, min(TOPK_MAX, (past + n_tok) // 4))

    wb = w_branch[0].astype(BF16)
    w_out_b = w_out[0].astype(BF16)
    w_in0 = jnp.swapaxes(w_in[0], 0, 1)

    mp = bp * seq
    tm_p = min(1024, seq)
    xp = x_prompt.reshape(mp, d)
    xn_p = _rmsnorm_bf16(xp, norm_g[0], min(512, seq))
    pp = _projections(xn_p, w_in0, jnp.arange(seq, dtype=I32), seq // tm_p, tm_p, idx_ln_w[0], idx_ln_b[0], dt_bias[0])
    y_att_p = _prompt_attention(pp, bp, seq, top_k_p)
    y_ssd_p, ssm_p, conv_p = _ssd(pp["xbc"], pp["z_ssd"], pp["dt"], jnp.zeros((bp, SUBLANES, conv_dim), F32), None,
                                  conv_w[0], conv_b[0], a_log[0], d_skip[0], ssd_norm_g[0], bp, seq)
    merged_p = _merge(y_att_p, y_ssd_p, wb, pp["gates"], min(1024, seq))
    y_p = _out_proj(merged_p, w_out_b, xp, final_norm_g, min(512, seq))

    ms_rows = bs * n_tok
    rows_pad = _round_up(ms_rows, LANES)
    xs = x_sample.reshape(ms_rows, d)
    xs_pad = jnp.pad(xs, ((0, rows_pad - ms_rows), (0, 0))) if rows_pad != ms_rows else xs
    pos_s = past + (jnp.arange(rows_pad, dtype=I32) % n_tok)
    xn_s = _rmsnorm_bf16(xs_pad, norm_g[0], rows_pad)
    ps = _projections(xn_s, w_in0, pos_s, 1, rows_pad, idx_ln_w[0], idx_ln_b[0], dt_bias[0])
    real = lambda a: a[:ms_rows]

    qi_rows = jnp.transpose(ps["qi"][:, :ms_rows], (1, 0, 2)).reshape(bs, n_tok * IDX_HEADS, IDX_DIM)
    wi_col = real(ps["wi"]).reshape(bs, n_tok * IDX_HEADS, 1)
    scores = _sample_scores(qi_rows, wi_col, jnp.swapaxes(cache_idx_k[0], 1, 2), page_table, n_tok)
    scores = scores.reshape(ms_rows, past)
    if rows_pad != ms_rows:
        scores = jnp.pad(scores, ((0, rows_pad - ms_rows), (0, 0)))
    keep, keep_new = _sample_select(scores, ps["qi"], ps["ki_bf"], ps["wi"], n_tok, ms_rows, top_k_s)
    keep = real(keep).reshape(bs, n_tok, past * KV_HEADS)
    keep_new = real(keep_new).reshape(bs, n_tok, rows_pad * KV_HEADS)
    q_rows = jnp.transpose(ps["q"][:, :ms_rows], (1, 0, 2)).reshape(bs, n_tok * ATT_HEADS, HEAD_DIM)
    z_rows = real(ps["z_att"]).reshape(bs, n_tok * ATT_HEADS, HEAD_DIM)
    k_new, v_new = ps["k"], ps["v"]
    ck = cache_k[0].reshape(n_phys, page * KV_HEADS, HEAD_DIM)
    cv = cache_v[0].reshape(n_phys, page * KV_HEADS, HEAD_DIM)
    y_att_s = _sample_attention(q_rows, z_rows, keep, keep_new, k_new, v_new, ck, cv, page_table, n_tok)
    y_att_s = y_att_s.reshape(ms_rows, att_w)

    conv0_s = jnp.pad(state_conv[0], ((0, 0), (SUBLANES - (CONV_W - 1), 0), (0, 0)))
    y_ssd_s, ssm_s, conv_s = _ssd(real(ps["xbc"]), real(ps["z_ssd"]), real(ps["dt"]), conv0_s, state_ssm[0],
                                  conv_w[0], conv_b[0], a_log[0], d_skip[0], ssd_norm_g[0], bs, n_tok)
    if rows_pad != ms_rows:
        padr = lambda a: jnp.pad(a, ((0, rows_pad - ms_rows), (0, 0)))
        y_att_s, y_ssd_s = padr(y_att_s), padr(y_ssd_s)
    merged_s = _merge(y_att_s, y_ssd_s, wb, ps["gates"], rows_pad)
    y_s = real(_out_proj(merged_s, w_out_b, xs_pad, final_norm_g, rows_pad))

    kv5 = lambda a, b, t: a[:b * t * KV_HEADS].reshape(1, b, t, KV_HEADS, HEAD_DIM)
    return (y_p.reshape(bp, seq, d), y_s.reshape(bs, n_tok, d),
            kv5(pp["k"], bp, seq), kv5(pp["v"], bp, seq), pp["ki"].reshape(1, bp, seq, IDX_DIM),
            ssm_p[None], conv_p[None],
            kv5(ps["k"], bs, n_tok), kv5(ps["v"], bs, n_tok), real(ps["ki"]).reshape(1, bs, n_tok, IDX_DIM),
            ssm_s[None], conv_s[None])
```

```python
import functools

import numpy as np
import jax
import jax.numpy as jnp
from jax import lax
from jax.experimental import pallas as pl
from jax.experimental.pallas import tpu as pltpu

F32, BF16, I32 = jnp.float32, jnp.bfloat16, jnp.int32

ATT_HEADS = 16
KV_HEADS = 4
HEAD_DIM = 128
IDX_HEADS = 16
IDX_DIM = 64
TOPK_MAX = 256
ROPE_THETA = 10000.0
SSD_HEAD_DIM = 64
SSD_GROUPS = 8
D_STATE = 128
CONV_W = 4
CHUNK = 128
EPS = 1e-6

LANES = 128
SUBLANES = 8
BF16_ROWS = 16
VMEM_LIMIT = 56 * 1024 * 1024
INT_MIN = -2 ** 31
KEY_OF_MOST_NEGATIVE_FLOAT = INT_MIN + 0x00800000
MASKED = -1e30
LOG2E = 1.4426950408889634
PAGES_PER_STEP = 32
SCORE_PAGES_PER_STEP = 32
GROUPS_PER_TRIP = 8
IDX_HEADS_PER_DOT = 8
NORM_ROWS = 512
PROJ_ROWS = 1024
PROJ_COLS = 512
PROJ_COLS_WIDE = 1024
ATTN_Q_ROWS = 256
ATTN_SCORE_KEYS = 512
ATTN_SOFTMAX_KEYS = 1024
ATTN_HEADS_PER_STACK = 2
MERGE_ROWS = 1024
MERGE_COLS = 512
CONV_LANES = LANES


def _cparams(sem):
    return pltpu.CompilerParams(dimension_semantics=sem, vmem_limit_bytes=VMEM_LIMIT)


def _nt_dot(a, b):
    return lax.dot_general(a, b, (((1,), (1,)), ((), ())), preferred_element_type=F32)


def _tile_lanes(x, n):
    return x if n == 1 else jnp.concatenate([x] * n, axis=1)


def _split3(x):
    hi = x.astype(BF16)
    r1 = x - hi.astype(F32)
    mid = r1.astype(BF16)
    lo = (r1 - mid.astype(F32)).astype(BF16)
    return hi, mid, lo


def _dot_exact_lhs(e, x):
    hi, mid, lo = _split3(x)
    d = functools.partial(jnp.dot, preferred_element_type=F32)
    return d(e, hi) + d(e, mid) + d(e, lo)


def _norm_kernel(x_ref, g_ref, o_ref):
    x = x_ref[...]
    ms = jnp.mean(x * x, axis=-1, keepdims=True)
    o_ref[...] = (x * lax.rsqrt(ms + EPS) * g_ref[...]).astype(o_ref.dtype)


def _rmsnorm_bf16(x, g, tm):
    m, d = x.shape
    return pl.pallas_call(
        _norm_kernel, grid=(m // tm,),
        in_specs=[pl.BlockSpec((tm, d), lambda i: (i, 0)), pl.BlockSpec((1, d), lambda i: (0, 0))],
        out_specs=pl.BlockSpec((tm, d), lambda i: (i, 0)),
        out_shape=jax.ShapeDtypeStruct((m, d), BF16),
        compiler_params=_cparams(("parallel",)), name="rmsnorm")(x, g.reshape(1, d))


def _mm_kernel(x_ref, w_ref, *rest, epilogue, n_aux):
    acc = _nt_dot(x_ref[...], w_ref[...].astype(BF16))
    epilogue(acc, rest[:n_aux], rest[n_aux:])


def _matmul(x, wt, row0, n, epilogue, aux, aux_specs, out_shapes, out_specs, tm, tn, name):
    m, k = x.shape
    kern = functools.partial(_mm_kernel, epilogue=epilogue, n_aux=len(aux))
    return pl.pallas_call(
        kern, grid=(m // tm, n // tn),
        in_specs=[pl.BlockSpec((tm, k), lambda i, j: (i, 0)),
                  pl.BlockSpec((pl.Element(tn), pl.Element(k)),
                               lambda i, j: (pl.multiple_of(row0 + j * tn, SUBLANES), 0))] + list(aux_specs),
        out_specs=out_specs, out_shape=out_shapes,
        compiler_params=_cparams(("parallel", "parallel")), name=name)(x, wt, *aux)


def _rope_half(x, cos, sin_signed, half):
    if 2 * half == LANES:
        partner = pltpu.roll(x, half, 1)
    else:
        partner = jnp.concatenate([x[:, half:], x[:, :half]], axis=1)
    return x * cos + partner * sin_signed


def _store_heads(outs, c, r, head_major, token_major_first):
    n_heads = outs[0].shape[0] // r.shape[0] if token_major_first else None
    for idx, o in enumerate(outs):
        if token_major_first and idx == 0:
            o[pl.ds(c, r.shape[0], stride=n_heads), :] = r.astype(o.dtype)
        elif head_major:
            o[c] = r.astype(o.dtype)
        else:
            o[:, c * HEAD_DIM:(c + 1) * HEAD_DIM] = r.astype(o.dtype)


def _epi_rope128(acc, aux, outs, post_scale=None, head_major=False, token_major_first=False):
    cos, sin = aux[0][...], aux[1][...]
    for c in range(acc.shape[1] // HEAD_DIM):
        r = _rope_half(acc[:, c * HEAD_DIM:(c + 1) * HEAD_DIM], cos, sin, HEAD_DIM // 2)
        if post_scale is not None:
            r = r * post_scale
        _store_heads(outs, c, r, head_major, token_major_first)


def _epi_heads(acc, aux, outs):
    for c in range(acc.shape[1] // HEAD_DIM):
        _store_heads(outs, c, acc[:, c * HEAD_DIM:(c + 1) * HEAD_DIM], False, True)


def _epi_store(acc, aux, outs, fn=None):
    val = acc if fn is None else fn(acc)
    for o in outs:
        o[...] = val.astype(o.dtype)


def _epi_qi(acc, aux, outs):
    cos, sin = aux[0][...], aux[1][...]
    for hh in range(acc.shape[1] // IDX_DIM):
        x = acc[:, hh * IDX_DIM:(hh + 1) * IDX_DIM]
        outs[0][hh] = _rope_half(x, cos, sin, IDX_DIM // 2).astype(outs[0].dtype)


def _epi_kiwi(acc, aux, outs, idx_scale):
    lnw, lnb, cos, sin = (a[...] for a in aux)
    ki = acc[:, :IDX_DIM]
    mu = jnp.mean(ki, axis=-1, keepdims=True)
    kc = ki - mu
    y = kc * lax.rsqrt(jnp.mean(kc * kc, axis=-1, keepdims=True) + EPS) * lnw + lnb
    r = _rope_half(y, cos, sin, IDX_DIM // 2)
    outs[0][...] = r
    outs[1][...] = r.astype(BF16)
    outs[2][...] = acc[:, IDX_DIM:] * idx_scale


def _epi_softplus(acc, aux, outs):
    x = acc + aux[0][...]
    outs[0][...] = jnp.maximum(x, 0.0) + jnp.log1p(jnp.exp(-jnp.abs(x)))


def _silu(x):
    return x * jax.nn.sigmoid(x)


def _rope_tables(pos, d):
    inv = ROPE_THETA ** (-jnp.arange(0, d, 2, dtype=F32) / d)
    ang = pos.astype(F32)[:, None] * inv[None, :]
    cos, sin = jnp.cos(ang), jnp.sin(ang)
    return jnp.concatenate([cos, cos], axis=-1), jnp.concatenate([-sin, sin], axis=-1)


def _projections(xn, wt, pos_rows, n_pos_blocks, tm, idx_ln_w, idx_ln_b, dt_bias):
    m, d = xn.shape
    att_w, kv_w = ATT_HEADS * HEAD_DIM, KV_HEADS * HEAD_DIM
    d_inner = 2 * d
    conv_dim = d_inner + 2 * SSD_GROUPS * D_STATE
    ssd_heads = d_inner // SSD_HEAD_DIM
    splits = (att_w, kv_w, kv_w, att_w, IDX_HEADS * IDX_DIM, IDX_DIM, IDX_HEADS, d_inner, conv_dim,
              ssd_heads, d, d)
    assert sum(splits) == wt.shape[0]
    off = np.concatenate([[0], np.cumsum(splits)]).astype(int)
    assert all(o % BF16_ROWS == 0 for o in off), "segment rows of the weight must start on packed-row boundaries"
    seg = lambda a, b: (wt, int(off[a]), int(off[b] - off[a]))

    cos128, sin128 = _rope_tables(pos_rows, HEAD_DIM)
    cos64, sin64 = _rope_tables(pos_rows, IDX_DIM)
    pos_map = lambda i, j: (i % n_pos_blocks, 0)
    t128 = [pl.BlockSpec((tm, HEAD_DIM), pos_map)] * 2
    t64 = [pl.BlockSpec((tm, IDX_DIM), pos_map)] * 2
    tile = lambda tn: pl.BlockSpec((tm, tn), lambda i, j: (i, j))
    sds = lambda n, dt: jax.ShapeDtypeStruct((m, n), dt)
    tn, tw = PROJ_COLS, PROJ_COLS_WIDE

    (q,) = _matmul(xn, *seg(0, 1), functools.partial(_epi_rope128, post_scale=HEAD_DIM ** -0.5 * LOG2E, head_major=True),
                   [cos128, sin128], t128, [jax.ShapeDtypeStruct((ATT_HEADS, m, HEAD_DIM), BF16)],
                   [pl.BlockSpec((tw // HEAD_DIM, tm, HEAD_DIM), lambda i, j: (j, i, 0))], tm, tw, "proj_q")
    kv_shapes = [jax.ShapeDtypeStruct((m * KV_HEADS, HEAD_DIM), F32), sds(kv_w, BF16)]
    kv_specs = [pl.BlockSpec((tm * KV_HEADS, HEAD_DIM), lambda i, j: (i, 0)), tile(kv_w)]
    k, k_bf = _matmul(xn, *seg(1, 2), functools.partial(_epi_rope128, token_major_first=True), [cos128, sin128], t128,
                      kv_shapes, kv_specs, tm, kv_w, "proj_k")
    v, v_bf = _matmul(xn, *seg(2, 3), _epi_heads, [], [], kv_shapes, kv_specs, tm, kv_w, "proj_v")
    (z_att,) = _matmul(xn, *seg(3, 4), functools.partial(_epi_store, fn=_silu), [], [],
                       [sds(att_w, BF16)], [tile(tw)], tm, tw, "proj_zatt")
    qi_tn = 4 * IDX_DIM
    (qi,) = _matmul(xn, *seg(4, 5), _epi_qi, [cos64, sin64], t64,
                    [jax.ShapeDtypeStruct((IDX_HEADS, m, IDX_DIM), BF16)],
                    [pl.BlockSpec((4, tm, IDX_DIM), lambda i, j: (j, i, 0))], tm, qi_tn, "proj_qi")
    kiwi_n = IDX_DIM + IDX_HEADS
    idx_scale = IDX_HEADS ** -0.5 * IDX_DIM ** -0.5
    row1 = lambda n: pl.BlockSpec((1, n), lambda i, j: (0, 0))
    ki, ki_bf, wi = _matmul(
        xn, *seg(5, 7), functools.partial(_epi_kiwi, idx_scale=idx_scale),
        [idx_ln_w.reshape(1, IDX_DIM), idx_ln_b.reshape(1, IDX_DIM), cos64, sin64],
        [row1(IDX_DIM), row1(IDX_DIM)] + t64,
        [sds(IDX_DIM, F32), sds(IDX_DIM, BF16), sds(IDX_HEADS, F32)],
        [pl.BlockSpec((tm, IDX_DIM), lambda i, j: (i, 0))] * 2 + [pl.BlockSpec((tm, IDX_HEADS), lambda i, j: (i, 0))],
        tm, kiwi_n, "proj_kiwi")
    (z_ssd,) = _matmul(xn, *seg(7, 8), functools.partial(_epi_store, fn=_silu), [], [],
                       [sds(d_inner, BF16)], [tile(tw)], tm, tw, "proj_zssd")
    (xbc,) = _matmul(xn, *seg(8, 9), _epi_store, [], [], [sds(conv_dim, F32)], [tile(tw)], tm, tw, "proj_xbc")
    (dt,) = _matmul(xn, *seg(9, 10), _epi_softplus, [dt_bias.reshape(1, ssd_heads)], [row1(ssd_heads)],
                    [sds(ssd_heads, F32)], [pl.BlockSpec((tm, ssd_heads), lambda i, j: (i, 0))],
                    tm, ssd_heads, "proj_dt")
    (gates,) = _matmul(xn, *seg(10, 12), functools.partial(_epi_store, fn=jax.nn.sigmoid), [], [],
                       [sds(2 * d, BF16)], [tile(tw)], tm, tw, "proj_gates")
    return dict(q=q, k=k, k_bf=k_bf, v=v, v_bf=v_bf, z_att=z_att, qi=qi, ki=ki, ki_bf=ki_bf, wi=wi,
                z_ssd=z_ssd, xbc=xbc, dt=dt, gates=gates)


def _key_to_float(key):
    return pltpu.bitcast(jnp.where(key < 0, key ^ jnp.int32(0x7FFFFFFF), key), F32)


def _kth_largest(count_ge, shape, top_k):
    kf = jnp.float32(top_k)

    def pick(t, n, cands):
        for cand, c in zip(cands, count_ge([_key_to_float(cand) for cand in cands])):
            c = jnp.broadcast_to(c, shape)
            t, n = jnp.where(c >= kf, cand, t), jnp.where(c >= kf, c, n)
        return t, n

    t, n = pick(jnp.full(shape, INT_MIN, I32), jnp.zeros(shape, F32), [jnp.zeros(shape, I32)])

    def body(i, tn):
        t, n = tn
        return pick(t, n, [t | lax.shift_left(jnp.int32(1), jnp.int32(30) - i)])

    t, n = lax.fori_loop(0, 31, body, (t, n))
    return _key_to_float(jnp.maximum(t, jnp.int32(KEY_OF_MOST_NEGATIVE_FLOAT))), n


def _lane_tile_sum(x):
    part = x[:, :LANES]
    for c in range(1, x.shape[1] // LANES):
        part = part + x[:, c * LANES:(c + 1) * LANES]
    return part


def _attn_kernel(q_ref, qi_ref, wi_ref, z_ref, k_ref, v_ref, ki_ref, o_ref,
                 score_ref, wexp_ref, bias_ref, s_ref, m_ref, l_ref, acc_ref, cand_ref, *, tq, tk, tkc, hps, top_k):
    qt = pl.program_id(1)
    n_kc = ((qt + 1) * tq + tk - 1) // tk
    nl = tk // LANES
    row_pos = qt * tq + lax.broadcasted_iota(I32, (tq, 1), 0)

    wi = wi_ref[...]
    for h in range(IDX_HEADS):
        wexp_ref[h] = jnp.broadcast_to(wi[:, h:h + 1], (tq, LANES))

    def score_chunk(kc, carry):
        off = pl.multiple_of(kc * tk, tk)
        ki = ki_ref[pl.ds(off, tk), :]
        sc = None
        for h0 in range(0, IDX_HEADS, IDX_HEADS_PER_DOT):
            hs = slice(h0, h0 + IDX_HEADS_PER_DOT)
            d = _nt_dot(qi_ref[hs].reshape(IDX_HEADS_PER_DOT * tq, IDX_DIM), ki)
            r = jnp.maximum(d, 0.0) * _tile_lanes(wexp_ref[hs].reshape(IDX_HEADS_PER_DOT * tq, LANES), nl)
            for h in range(IDX_HEADS_PER_DOT):
                sc = r[h * tq:(h + 1) * tq] if sc is None else sc + r[h * tq:(h + 1) * tq]
        kpos = off + lax.broadcasted_iota(I32, (1, tk), 1)
        score_ref[:, pl.ds(off, tk)] = jnp.where(kpos <= row_pos, sc, -jnp.inf)
        return carry

    lax.fori_loop(0, n_kc, score_chunk, 0)

    def count(cands, strict=False):
        for i, c in enumerate(cands):
            cand_ref[i] = c

        def body(kc, cnts):
            sc = score_ref[:, pl.ds(pl.multiple_of(kc * tk, tk), tk)]
            hit = lambda c: jnp.where((sc > c) if strict else (sc >= c), 1.0, 0.0)
            return tuple(cnt + _lane_tile_sum(hit(_tile_lanes(cand_ref[i], nl))) for i, cnt in enumerate(cnts))

        cnts = lax.fori_loop(0, n_kc, body, tuple(jnp.zeros((tq, LANES), F32) for _ in cands))
        return [jnp.sum(cnt, axis=1, keepdims=True) for cnt in cnts]

    thr1, n_ge = _kth_largest(count, (tq, LANES), top_k)
    thr = _tile_lanes(thr1, nl)
    has_ties = jnp.max(n_ge) > jnp.float32(top_k)

    @pl.when(jnp.logical_not(has_ties))
    def _():
        def bias_chunk(kc, carry):
            off = pl.multiple_of(kc * tk, tk)
            bias_ref[:, pl.ds(off, tk)] = jnp.where(score_ref[:, pl.ds(off, tk)] >= thr, 0.0, MASKED)
            return carry

        lax.fori_loop(0, n_kc, bias_chunk, 0)

    @pl.when(has_ties)
    def _():
        room = jnp.float32(top_k) - count([thr1], strict=True)[0]
        ri = lax.broadcasted_iota(I32, (tk, tk), 0)
        ci = lax.broadcasted_iota(I32, (tk, tk), 1)
        prefix = jnp.where(ri <= ci, 1.0, 0.0).astype(BF16)

        def bias_chunk(kc, seen):
            off = pl.multiple_of(kc * tk, tk)
            key = score_ref[:, pl.ds(off, tk)]
            eq = jnp.where(key == thr, 1.0, 0.0)
            rank = seen + jnp.dot(eq.astype(BF16), prefix, preferred_element_type=F32)
            keep = (key > thr) | ((key == thr) & (rank <= room))
            bias_ref[:, pl.ds(off, tk)] = jnp.where(keep, 0.0, MASKED)
            return seen + jnp.sum(eq, axis=1, keepdims=True)

        lax.fori_loop(0, n_kc, bias_chunk, jnp.zeros((tq, 1), F32))

    n_kcc = ((qt + 1) * tq + tkc - 1) // tkc
    nlc = tkc // LANES

    def bias_tail(kc, carry):
        bias_ref[:, pl.ds(pl.multiple_of(kc * tk, tk), tk)] = jnp.full((tq, tk), MASKED, F32)
        return carry

    lax.fori_loop(n_kc, n_kcc * (tkc // tk), bias_tail, 0)
    rep = ATT_HEADS // KV_HEADS

    rows = hps * tq
    n_stacks = ATT_HEADS // hps
    kv_lanes = lambda g: slice((g * hps // rep) * HEAD_DIM, (g * hps // rep + 1) * HEAD_DIM)

    def logits_chunk(g, ks):
        slot = g % 2
        gs = kv_lanes(g)
        qg = q_ref[g * hps:(g + 1) * hps].reshape(rows, HEAD_DIM)
        bias = bias_ref[:, ks]
        s = jnp.concatenate([bias] * hps, axis=0) + _nt_dot(qg, k_ref[ks, gs])
        s_ref[slot, :, ks] = s
        part = s[:, :LANES]
        for c in range(1, nlc):
            part = jnp.maximum(part, s[:, c * LANES:(c + 1) * LANES])
        m_ref[slot] = jnp.maximum(m_ref[slot], part)

    def pv_chunk(g, ks):
        slot = g % 2
        gs = kv_lanes(g)
        p = jnp.exp2(s_ref[slot, :, ks] - _tile_lanes(m_ref[slot], nlc))
        l_ref[...] += _lane_tile_sum(p)
        acc_ref[...] += jnp.dot(p.astype(BF16), v_ref[ks, gs], preferred_element_type=F32)

    for stage in range(n_stacks + 1):
        g_logits = stage if stage < n_stacks else None
        g_pv = stage - 1 if stage > 0 else None
        if g_logits is not None:
            m_ref[g_logits % 2] = jnp.full((rows, LANES), MASKED, F32)
        if g_pv is not None:
            slot = g_pv % 2
            m_ref[slot] = jnp.broadcast_to(jnp.max(m_ref[slot], axis=1, keepdims=True), (rows, LANES))
            l_ref[...] = jnp.zeros((rows, LANES), F32)
            acc_ref[...] = jnp.zeros((rows, HEAD_DIM), F32)

        def stage_chunk(kc, carry, g_logits=g_logits, g_pv=g_pv):
            ks = pl.ds(pl.multiple_of(kc * tkc, tkc), tkc)
            if g_logits is not None:
                logits_chunk(g_logits, ks)
            if g_pv is not None:
                pv_chunk(g_pv, ks)
            return carry

        lax.fori_loop(0, n_kcc, stage_chunk, 0)
        if g_pv is not None:
            o = acc_ref[...] / jnp.sum(l_ref[...], axis=1, keepdims=True)
            for r in range(hps):
                hs = slice((g_pv * hps + r) * HEAD_DIM, (g_pv * hps + r + 1) * HEAD_DIM)
                o_ref[:, hs] = (o[r * tq:(r + 1) * tq] * z_ref[:, hs].astype(F32)).astype(o_ref.dtype)


def _prompt_attention(p, batch, seq, top_k):
    m = batch * seq
    tq = min(ATTN_Q_ROWS, seq)
    tk = min(ATTN_SCORE_KEYS, seq)
    nq = seq // tq
    att_w, kv_w = ATT_HEADS * HEAD_DIM, KV_HEADS * HEAD_DIM
    hps = ATTN_HEADS_PER_STACK
    rows = lambda w: pl.BlockSpec((tq, w), lambda b, t: (b * nq + t, 0))
    whole = lambda w: pl.BlockSpec((seq, w), lambda b, t: (b, 0), pipeline_mode=pl.Buffered(1))
    tkc = min(ATTN_SOFTMAX_KEYS, seq)
    kern = functools.partial(_attn_kernel, tq=tq, tk=tk, tkc=tkc, hps=hps, top_k=top_k)
    return pl.pallas_call(
        kern, grid=(batch, nq),
        in_specs=[pl.BlockSpec((ATT_HEADS, tq, HEAD_DIM), lambda b, t: (0, b * nq + t, 0)),
                  pl.BlockSpec((IDX_HEADS, tq, IDX_DIM), lambda b, t: (0, b * nq + t, 0)),
                  rows(IDX_HEADS), rows(att_w), whole(kv_w), whole(kv_w), whole(IDX_DIM)],
        out_specs=rows(att_w),
        out_shape=jax.ShapeDtypeStruct((m, att_w), BF16),
        scratch_shapes=[pltpu.VMEM((tq, seq), F32),
                        pltpu.VMEM((IDX_HEADS, tq, LANES), F32),
                        pltpu.VMEM((tq, seq), F32),
                        pltpu.VMEM((2, hps * tq, seq), F32),
                        pltpu.VMEM((2, hps * tq, LANES), F32),
                        pltpu.VMEM((hps * tq, LANES), F32),
                        pltpu.VMEM((hps * tq, HEAD_DIM), F32),
                        pltpu.VMEM((1, tq, LANES), F32)],
        compiler_params=_cparams(("parallel", "arbitrary")), name="prompt_attention",
    )(p["q"], p["qi"], p["wi"], p["z_att"], p["k_bf"], p["v_bf"], p["ki_bf"])


def _sample_score_kernel(pt_ref, qi_ref, wi_ref, *rest, n_pages, n_tok):
    page_refs, o_ref = rest[:n_pages], rest[n_pages]
    qi = qi_ref[...]
    w = jnp.broadcast_to(wi_ref[...], (qi.shape[0], LANES))
    for k in range(n_pages):
        kit = page_refs[k][...].astype(BF16)
        page = kit.shape[1]
        r = jnp.maximum(jnp.dot(qi, kit, preferred_element_type=F32), 0.0) * _tile_lanes(w, page // LANES)
        sc = jnp.sum(r.reshape(n_tok, IDX_HEADS, page), axis=1)
        o_ref[:, k * page:(k + 1) * page] = sc


def _sample_scores(qi_rows, wi_col, cache_idx_kt, page_table, n_tok):
    bsz, n_pages_total = page_table.shape
    page = cache_idx_kt.shape[2]
    g = SCORE_PAGES_PER_STEP if n_pages_total % SCORE_PAGES_PER_STEP == 0 else 1
    rows = n_tok * IDX_HEADS
    page_spec = lambda k: pl.BlockSpec((None, IDX_DIM, page), lambda b, s, pt: (pt[b, s * g + k], 0, 0))
    grid_spec = pltpu.PrefetchScalarGridSpec(
        num_scalar_prefetch=1, grid=(bsz, n_pages_total // g),
        in_specs=[pl.BlockSpec((None, rows, IDX_DIM), lambda b, s, pt: (b, 0, 0)),
                  pl.BlockSpec((None, rows, 1), lambda b, s, pt: (b, 0, 0))] + [page_spec(k) for k in range(g)],
        out_specs=pl.BlockSpec((None, n_tok, g * page), lambda b, s, pt: (b, 0, s)))
    kern = functools.partial(_sample_score_kernel, n_pages=g, n_tok=n_tok)
    return pl.pallas_call(
        kern, grid_spec=grid_spec,
        out_shape=jax.ShapeDtypeStruct((bsz, n_tok, n_pages_total * page), F32),
        compiler_params=_cparams(("parallel", "arbitrary")), name="sample_scores",
    )(page_table, qi_rows, wi_col, *([cache_idx_kt] * g))


def _sample_select_kernel(sc_ref, qi_ref, ki_ref, wi_ref, rexp_ref, keep_ref, keep_new_ref, *, n_tok, n_rows, top_k):
    rows, past = sc_ref.shape
    nk = ki_ref.shape[0]
    wi = wi_ref[...]
    ki = ki_ref[...]
    sc_new = jnp.zeros((rows, nk), F32)
    for h in range(IDX_HEADS):
        w = jnp.broadcast_to(wi[:, h:h + 1], (rows, LANES))
        sc_new = sc_new + jnp.maximum(_nt_dot(qi_ref[h], ki), 0.0) * _tile_lanes(w, nk // LANES)
    r = lax.broadcasted_iota(I32, (rows, nk), 0)
    c = lax.broadcasted_iota(I32, (rows, nk), 1)
    ok = (r // n_tok == c // n_tok) & (c % n_tok <= r % n_tok) & (r < n_rows) & (c < n_rows)
    sc_new = jnp.where(ok, sc_new, -jnp.inf)

    def count(cands, strict=False):
        def one(cand):
            hit = lambda sc, n: jnp.where((sc > _tile_lanes(cand, n)) if strict else (sc >= _tile_lanes(cand, n)), 1.0, 0.0)
            return jnp.sum(_lane_tile_sum(hit(sc_ref[...], past // LANES)) + _lane_tile_sum(hit(sc_new, nk // LANES)),
                           axis=1, keepdims=True)

        return [one(cand) for cand in cands]

    thr, _ = _kth_largest(count, (rows, LANES), top_k)
    rexp = rexp_ref[...]
    xw = rexp.shape[1]
    room = jnp.float32(top_k) - count([thr], strict=True)[0]
    ri = lax.broadcasted_iota(I32, (LANES, LANES), 0)
    ci = lax.broadcasted_iota(I32, (LANES, LANES), 1)
    prefix = jnp.where(ri <= ci, 1.0, 0.0).astype(BF16)

    def select(key_tile, seen):
        eq = jnp.where(key_tile == thr, 1.0, 0.0)
        rank = seen + jnp.dot(eq.astype(BF16), prefix, preferred_element_type=F32)
        keep = (key_tile > thr) | ((key_tile == thr) & (rank <= room))
        sel = jnp.where(keep, 1.0, 0.0).astype(BF16)
        return (jnp.dot(sel, rexp, preferred_element_type=F32).astype(BF16),
                seen + jnp.sum(eq, axis=1, keepdims=True))

    def past_tile(ct, seen):
        src = pl.ds(pl.multiple_of(ct * LANES, LANES), LANES)
        kept, seen = select(sc_ref[:, src], seen)
        keep_ref[:, pl.ds(pl.multiple_of(ct * xw, xw), xw)] = kept
        return seen

    seen = lax.fori_loop(0, past // LANES, past_tile, jnp.zeros((rows, 1), F32))
    for ct in range(nk // LANES):
        kept, seen = select(sc_new[:, ct * LANES:(ct + 1) * LANES], seen)
        keep_new_ref[:, ct * xw:(ct + 1) * xw] = kept


def _sample_select(scores, qi, ki_new_bf, wi, n_tok, n_rows, top_k):
    rows, past = scores.shape
    nk = ki_new_bf.shape[0]
    rexp = (jnp.arange(LANES * KV_HEADS)[None, :] // KV_HEADS == jnp.arange(LANES)[:, None]).astype(BF16)
    kern = functools.partial(_sample_select_kernel, n_tok=n_tok, n_rows=n_rows, top_k=top_k)
    full = lambda shape: pl.BlockSpec(shape, lambda i: (0,) * len(shape))
    return pl.pallas_call(
        kern, grid=(1,),
        in_specs=[full((rows, past)), full((IDX_HEADS, rows, IDX_DIM)), full((nk, IDX_DIM)), full((rows, IDX_HEADS)),
                  full(rexp.shape)],
        out_specs=[full((rows, past * KV_HEADS)), full((rows, nk * KV_HEADS))],
        out_shape=[jax.ShapeDtypeStruct((rows, past * KV_HEADS), BF16),
                   jax.ShapeDtypeStruct((rows, nk * KV_HEADS), BF16)],
        compiler_params=_cparams(("arbitrary",)), name="sample_select")(scores, qi, ki_new_bf, wi, rexp)


def _sample_attn_kernel(pt_ref, q_ref, z_ref, keep_ref, keep_new_ref, knew_ref, vnew_ref, *rest, n_pages, n_tok, page):
    k_refs, v_refs = rest[:n_pages], rest[n_pages:2 * n_pages]
    o_ref, m_ref, l_ref, acc_ref, s_ref = rest[2 * n_pages:]
    step = pl.program_id(1)
    rows = n_tok * ATT_HEADS
    cols = page * KV_HEADS
    rep = ATT_HEADS // KV_HEADS
    nl = cols // LANES

    @pl.when(step == 0)
    def _():
        m_ref[...] = jnp.full(m_ref.shape, MASKED, F32)
        l_ref[...] = jnp.zeros(l_ref.shape, F32)
        acc_ref[...] = jnp.zeros(acc_ref.shape, F32)

    q = q_ref[...]
    rr = lax.broadcasted_iota(I32, (rows, cols), 0)
    cc = lax.broadcasted_iota(I32, (rows, cols), 1)
    head_bias = jnp.where((cc % KV_HEADS) == ((rr % ATT_HEADS) // rep), 0.0, MASKED)

    def update(blocks):
        mx = None
        for i, (load_k, _, load_keep) in enumerate(blocks):
            kp = load_k().astype(BF16)
            kb = (load_keep().astype(F32) - 1.0) * (-MASKED)
            kb_rows = jnp.concatenate(
                [jnp.broadcast_to(kb[t:t + 1, :], (ATT_HEADS, cols)) for t in range(n_tok)], axis=0)
            s = _nt_dot(q, kp) + (kb_rows + head_bias)
            s_ref[:, i * cols:(i + 1) * cols] = s
            part = s[:, :LANES]
            for c in range(1, nl):
                part = jnp.maximum(part, s[:, c * LANES:(c + 1) * LANES])
            mx = part if mx is None else jnp.maximum(mx, part)
        m_old = m_ref[...]
        m_new = jnp.maximum(m_old, jnp.max(mx, axis=1, keepdims=True))
        alpha = jnp.exp2(m_old - m_new)
        m_t = _tile_lanes(m_new, nl)
        lsum = jnp.zeros((rows, LANES), F32)
        pv = jnp.zeros((rows, HEAD_DIM), F32)
        for i, (_, load_v, _) in enumerate(blocks):
            p = jnp.exp2(s_ref[:, i * cols:(i + 1) * cols] - m_t)
            lsum = lsum + _lane_tile_sum(p)
            pv = pv + jnp.dot(p.astype(BF16), load_v().astype(BF16), preferred_element_type=F32)
        l_ref[...] = alpha * l_ref[...] + lsum
        acc_ref[...] = alpha * acc_ref[...] + pv
        m_ref[...] = m_new

    update([(lambda k=k: k_refs[k][...], lambda k=k: v_refs[k][...], lambda k=k: keep_ref[:, k * cols:(k + 1) * cols])
            for k in range(n_pages)])

    @pl.when(step == pl.num_programs(1) - 1)
    def _():
        update([(lambda c=c: knew_ref[c * cols:(c + 1) * cols, :], lambda c=c: vnew_ref[c * cols:(c + 1) * cols, :],
                 lambda c=c: keep_new_ref[:, c * cols:(c + 1) * cols]) for c in range(knew_ref.shape[0] // cols)])
        lsum = jnp.sum(l_ref[...], axis=1, keepdims=True)
        o_ref[...] = (acc_ref[...] / lsum * z_ref[...].astype(F32)).astype(o_ref.dtype)


def _sample_attention(q_rows, z_rows, keep, keep_new, k_new, v_new, cache_k, cache_v, page_table, n_tok):
    bsz, n_pages_total = page_table.shape
    cols = cache_k.shape[1]
    page = cols // KV_HEADS
    g = PAGES_PER_STEP if n_pages_total % PAGES_PER_STEP == 0 else 1
    rows = n_tok * ATT_HEADS
    per_b = lambda shape: pl.BlockSpec((None,) + shape, lambda b, s, pt: (b,) + (0,) * len(shape))
    const = lambda shape: pl.BlockSpec(shape, lambda b, s, pt: (0,) * len(shape))
    page_spec = lambda k: pl.BlockSpec((None, cols, HEAD_DIM), lambda b, s, pt: (pt[b, s * g + k], 0, 0))
    grid_spec = pltpu.PrefetchScalarGridSpec(
        num_scalar_prefetch=1, grid=(bsz, n_pages_total // g),
        in_specs=[per_b((rows, HEAD_DIM)), per_b((rows, HEAD_DIM)),
                  pl.BlockSpec((None, n_tok, g * cols), lambda b, s, pt: (b, 0, s)),
                  per_b((n_tok, keep_new.shape[-1])), const(k_new.shape), const(v_new.shape)]
                 + [page_spec(k) for k in range(g)] * 2,
        out_specs=per_b((rows, HEAD_DIM)),
        scratch_shapes=[pltpu.VMEM((rows, LANES), F32), pltpu.VMEM((rows, LANES), F32),
                        pltpu.VMEM((rows, HEAD_DIM), F32), pltpu.VMEM((rows, g * cols), F32)])
    kern = functools.partial(_sample_attn_kernel, n_pages=g, n_tok=n_tok, page=page)
    return pl.pallas_call(
        kern, grid_spec=grid_spec,
        out_shape=jax.ShapeDtypeStruct((bsz, rows, HEAD_DIM), BF16),
        compiler_params=_cparams(("parallel", "arbitrary")), name="sample_attention",
    )(page_table, q_rows, z_rows, keep, keep_new, k_new, v_new, *([cache_k] * g), *([cache_v] * g))


def _ssd_kernel(*refs, t_rows, q_rows, has_state):
    if has_state:
        (xbc_ref, z_ref, dt_ref, conv0_ref, s0_ref, cw_ref, cb_ref, alog_ref, dskip_ref, ng_ref, e64_ref,
         y_ref, sfin_ref, cnew_ref, xp_ref, act_ref, dtp_ref, acst_ref, acsc_ref, st_ref) = refs
    else:
        (xbc_ref, z_ref, dt_ref, conv0_ref, cw_ref, cb_ref, alog_ref, dskip_ref, ng_ref, e64_ref,
         y_ref, sfin_ref, cnew_ref, xp_ref, act_ref, dtp_ref, acst_ref, acsc_ref, st_ref) = refs
        s0_ref = None
    c = pl.program_id(1)
    q = q_rows
    halo = SUBLANES
    n_heads = dt_ref.shape[-1]
    d_inner = n_heads * SSD_HEAD_DIM
    gw = d_inner // SSD_GROUPS
    hpg = n_heads // SSD_GROUPS
    conv_dim = xbc_ref.shape[-1]

    @pl.when(c == 0)
    def _():
        xp_ref[0:halo, :] = conv0_ref[...]
        if t_rows < q:
            xp_ref[halo:, :] = jnp.zeros((q, conv_dim), F32)
            dtp_ref[...] = jnp.zeros(dtp_ref.shape, F32)
        for g in range(SSD_GROUPS):
            if has_state:
                st_ref[g] = s0_ref[g * hpg:(g + 1) * hpg].reshape(gw, D_STATE).T
            else:
                st_ref[g] = jnp.zeros((D_STATE, gw), F32)

    xp_ref[halo:halo + t_rows, :] = xbc_ref[...]
    dtp_ref[0:t_rows, :] = dt_ref[...]

    cblk = CONV_LANES
    for cbi in range(conv_dim // cblk):
        cs = slice(cbi * cblk, (cbi + 1) * cblk)
        acc = jnp.broadcast_to(cb_ref[:, cs], (q, cblk))
        for tap in range(CONV_W):
            lo = halo - (CONV_W - 1) + tap
            acc = acc + xp_ref[lo:lo + q, cs] * cw_ref[tap:tap + 1, cs]
        act_ref[:, cs] = _silu(acc)

    dt = dtp_ref[...]
    a = -jnp.exp(alog_ref[...])
    ri = lax.broadcasted_iota(I32, (q, q), 0)
    ci = lax.broadcasted_iota(I32, (q, q), 1)
    tril = ri >= ci
    a_cs = _dot_exact_lhs(jnp.where(tril, 1.0, 0.0).astype(BF16), dt * a)
    a_last = a_cs[q - 1:q, :]
    acst_ref[...] = a_cs.T
    for h in range(n_heads):
        acsc_ref[h] = jnp.broadcast_to(a_cs[:, h:h + 1], (q, LANES))
    dt_b = dt.astype(BF16)
    dte_b = jnp.exp(a_last - a_cs).astype(BF16)
    ea_hi, ea_mid, _ = _split3(jnp.exp(a_cs))
    pad = jnp.zeros((SUBLANES - 2, n_heads), F32)
    row_pieces = _split3(jnp.concatenate([jnp.exp(a_last), dskip_ref[...], pad], axis=0))
    mxu = functools.partial(jnp.dot, preferred_element_type=F32)
    lane = lax.broadcasted_iota(I32, (1, LANES), 1)
    lo_half = lane < SSD_HEAD_DIM

    def group_body(g, carry):
        xs = act_ref[:, pl.ds(pl.multiple_of(g * gw, gw), gw)]
        bm = act_ref[:, pl.ds(pl.multiple_of(d_inner + g * D_STATE, D_STATE), D_STATE)]
        cm = act_ref[:, pl.ds(pl.multiple_of(d_inner + SSD_GROUPS * D_STATE + g * D_STATE, D_STATE), D_STATE)]
        e64g = e64_ref[:, pl.ds(pl.multiple_of(g * gw, gw), gw)]
        dt_x, dte_x = mxu(dt_b, e64g), mxu(dte_b, e64g)
        ea_x = mxu(ea_hi, e64g) + mxu(ea_mid, e64g)
        rows_x = mxu(row_pieces[0], e64g) + mxu(row_pieces[1], e64g) + mxu(row_pieces[2], e64g)
        cdec_x, dskip_x = rows_x[0:1], rows_x[1:2]
        xdt = xs * dt_x
        xdt_b = xdt.astype(BF16)
        bm_b, cm_b = bm.astype(BF16), cm.astype(BF16)
        cb = _nt_dot(cm_b, bm_b)
        st = st_ref[g]
        y_off = jnp.dot(cm_b, st.astype(BF16), preferred_element_type=F32) * ea_x
        y_parts = []
        for pr in range(hpg // 2):
            xpair = xdt_b[:, pr * LANES:(pr + 1) * LANES]
            halves = (jnp.where(lo_half, xpair, jnp.zeros_like(xpair)), jnp.where(lo_half, jnp.zeros_like(xpair), xpair))
            yp = jnp.zeros((q, LANES), F32)
            for s in range(2):
                hl = 2 * pr + s
                seg = acsc_ref[g * hpg + hl][:, :q] - acst_ref[pl.ds(g * hpg + hl, 1), :]
                lmat = jnp.where(tril, jnp.exp(jnp.where(tril, seg, 0.0)), 0.0)
                yp = yp + jnp.dot((cb * lmat).astype(BF16), halves[s], preferred_element_type=F32)
            y_parts.append(yp)
        y = jnp.concatenate(y_parts, axis=1) + y_off + dskip_x * xs
        st_ref[g] = cdec_x * st + jnp.dot(bm.T.astype(BF16), (xdt * dte_x).astype(BF16), preferred_element_type=F32)
        gsl = pl.ds(pl.multiple_of(g * gw, gw), gw)
        yz = y[0:t_rows] * z_ref[:, gsl].astype(F32)
        ms = jnp.mean(yz * yz, axis=-1, keepdims=True)
        y_ref[:, gsl] = (yz * lax.rsqrt(ms + EPS) * ng_ref[:, gsl]).astype(y_ref.dtype)
        return carry

    def group_batch(i, carry):
        for u in range(GROUPS_PER_TRIP):
            group_body(GROUPS_PER_TRIP * i + u, carry)
        return carry

    lax.fori_loop(0, SSD_GROUPS // GROUPS_PER_TRIP, group_batch, 0)

    @pl.when(c == pl.num_programs(1) - 1)
    def _():
        cnew_ref[...] = xp_ref[halo + t_rows - (CONV_W - 1):halo + t_rows, :]
        for g in range(SSD_GROUPS):
            sfin_ref[g * hpg:(g + 1) * hpg] = st_ref[g].T.reshape(hpg, SSD_HEAD_DIM, D_STATE)

    if t_rows == q:
        @pl.when(c < pl.num_programs(1) - 1)
        def _():
            xp_ref[0:halo, :] = xp_ref[q:q + halo, :]


def _ssd(xbc, z, dt, conv0, s0, conv_w, conv_b, a_log, d_skip, norm_g, bsz, seq):
    conv_dim = xbc.shape[-1]
    n_heads = dt.shape[-1]
    d_inner = n_heads * SSD_HEAD_DIM
    t_rows = min(CHUNK, seq)
    n_chunks = seq // t_rows
    q_rows = CHUNK if t_rows == CHUNK else _round_up(t_rows, BF16_ROWS)
    has_state = s0 is not None
    hpg = n_heads // SSD_GROUPS
    e64 = (jnp.arange(d_inner)[None, :] // SSD_HEAD_DIM == jnp.arange(n_heads)[:, None]).astype(BF16)
    x3 = lambda a: a.reshape(bsz, seq, a.shape[-1])
    rows = lambda w: pl.BlockSpec((None, t_rows, w), lambda b, c: (b, c, 0))
    const = lambda shape: pl.BlockSpec(shape, lambda b, c: (0,) * len(shape))
    state_spec = pl.BlockSpec((None, n_heads, SSD_HEAD_DIM, D_STATE), lambda b, c: (b, 0, 0, 0))
    args = [x3(xbc), x3(z), x3(dt), conv0]
    in_specs = [rows(conv_dim), rows(d_inner), rows(n_heads),
                pl.BlockSpec((None, SUBLANES, conv_dim), lambda b, c: (b, 0, 0))]
    if has_state:
        args.append(s0)
        in_specs.append(state_spec)
    args += [conv_w, conv_b.reshape(1, conv_dim), a_log.reshape(1, n_heads), d_skip.reshape(1, n_heads),
             norm_g.reshape(1, d_inner), e64]
    in_specs += [const((CONV_W, conv_dim)), const((1, conv_dim)), const((1, n_heads)), const((1, n_heads)),
                 const((1, d_inner)), const(e64.shape)]
    kern = functools.partial(_ssd_kernel, t_rows=t_rows, q_rows=q_rows, has_state=has_state)
    y, sfin, cnew = pl.pallas_call(
        kern, grid=(bsz, n_chunks), in_specs=in_specs,
        out_specs=[rows(d_inner), state_spec,
                   pl.BlockSpec((None, CONV_W - 1, conv_dim), lambda b, c: (b, 0, 0))],
        out_shape=[jax.ShapeDtypeStruct((bsz, seq, d_inner), BF16),
                   jax.ShapeDtypeStruct((bsz, n_heads, SSD_HEAD_DIM, D_STATE), F32),
                   jax.ShapeDtypeStruct((bsz, CONV_W - 1, conv_dim), F32)],
        scratch_shapes=[pltpu.VMEM((SUBLANES + q_rows, conv_dim), F32),
                        pltpu.VMEM((q_rows, conv_dim), F32),
                        pltpu.VMEM((q_rows, n_heads), F32),
                        pltpu.VMEM((n_heads, q_rows), F32),
                        pltpu.VMEM((n_heads, q_rows, LANES), F32),
                        pltpu.VMEM((SSD_GROUPS, D_STATE, hpg * SSD_HEAD_DIM), F32)],
        compiler_params=_cparams(("parallel", "arbitrary")), name="ssd")(*args)
    return y.reshape(bsz * seq, d_inner), sfin, cnew


def _merge_kernel(ya_ref, ys_ref, wa_ref, *rest):
    ws_refs, (ga_ref, gs_ref, o_ref) = rest[:-3], rest[-3:]
    kb = wa_ref.shape[0]
    ya = jnp.dot(ya_ref[...], wa_ref[...], preferred_element_type=F32)
    yb = jnp.dot(ys_ref[:, :kb], ws_refs[0][...], preferred_element_type=F32)
    for i in range(1, len(ws_refs)):
        yb = yb + jnp.dot(ys_ref[:, i * kb:(i + 1) * kb], ws_refs[i][...], preferred_element_type=F32)
    o_ref[...] = (ga_ref[...].astype(F32) * ya + gs_ref[...].astype(F32) * yb).astype(o_ref.dtype)


def _merge(y_att, y_ssd, wb, gates, tm):
    m, att_w = y_att.shape
    d_inner = y_ssd.shape[1]
    d = wb.shape[1]
    tn = MERGE_COLS
    nj = d // tn
    n_ssd = d_inner // att_w
    w_blk = lambda r: pl.BlockSpec((att_w, tn), lambda i, j: (r, j))
    return pl.pallas_call(
        _merge_kernel, grid=(m // tm, nj),
        in_specs=[pl.BlockSpec((tm, att_w), lambda i, j: (i, 0)), pl.BlockSpec((tm, d_inner), lambda i, j: (i, 0))]
                 + [w_blk(r) for r in range(1 + n_ssd)]
                 + [pl.BlockSpec((tm, tn), lambda i, j: (i, j)), pl.BlockSpec((tm, tn), lambda i, j: (i, j + nj))],
        out_specs=pl.BlockSpec((tm, tn), lambda i, j: (i, j)),
        out_shape=jax.ShapeDtypeStruct((m, d), BF16),
        compiler_params=_cparams(("parallel", "parallel")), name="merge",
    )(y_att, y_ssd, *([wb] * (1 + n_ssd)), gates, gates)


def _out_kernel(m_ref, w_ref, x_ref, g_ref, o_ref):
    h = x_ref[...] + jnp.dot(m_ref[...], w_ref[...], preferred_element_type=F32)
    ms = jnp.mean(h * h, axis=-1, keepdims=True)
    o_ref[...] = h * lax.rsqrt(ms + EPS) * g_ref[...]


def _out_proj(merged, w_out, x, final_g, tm):
    m, d = x.shape
    return pl.pallas_call(
        _out_kernel, grid=(m // tm,),
        in_specs=[pl.BlockSpec((tm, d), lambda i: (i, 0)), pl.BlockSpec((d, d), lambda i: (0, 0)),
                  pl.BlockSpec((tm, d), lambda i: (i, 0)), pl.BlockSpec((1, d), lambda i: (0, 0))],
        out_specs=pl.BlockSpec((tm, d), lambda i: (i, 0)),
        out_shape=jax.ShapeDtypeStruct((m, d), F32),
        compiler_params=_cparams(("parallel",)), name="out_proj")(merged, w_out, x, final_g.reshape(1, d))


def _round_up(x, n):
    return (x + n - 1) // n * n


def kernel(x_prompt, x_sample, cache_k, cache_v, cache_idx_k, state_ssm, state_conv, page_table, norm_g, w_in, conv_w,
           conv_b, dt_bias, a_log, d_skip, ssd_norm_g, idx_ln_w, idx_ln_b, w_branch, w_out, final_norm_g):
    assert w_in.shape[0] == 1, "single-layer trunk"
    bp, seq, d = x_prompt.shape
    bs, n_tok, _ = x_sample.shape
    n_phys, page = cache_k.shape[1], cache_k.shape[2]
    past = page_table.shape[1] * page
    att_w, kv_w = ATT_HEADS * HEAD_DIM, KV_HEADS * HEAD_DIM
    d_inner = 2 * d
    conv_dim = d_inner + 2 * SSD_GROUPS * D_STATE
    top_k_p = max(1, min(TOPK_MAX, seq // 4))
    top_k_s = max(1, min(TOPK_MAX, (past + n_tok) // 4))

    wb = w_branch[0].astype(BF16)
    w_out_b = w_out[0].astype(BF16)
    w_in0 = jnp.swapaxes(w_in[0], 0, 1)

    mp = bp * seq
    tm_p = min(PROJ_ROWS, seq)
    xp = x_prompt.reshape(mp, d)
    xn_p = _rmsnorm_bf16(xp, norm_g[0], min(NORM_ROWS, seq))
    pp = _projections(xn_p, w_in0, jnp.arange(seq, dtype=I32), seq // tm_p, tm_p, idx_ln_w[0], idx_ln_b[0], dt_bias[0])
    y_att_p = _prompt_attention(pp, bp, seq, top_k_p)
    y_ssd_p, ssm_p, conv_p = _ssd(pp["xbc"], pp["z_ssd"], pp["dt"], jnp.zeros((bp, SUBLANES, conv_dim), F32), None,
                                  conv_w[0], conv_b[0], a_log[0], d_skip[0], ssd_norm_g[0], bp, seq)
    merged_p = _merge(y_att_p, y_ssd_p, wb, pp["gates"], min(MERGE_ROWS, seq))
    y_p = _out_proj(merged_p, w_out_b, xp, final_norm_g, min(NORM_ROWS, seq))

    ms_rows = bs * n_tok
    rows_pad = _round_up(ms_rows, LANES)
    xs = x_sample.reshape(ms_rows, d)
    xs_pad = jnp.pad(xs, ((0, rows_pad - ms_rows), (0, 0))) if rows_pad != ms_rows else xs
    pos_s = past + (jnp.arange(rows_pad, dtype=I32) % n_tok)
    xn_s = _rmsnorm_bf16(xs_pad, norm_g[0], rows_pad)
    ps = _projections(xn_s, w_in0, pos_s, 1, rows_pad, idx_ln_w[0], idx_ln_b[0], dt_bias[0])
    real = lambda a: a[:ms_rows]

    qi_rows = jnp.transpose(ps["qi"][:, :ms_rows], (1, 0, 2)).reshape(bs, n_tok * IDX_HEADS, IDX_DIM)
    wi_col = real(ps["wi"]).reshape(bs, n_tok * IDX_HEADS, 1)
    scores = _sample_scores(qi_rows, wi_col, jnp.swapaxes(cache_idx_k[0], 1, 2), page_table, n_tok)
    scores = scores.reshape(ms_rows, past)
    if rows_pad != ms_rows:
        scores = jnp.pad(scores, ((0, rows_pad - ms_rows), (0, 0)))
    keep, keep_new = _sample_select(scores, ps["qi"], ps["ki_bf"], ps["wi"], n_tok, ms_rows, top_k_s)
    keep = real(keep).reshape(bs, n_tok, past * KV_HEADS)
    keep_new = real(keep_new).reshape(bs, n_tok, rows_pad * KV_HEADS)
    q_rows = jnp.transpose(ps["q"][:, :ms_rows], (1, 0, 2)).reshape(bs, n_tok * ATT_HEADS, HEAD_DIM)
    z_rows = real(ps["z_att"]).reshape(bs, n_tok * ATT_HEADS, HEAD_DIM)
    k_new, v_new = ps["k"], ps["v"]
    ck = cache_k[0].reshape(n_phys, page * KV_HEADS, HEAD_DIM)
    cv = cache_v[0].reshape(n_phys, page * KV_HEADS, HEAD_DIM)
    y_att_s = _sample_attention(q_rows, z_rows, keep, keep_new, k_new, v_new, ck, cv, page_table, n_tok)
    y_att_s = y_att_s.reshape(ms_rows, att_w)

    conv0_s = jnp.pad(state_conv[0], ((0, 0), (SUBLANES - (CONV_W - 1), 0), (0, 0)))
    y_ssd_s, ssm_s, conv_s = _ssd(real(ps["xbc"]), real(ps["z_ssd"]), real(ps["dt"]), conv0_s, state_ssm[0],
                                  conv_w[0], conv_b[0], a_log[0], d_skip[0], ssd_norm_g[0], bs, n_tok)
    if rows_pad != ms_rows:
        padr = lambda a: jnp.pad(a, ((0, rows_pad - ms_rows), (0, 0)))
        y_att_s, y_ssd_s = padr(y_att_s), padr(y_ssd_s)
    merged_s = _merge(y_att_s, y_ssd_s, wb, ps["gates"], rows_pad)
    y_s = real(_out_proj(merged_s, w_out_b, xs_pad, final_norm_g, rows_pad))

    kv5 = lambda a, b, t: a[:b * t * KV_HEADS].reshape(1, b, t, KV_HEADS, HEAD_DIM)
    return (y_p.reshape(bp, seq, d), y_s.reshape(bs, n_tok, d),
            kv5(pp["k"], bp, seq), kv5(pp["v"], bp, seq), pp["ki"].reshape(1, bp, seq, IDX_DIM),
            ssm_p[None], conv_p[None],
            kv5(ps["k"], bs, n_tok), kv5(ps["v"], bs, n_tok), real(ps["ki"]).reshape(1, bs, n_tok, IDX_DIM),
            ssm_s[None], conv_s[None])
```

```python
import functools

import numpy as np
import jax
import jax.numpy as jnp
from jax import lax
from jax.experimental import pallas as pl
from jax.experimental.pallas import tpu as pltpu

F32, BF16, I32 = jnp.float32, jnp.bfloat16, jnp.int32

ATT_HEADS = 16
KV_HEADS = 4
HEAD_DIM = 128
IDX_HEADS = 16
IDX_DIM = 64
TOPK_MAX = 256
ROPE_THETA = 10000.0
SSD_HEAD_DIM = 64
SSD_GROUPS = 8
D_STATE = 128
CONV_W = 4
CHUNK = 128
EPS = 1e-6

LANES = 128
SUBLANES = 8
BF16_ROWS = 16
VMEM_LIMIT = 56 * 1024 * 1024
INT_MIN = -2 ** 31
KEY_OF_MOST_NEGATIVE_FLOAT = INT_MIN + 0x00800000
MASKED = -1e30
LOG2E = 1.4426950408889634
PAGES_PER_STEP = 32
SCORE_PAGES_PER_STEP = 32
GROUPS_PER_TRIP = 8
IDX_HEADS_PER_DOT = 8
NORM_ROWS = 512
PROJ_ROWS = 1024
PROJ_COLS = 512
PROJ_COLS_WIDE = 1024
ATTN_Q_ROWS = 256
ATTN_SCORE_KEYS = 512
ATTN_SOFTMAX_KEYS = 1024
ATTN_HEADS_PER_STACK = 2
MERGE_ROWS = 1024
MERGE_COLS = 512
CONV_LANES = LANES


def _cparams(sem):
    return pltpu.CompilerParams(dimension_semantics=sem, vmem_limit_bytes=VMEM_LIMIT)


def _nt_dot(a, b):
    return lax.dot_general(a, b, (((1,), (1,)), ((), ())), preferred_element_type=F32)


def _tile_lanes(x, n):
    return x if n == 1 else jnp.concatenate([x] * n, axis=1)


def _split3(x):
    hi = x.astype(BF16)
    r1 = x - hi.astype(F32)
    mid = r1.astype(BF16)
    lo = (r1 - mid.astype(F32)).astype(BF16)
    return hi, mid, lo


def _dot_exact_lhs(e, x):
    hi, mid, lo = _split3(x)
    d = functools.partial(jnp.dot, preferred_element_type=F32)
    return d(e, hi) + d(e, mid) + d(e, lo)


def _norm_kernel(x_ref, g_ref, o_ref):
    x = x_ref[...]
    ms = jnp.mean(x * x, axis=-1, keepdims=True)
    o_ref[...] = (x * lax.rsqrt(ms + EPS) * g_ref[...]).astype(o_ref.dtype)


def _rmsnorm_bf16(x, g, tm):
    m, d = x.shape
    return pl.pallas_call(
        _norm_kernel, grid=(m // tm,),
        in_specs=[pl.BlockSpec((tm, d), lambda i: (i, 0)), pl.BlockSpec((1, d), lambda i: (0, 0))],
        out_specs=pl.BlockSpec((tm, d), lambda i: (i, 0)),
        out_shape=jax.ShapeDtypeStruct((m, d), BF16),
        compiler_params=_cparams(("parallel",)), name="rmsnorm")(x, g.reshape(1, d))


def _mm_kernel(x_ref, w_ref, *rest, epilogue, n_aux):
    acc = _nt_dot(x_ref[...], w_ref[...].astype(BF16))
    epilogue(acc, rest[:n_aux], rest[n_aux:])


def _matmul(x, wt, row0, n, epilogue, aux, aux_specs, out_shapes, out_specs, tm, tn, name):
    m, k = x.shape
    kern = functools.partial(_mm_kernel, epilogue=epilogue, n_aux=len(aux))
    return pl.pallas_call(
        kern, grid=(m // tm, n // tn),
        in_specs=[pl.BlockSpec((tm, k), lambda i, j: (i, 0)),
                  pl.BlockSpec((pl.Element(tn), pl.Element(k)),
                               lambda i, j: (pl.multiple_of(row0 + j * tn, SUBLANES), 0))] + list(aux_specs),
        out_specs=out_specs, out_shape=out_shapes,
        compiler_params=_cparams(("parallel", "parallel")), name=name)(x, wt, *aux)


def _rope_half(x, cos, sin_signed, half):
    if 2 * half == LANES:
        partner = pltpu.roll(x, half, 1)
    else:
        partner = jnp.concatenate([x[:, half:], x[:, :half]], axis=1)
    return x * cos + partner * sin_signed


def _store_heads(outs, c, r, head_major, token_major_first):
    n_heads = outs[0].shape[0] // r.shape[0] if token_major_first else None
    for idx, o in enumerate(outs):
        if token_major_first and idx == 0:
            o[pl.ds(c, r.shape[0], stride=n_heads), :] = r.astype(o.dtype)
        elif head_major:
            o[c] = r.astype(o.dtype)
        else:
            o[:, c * HEAD_DIM:(c + 1) * HEAD_DIM] = r.astype(o.dtype)


def _epi_rope128(acc, aux, outs, post_scale=None, head_major=False, token_major_first=False):
    cos, sin = aux[0][...], aux[1][...]
    for c in range(acc.shape[1] // HEAD_DIM):
        r = _rope_half(acc[:, c * HEAD_DIM:(c + 1) * HEAD_DIM], cos, sin, HEAD_DIM // 2)
        if post_scale is not None:
            r = r * post_scale
        _store_heads(outs, c, r, head_major, token_major_first)


def _epi_heads(acc, aux, outs):
    for c in range(acc.shape[1] // HEAD_DIM):
        _store_heads(outs, c, acc[:, c * HEAD_DIM:(c + 1) * HEAD_DIM], False, True)


def _epi_store(acc, aux, outs, fn=None):
    val = acc if fn is None else fn(acc)
    for o in outs:
        o[...] = val.astype(o.dtype)


def _epi_qi(acc, aux, outs):
    cos, sin = aux[0][...], aux[1][...]
    for hh in range(acc.shape[1] // IDX_DIM):
        x = acc[:, hh * IDX_DIM:(hh + 1) * IDX_DIM]
        outs[0][hh] = _rope_half(x, cos, sin, IDX_DIM // 2).astype(outs[0].dtype)


def _epi_kiwi(acc, aux, outs, idx_scale):
    lnw, lnb, cos, sin = (a[...] for a in aux)
    ki = acc[:, :IDX_DIM]
    mu = jnp.mean(ki, axis=-1, keepdims=True)
    kc = ki - mu
    y = kc * lax.rsqrt(jnp.mean(kc * kc, axis=-1, keepdims=True) + EPS) * lnw + lnb
    r = _rope_half(y, cos, sin, IDX_DIM // 2)
    outs[0][...] = r
    outs[1][...] = r.astype(BF16)
    outs[2][...] = acc[:, IDX_DIM:] * idx_scale


def _epi_softplus(acc, aux, outs):
    x = acc + aux[0][...]
    outs[0][...] = jnp.maximum(x, 0.0) + jnp.log1p(jnp.exp(-jnp.abs(x)))


def _silu(x):
    return x * jax.nn.sigmoid(x)


def _rope_tables(pos, d):
    inv = ROPE_THETA ** (-jnp.arange(0, d, 2, dtype=F32) / d)
    ang = pos.astype(F32)[:, None] * inv[None, :]
    cos, sin = jnp.cos(ang), jnp.sin(ang)
    return jnp.concatenate([cos, cos], axis=-1), jnp.concatenate([-sin, sin], axis=-1)


def _projections(xn, wt, pos_rows, n_pos_blocks, tm, idx_ln_w, idx_ln_b, dt_bias):
    m, d = xn.shape
    att_w, kv_w = ATT_HEADS * HEAD_DIM, KV_HEADS * HEAD_DIM
    d_inner = 2 * d
    conv_dim = d_inner + 2 * SSD_GROUPS * D_STATE
    ssd_heads = d_inner // SSD_HEAD_DIM
    splits = (att_w, kv_w, kv_w, att_w, IDX_HEADS * IDX_DIM, IDX_DIM, IDX_HEADS, d_inner, conv_dim,
              ssd_heads, d, d)
    assert sum(splits) == wt.shape[0]
    off = np.concatenate([[0], np.cumsum(splits)]).astype(int)
    assert all(o % BF16_ROWS == 0 for o in off), "segment rows of the weight must start on packed-row boundaries"
    seg = lambda a, b: (wt, int(off[a]), int(off[b] - off[a]))

    cos128, sin128 = _rope_tables(pos_rows, HEAD_DIM)
    cos64, sin64 = _rope_tables(pos_rows, IDX_DIM)
    pos_map = lambda i, j: (i % n_pos_blocks, 0)
    t128 = [pl.BlockSpec((tm, HEAD_DIM), pos_map)] * 2
    t64 = [pl.BlockSpec((tm, IDX_DIM), pos_map)] * 2
    tile = lambda tn: pl.BlockSpec((tm, tn), lambda i, j: (i, j))
    sds = lambda n, dt: jax.ShapeDtypeStruct((m, n), dt)
    tn, tw = PROJ_COLS, PROJ_COLS_WIDE

    (q,) = _matmul(xn, *seg(0, 1), functools.partial(_epi_rope128, post_scale=HEAD_DIM ** -0.5 * LOG2E, head_major=True),
                   [cos128, sin128], t128, [jax.ShapeDtypeStruct((ATT_HEADS, m, HEAD_DIM), BF16)],
                   [pl.BlockSpec((tw // HEAD_DIM, tm, HEAD_DIM), lambda i, j: (j, i, 0))], tm, tw, "proj_q")
    kv_shapes = [jax.ShapeDtypeStruct((m * KV_HEADS, HEAD_DIM), F32), sds(kv_w, BF16)]
    kv_specs = [pl.BlockSpec((tm * KV_HEADS, HEAD_DIM), lambda i, j: (i, 0)), tile(kv_w)]
    k, k_bf = _matmul(xn, *seg(1, 2), functools.partial(_epi_rope128, token_major_first=True), [cos128, sin128], t128,
                      kv_shapes, kv_specs, tm, kv_w, "proj_k")
    v, v_bf = _matmul(xn, *seg(2, 3), _epi_heads, [], [], kv_shapes, kv_specs, tm, kv_w, "proj_v")
    (z_att,) = _matmul(xn, *seg(3, 4), functools.partial(_epi_store, fn=_silu), [], [],
                       [sds(att_w, BF16)], [tile(tw)], tm, tw, "proj_zatt")
    qi_tn = 4 * IDX_DIM
    (qi,) = _matmul(xn, *seg(4, 5), _epi_qi, [cos64, sin64], t64,
                    [jax.ShapeDtypeStruct((IDX_HEADS, m, IDX_DIM), BF16)],
                    [pl.BlockSpec((4, tm, IDX_DIM), lambda i, j: (j, i, 0))], tm, qi_tn, "proj_qi")
    kiwi_n = IDX_DIM + IDX_HEADS
    idx_scale = IDX_HEADS ** -0.5 * IDX_DIM ** -0.5
    row1 = lambda n: pl.BlockSpec((1, n), lambda i, j: (0, 0))
    ki, ki_bf, wi = _matmul(
        xn, *seg(5, 7), functools.partial(_epi_kiwi, idx_scale=idx_scale),
        [idx_ln_w.reshape(1, IDX_DIM), idx_ln_b.reshape(1, IDX_DIM), cos64, sin64],
        [row1(IDX_DIM), row1(IDX_DIM)] + t64,
        [sds(IDX_DIM, F32), sds(IDX_DIM, BF16), sds(IDX_HEADS, F32)],
        [pl.BlockSpec((tm, IDX_DIM), lambda i, j: (i, 0))] * 2 + [pl.BlockSpec((tm, IDX_HEADS), lambda i, j: (i, 0))],
        tm, kiwi_n, "proj_kiwi")
    (z_ssd,) = _matmul(xn, *seg(7, 8), functools.partial(_epi_store, fn=_silu), [], [],
                       [sds(d_inner, BF16)], [tile(tw)], tm, tw, "proj_zssd")
    (xbc,) = _matmul(xn, *seg(8, 9), _epi_store, [], [], [sds(conv_dim, F32)], [tile(tw)], tm, tw, "proj_xbc")
    (dt,) = _matmul(xn, *seg(9, 10), _epi_softplus, [dt_bias.reshape(1, ssd_heads)], [row1(ssd_heads)],
                    [sds(ssd_heads, F32)], [pl.BlockSpec((tm, ssd_heads), lambda i, j: (i, 0))],
                    tm, ssd_heads, "proj_dt")
    (gates,) = _matmul(xn, *seg(10, 12), functools.partial(_epi_store, fn=jax.nn.sigmoid), [], [],
                       [sds(2 * d, BF16)], [tile(tw)], tm, tw, "proj_gates")
    return dict(q=q, k=k, k_bf=k_bf, v=v, v_bf=v_bf, z_att=z_att, qi=qi, ki=ki, ki_bf=ki_bf, wi=wi,
                z_ssd=z_ssd, xbc=xbc, dt=dt, gates=gates)


def _key_to_float(key):
    return pltpu.bitcast(jnp.where(key < 0, key ^ jnp.int32(0x7FFFFFFF), key), F32)


def _kth_largest(count_ge, shape, top_k):
    kf = jnp.float32(top_k)

    def pick(t, n, cands):
        for cand, c in zip(cands, count_ge([_key_to_float(cand) for cand in cands])):
            c = jnp.broadcast_to(c, shape)
            t, n = jnp.where(c >= kf, cand, t), jnp.where(c >= kf, c, n)
        return t, n

    t, n = pick(jnp.full(shape, INT_MIN, I32), jnp.zeros(shape, F32), [jnp.zeros(shape, I32)])

    def body(i, tn):
        t, n = tn
        return pick(t, n, [t | lax.shift_left(jnp.int32(1), jnp.int32(30) - i)])

    t, n = lax.fori_loop(0, 31, body, (t, n))
    return _key_to_float(jnp.maximum(t, jnp.int32(KEY_OF_MOST_NEGATIVE_FLOAT))), n


def _lane_tile_sum(x):
    part = x[:, :LANES]
    for c in range(1, x.shape[1] // LANES):
        part = part + x[:, c * LANES:(c + 1) * LANES]
    return part


def _attn_kernel(q_ref, qi_ref, wi_ref, z_ref, k_ref, v_ref, ki_ref, o_ref,
                 score_ref, wexp_ref, bias_ref, s_ref, m_ref, l_ref, acc_ref, cand_ref, *, tq, tk, tkc, hps, top_k):
    qt = pl.program_id(1)
    n_kc = ((qt + 1) * tq + tk - 1) // tk
    nl = tk // LANES
    row_pos = qt * tq + lax.broadcasted_iota(I32, (tq, 1), 0)

    wi = wi_ref[...]
    for h in range(IDX_HEADS):
        wexp_ref[h] = jnp.broadcast_to(wi[:, h:h + 1], (tq, LANES))

    def score_chunk(kc, carry):
        off = pl.multiple_of(kc * tk, tk)
        ki = ki_ref[pl.ds(off, tk), :]
        sc = None
        for h0 in range(0, IDX_HEADS, IDX_HEADS_PER_DOT):
            hs = slice(h0, h0 + IDX_HEADS_PER_DOT)
            d = _nt_dot(qi_ref[hs].reshape(IDX_HEADS_PER_DOT * tq, IDX_DIM), ki)
            r = jnp.maximum(d, 0.0) * _tile_lanes(wexp_ref[hs].reshape(IDX_HEADS_PER_DOT * tq, LANES), nl)
            for h in range(IDX_HEADS_PER_DOT):
                sc = r[h * tq:(h + 1) * tq] if sc is None else sc + r[h * tq:(h + 1) * tq]
        kpos = off + lax.broadcasted_iota(I32, (1, tk), 1)
        score_ref[:, pl.ds(off, tk)] = jnp.where(kpos <= row_pos, sc, -jnp.inf)
        return carry

    lax.fori_loop(0, n_kc, score_chunk, 0)

    def count(cands, strict=False):
        for i, c in enumerate(cands):
            cand_ref[i] = c

        def body(kc, cnts):
            sc = score_ref[:, pl.ds(pl.multiple_of(kc * tk, tk), tk)]
            hit = lambda c: jnp.where((sc > c) if strict else (sc >= c), 1.0, 0.0)
            return tuple(cnt + _lane_tile_sum(hit(_tile_lanes(cand_ref[i], nl))) for i, cnt in enumerate(cnts))

        cnts = lax.fori_loop(0, n_kc, body, tuple(jnp.zeros((tq, LANES), F32) for _ in cands))
        return [jnp.sum(cnt, axis=1, keepdims=True) for cnt in cnts]

    thr1, n_ge = _kth_largest(count, (tq, LANES), top_k)
    thr = _tile_lanes(thr1, nl)
    has_ties = jnp.max(n_ge) > jnp.float32(top_k)

    @pl.when(jnp.logical_not(has_ties))
    def _():
        def bias_chunk(kc, carry):
            off = pl.multiple_of(kc * tk, tk)
            bias_ref[:, pl.ds(off, tk)] = jnp.where(score_ref[:, pl.ds(off, tk)] >= thr, 0.0, MASKED)
            return carry

        lax.fori_loop(0, n_kc, bias_chunk, 0)

    @pl.when(has_ties)
    def _():
        room = jnp.float32(top_k) - count([thr1], strict=True)[0]
        ri = lax.broadcasted_iota(I32, (tk, tk), 0)
        ci = lax.broadcasted_iota(I32, (tk, tk), 1)
        prefix = jnp.where(ri <= ci, 1.0, 0.0).astype(BF16)

        def bias_chunk(kc, seen):
            off = pl.multiple_of(kc * tk, tk)
            key = score_ref[:, pl.ds(off, tk)]
            eq = jnp.where(key == thr, 1.0, 0.0)
            rank = seen + jnp.dot(eq.astype(BF16), prefix, preferred_element_type=F32)
            keep = (key > thr) | ((key == thr) & (rank <= room))
            bias_ref[:, pl.ds(off, tk)] = jnp.where(keep, 0.0, MASKED)
            return seen + jnp.sum(eq, axis=1, keepdims=True)

        lax.fori_loop(0, n_kc, bias_chunk, jnp.zeros((tq, 1), F32))

    n_kcc = ((qt + 1) * tq + tkc - 1) // tkc
    nlc = tkc // LANES

    def bias_tail(kc, carry):
        bias_ref[:, pl.ds(pl.multiple_of(kc * tk, tk), tk)] = jnp.full((tq, tk), MASKED, F32)
        return carry

    lax.fori_loop(n_kc, n_kcc * (tkc // tk), bias_tail, 0)
    rep = ATT_HEADS // KV_HEADS

    rows = hps * tq
    n_stacks = ATT_HEADS // hps
    kv_lanes = lambda g: slice((g * hps // rep) * HEAD_DIM, (g * hps // rep + 1) * HEAD_DIM)

    def logits_chunk(g, ks):
        slot = g % 2
        gs = kv_lanes(g)
        qg = q_ref[g * hps:(g + 1) * hps].reshape(rows, HEAD_DIM)
        bias = bias_ref[:, ks]
        s = jnp.concatenate([bias] * hps, axis=0) + _nt_dot(qg, k_ref[ks, gs])
        s_ref[slot, :, ks] = s
        part = s[:, :LANES]
        for c in range(1, nlc):
            part = jnp.maximum(part, s[:, c * LANES:(c + 1) * LANES])
        m_ref[slot] = jnp.maximum(m_ref[slot], part)

    def pv_chunk(g, ks):
        slot = g % 2
        gs = kv_lanes(g)
        p = jnp.exp2(s_ref[slot, :, ks] - _tile_lanes(m_ref[slot], nlc))
        l_ref[...] += _lane_tile_sum(p)
        acc_ref[...] += jnp.dot(p.astype(BF16), v_ref[ks, gs], preferred_element_type=F32)

    for stage in range(n_stacks + 1):
        g_logits = stage if stage < n_stacks else None
        g_pv = stage - 1 if stage > 0 else None
        if g_logits is not None:
            m_ref[g_logits % 2] = jnp.full((rows, LANES), MASKED, F32)
        if g_pv is not None:
            slot = g_pv % 2
            m_ref[slot] = jnp.broadcast_to(jnp.max(m_ref[slot], axis=1, keepdims=True), (rows, LANES))
            l_ref[...] = jnp.zeros((rows, LANES), F32)
            acc_ref[...] = jnp.zeros((rows, HEAD_DIM), F32)

        def stage_chunk(kc, carry, g_logits=g_logits, g_pv=g_pv):
            ks = pl.ds(pl.multiple_of(kc * tkc, tkc), tkc)
            if g_logits is not None:
                logits_chunk(g_logits, ks)
            if g_pv is not None:
                pv_chunk(g_pv, ks)
            return carry

        lax.fori_loop(0, n_kcc, stage_chunk, 0)
        if g_pv is not None:
            o = acc_ref[...] / jnp.sum(l_ref[...], axis=1, keepdims=True)
            for r in range(hps):
                hs = slice((g_pv * hps + r) * HEAD_DIM, (g_pv * hps + r + 1) * HEAD_DIM)
                o_ref[:, hs] = (o[r * tq:(r + 1) * tq] * z_ref[:, hs].astype(F32)).astype(o_ref.dtype)


def _prompt_attention(p, batch, seq, top_k):
    m = batch * seq
    tq = min(ATTN_Q_ROWS, seq)
    tk = min(ATTN_SCORE_KEYS, seq)
    nq = seq // tq
    att_w, kv_w = ATT_HEADS * HEAD_DIM, KV_HEADS * HEAD_DIM
    hps = ATTN_HEADS_PER_STACK
    rows = lambda w: pl.BlockSpec((tq, w), lambda b, t: (b * nq + t, 0))
    whole = lambda w: pl.BlockSpec((seq, w), lambda b, t: (b, 0), pipeline_mode=pl.Buffered(1))
    tkc = min(ATTN_SOFTMAX_KEYS, seq)
    kern = functools.partial(_attn_kernel, tq=tq, tk=tk, tkc=tkc, hps=hps, top_k=top_k)
    return pl.pallas_call(
        kern, grid=(batch, nq),
        in_specs=[pl.BlockSpec((ATT_HEADS, tq, HEAD_DIM), lambda b, t: (0, b * nq + t, 0)),
                  pl.BlockSpec((IDX_HEADS, tq, IDX_DIM), lambda b, t: (0, b * nq + t, 0)),
                  rows(IDX_HEADS), rows(att_w), whole(kv_w), whole(kv_w), whole(IDX_DIM)],
        out_specs=rows(att_w),
        out_shape=jax.ShapeDtypeStruct((m, att_w), BF16),
        scratch_shapes=[pltpu.VMEM((tq, seq), F32),
                        pltpu.VMEM((IDX_HEADS, tq, LANES), F32),
                        pltpu.VMEM((tq, seq), F32),
                        pltpu.VMEM((2, hps * tq, seq), F32),
                        pltpu.VMEM((2, hps * tq, LANES), F32),
                        pltpu.VMEM((hps * tq, LANES), F32),
                        pltpu.VMEM((hps * tq, HEAD_DIM), F32),
                        pltpu.VMEM((1, tq, LANES), F32)],
        compiler_params=_cparams(("parallel", "arbitrary")), name="prompt_attention",
    )(p["q"], p["qi"], p["wi"], p["z_att"], p["k_bf"], p["v_bf"], p["ki_bf"])


def _sample_score_kernel(pt_ref, qi_ref, wi_ref, *rest, n_pages, n_tok):
    page_refs, o_ref = rest[:n_pages], rest[n_pages]
    qi = qi_ref[...]
    w = jnp.broadcast_to(wi_ref[...], (qi.shape[0], LANES))
    for k in range(n_pages):
        kit = page_refs[k][...].astype(BF16)
        page = kit.shape[1]
        r = jnp.maximum(jnp.dot(qi, kit, preferred_element_type=F32), 0.0) * _tile_lanes(w, page // LANES)
        sc = jnp.sum(r.reshape(n_tok, IDX_HEADS, page), axis=1)
        o_ref[:, k * page:(k + 1) * page] = sc


def _sample_scores(qi_rows, wi_col, cache_idx_kt, page_table, n_tok):
    bsz, n_pages_total = page_table.shape
    page = cache_idx_kt.shape[2]
    g = SCORE_PAGES_PER_STEP if n_pages_total % SCORE_PAGES_PER_STEP == 0 else 1
    rows = n_tok * IDX_HEADS
    page_spec = lambda k: pl.BlockSpec((None, IDX_DIM, page), lambda b, s, pt: (pt[b, s * g + k], 0, 0))
    grid_spec = pltpu.PrefetchScalarGridSpec(
        num_scalar_prefetch=1, grid=(bsz, n_pages_total // g),
        in_specs=[pl.BlockSpec((None, rows, IDX_DIM), lambda b, s, pt: (b, 0, 0)),
                  pl.BlockSpec((None, rows, 1), lambda b, s, pt: (b, 0, 0))] + [page_spec(k) for k in range(g)],
        out_specs=pl.BlockSpec((None, n_tok, g * page), lambda b, s, pt: (b, 0, s)))
    kern = functools.partial(_sample_score_kernel, n_pages=g, n_tok=n_tok)
    return pl.pallas_call(
        kern, grid_spec=grid_spec,
        out_shape=jax.ShapeDtypeStruct((bsz, n_tok, n_pages_total * page), F32),
        compiler_params=_cparams(("parallel", "arbitrary")), name="sample_scores",
    )(page_table, qi_rows, wi_col, *([cache_idx_kt] * g))


def _sample_select_kernel(sc_ref, qi_ref, ki_ref, wi_ref, rexp_ref, keep_ref, keep_new_ref, *, n_tok, n_rows, top_k):
    rows, past = sc_ref.shape
    nk = ki_ref.shape[0]
    wi = wi_ref[...]
    ki = ki_ref[...]
    sc_new = jnp.zeros((rows, nk), F32)
    for h in range(IDX_HEADS):
        w = jnp.broadcast_to(wi[:, h:h + 1], (rows, LANES))
        sc_new = sc_new + jnp.maximum(_nt_dot(qi_ref[h], ki), 0.0) * _tile_lanes(w, nk // LANES)
    r = lax.broadcasted_iota(I32, (rows, nk), 0)
    c = lax.broadcasted_iota(I32, (rows, nk), 1)
    ok = (r // n_tok == c // n_tok) & (c % n_tok <= r % n_tok) & (r < n_rows) & (c < n_rows)
    sc_new = jnp.where(ok, sc_new, -jnp.inf)

    def count(cands, strict=False):
        def one(cand):
            hit = lambda sc, n: jnp.where((sc > _tile_lanes(cand, n)) if strict else (sc >= _tile_lanes(cand, n)), 1.0, 0.0)
            return jnp.sum(_lane_tile_sum(hit(sc_ref[...], past // LANES)) + _lane_tile_sum(hit(sc_new, nk // LANES)),
                           axis=1, keepdims=True)

        return [one(cand) for cand in cands]

    thr, _ = _kth_largest(count, (rows, LANES), top_k)
    rexp = rexp_ref[...]
    xw = rexp.shape[1]
    room = jnp.float32(top_k) - count([thr], strict=True)[0]
    ri = lax.broadcasted_iota(I32, (LANES, LANES), 0)
    ci = lax.broadcasted_iota(I32, (LANES, LANES), 1)
    prefix = jnp.where(ri <= ci, 1.0, 0.0).astype(BF16)

    def select(key_tile, seen):
        eq = jnp.where(key_tile == thr, 1.0, 0.0)
        rank = seen + jnp.dot(eq.astype(BF16), prefix, preferred_element_type=F32)
        keep = (key_tile > thr) | ((key_tile == thr) & (rank <= room))
        sel = jnp.where(keep, 1.0, 0.0).astype(BF16)
        return (jnp.dot(sel, rexp, preferred_element_type=F32).astype(BF16),
                seen + jnp.sum(eq, axis=1, keepdims=True))

    def past_tile(ct, seen):
        src = pl.ds(pl.multiple_of(ct * LANES, LANES), LANES)
        kept, seen = select(sc_ref[:, src], seen)
        keep_ref[:, pl.ds(pl.multiple_of(ct * xw, xw), xw)] = kept
        return seen

    seen = lax.fori_loop(0, past // LANES, past_tile, jnp.zeros((rows, 1), F32))
    for ct in range(nk // LANES):
        kept, seen = select(sc_new[:, ct * LANES:(ct + 1) * LANES], seen)
        keep_new_ref[:, ct * xw:(ct + 1) * xw] = kept


def _sample_select(scores, qi, ki_new_bf, wi, n_tok, n_rows, top_k):
    rows, past = scores.shape
    nk = ki_new_bf.shape[0]
    rexp = (jnp.arange(LANES * KV_HEADS)[None, :] // KV_HEADS == jnp.arange(LANES)[:, None]).astype(BF16)
    kern = functools.partial(_sample_select_kernel, n_tok=n_tok, n_rows=n_rows, top_k=top_k)
    full = lambda shape: pl.BlockSpec(shape, lambda i: (0,) * len(shape))
    return pl.pallas_call(
        kern, grid=(1,),
        in_specs=[full((rows, past)), full((IDX_HEADS, rows, IDX_DIM)), full((nk, IDX_DIM)), full((rows, IDX_HEADS)),
                  full(rexp.shape)],
        out_specs=[full((rows, past * KV_HEADS)), full((rows, nk * KV_HEADS))],
        out_shape=[jax.ShapeDtypeStruct((rows, past * KV_HEADS), BF16),
                   jax.ShapeDtypeStruct((rows, nk * KV_HEADS), BF16)],
        compiler_params=_cparams(("arbitrary",)), name="sample_select")(scores, qi, ki_new_bf, wi, rexp)


def _sample_attn_kernel(pt_ref, q_ref, z_ref, keep_ref, keep_new_ref, knew_ref, vnew_ref, *rest, n_pages, n_tok, page):
    k_refs, v_refs = rest[:n_pages], rest[n_pages:2 * n_pages]
    o_ref, m_ref, l_ref, acc_ref, s_ref = rest[2 * n_pages:]
    step = pl.program_id(1)
    rows = n_tok * ATT_HEADS
    cols = page * KV_HEADS
    rep = ATT_HEADS // KV_HEADS
    nl = cols // LANES

    @pl.when(step == 0)
    def _():
        m_ref[...] = jnp.full(m_ref.shape, MASKED, F32)
        l_ref[...] = jnp.zeros(l_ref.shape, F32)
        acc_ref[...] = jnp.zeros(acc_ref.shape, F32)

    q = q_ref[...]
    rr = lax.broadcasted_iota(I32, (rows, cols), 0)
    cc = lax.broadcasted_iota(I32, (rows, cols), 1)
    head_bias = jnp.where((cc % KV_HEADS) == ((rr % ATT_HEADS) // rep), 0.0, MASKED)

    def update(blocks):
        mx = None
        for i, (load_k, _, load_keep) in enumerate(blocks):
            kp = load_k().astype(BF16)
            kb = (load_keep().astype(F32) - 1.0) * (-MASKED)
            kb_rows = jnp.concatenate(
                [jnp.broadcast_to(kb[t:t + 1, :], (ATT_HEADS, cols)) for t in range(n_tok)], axis=0)
            s = _nt_dot(q, kp) + (kb_rows + head_bias)
            s_ref[:, i * cols:(i + 1) * cols] = s
            part = s[:, :LANES]
            for c in range(1, nl):
                part = jnp.maximum(part, s[:, c * LANES:(c + 1) * LANES])
            mx = part if mx is None else jnp.maximum(mx, part)
        m_old = m_ref[...]
        m_new = jnp.maximum(m_old, jnp.max(mx, axis=1, keepdims=True))
        alpha = jnp.exp2(m_old - m_new)
        m_t = _tile_lanes(m_new, nl)
        lsum = jnp.zeros((rows, LANES), F32)
        pv = jnp.zeros((rows, HEAD_DIM), F32)
        for i, (_, load_v, _) in enumerate(blocks):
            p = jnp.exp2(s_ref[:, i * cols:(i + 1) * cols] - m_t)
            lsum = lsum + _lane_tile_sum(p)
            pv = pv + jnp.dot(p.astype(BF16), load_v().astype(BF16), preferred_element_type=F32)
        l_ref[...] = alpha * l_ref[...] + lsum
        acc_ref[...] = alpha * acc_ref[...] + pv
        m_ref[...] = m_new

    update([(lambda k=k: k_refs[k][...], lambda k=k: v_refs[k][...], lambda k=k: keep_ref[:, k * cols:(k + 1) * cols])
            for k in range(n_pages)])

    @pl.when(step == pl.num_programs(1) - 1)
    def _():
        update([(lambda c=c: knew_ref[c * cols:(c + 1) * cols, :], lambda c=c: vnew_ref[c * cols:(c + 1) * cols, :],
                 lambda c=c: keep_new_ref[:, c * cols:(c + 1) * cols]) for c in range(knew_ref.shape[0] // cols)])
        lsum = jnp.sum(l_ref[...], axis=1, keepdims=True)
        o_ref[...] = (acc_ref[...] / lsum * z_ref[...].astype(F32)).astype(o_ref.dtype)


def _sample_attention(q_rows, z_rows, keep, keep_new, k_new, v_new, cache_k, cache_v, page_table, n_tok):
    bsz, n_pages_total = page_table.shape
    cols = cache_k.shape[1]
    page = cols // KV_HEADS
    g = PAGES_PER_STEP if n_pages_total % PAGES_PER_STEP == 0 else 1
    rows = n_tok * ATT_HEADS
    per_b = lambda shape: pl.BlockSpec((None,) + shape, lambda b, s, pt: (b,) + (0,) * len(shape))
    const = lambda shape: pl.BlockSpec(shape, lambda b, s, pt: (0,) * len(shape))
    page_spec = lambda k: pl.BlockSpec((None, cols, HEAD_DIM), lambda b, s, pt: (pt[b, s * g + k], 0, 0))
    grid_spec = pltpu.PrefetchScalarGridSpec(
        num_scalar_prefetch=1, grid=(bsz, n_pages_total // g),
        in_specs=[per_b((rows, HEAD_DIM)), per_b((rows, HEAD_DIM)),
                  pl.BlockSpec((None, n_tok, g * cols), lambda b, s, pt: (b, 0, s)),
                  per_b((n_tok, keep_new.shape[-1])), const(k_new.shape), const(v_new.shape)]
                 + [page_spec(k) for k in range(g)] * 2,
        out_specs=per_b((rows, HEAD_DIM)),
        scratch_shapes=[pltpu.VMEM((rows, LANES), F32), pltpu.VMEM((rows, LANES), F32),
                        pltpu.VMEM((rows, HEAD_DIM), F32), pltpu.VMEM((rows, g * cols), F32)])
    kern = functools.partial(_sample_attn_kernel, n_pages=g, n_tok=n_tok, page=page)
    return pl.pallas_call(
        kern, grid_spec=grid_spec,
        out_shape=jax.ShapeDtypeStruct((bsz, rows, HEAD_DIM), BF16),
        compiler_params=_cparams(("parallel", "arbitrary")), name="sample_attention",
    )(page_table, q_rows, z_rows, keep, keep_new, k_new, v_new, *([cache_k] * g), *([cache_v] * g))


def _ssd_kernel(*refs, t_rows, q_rows, has_state):
    if has_state:
        (xbc_ref, z_ref, dt_ref, conv0_ref, s0_ref, cw_ref, cb_ref, alog_ref, dskip_ref, ng_ref, e64_ref,
         y_ref, sfin_ref, cnew_ref, xp_ref, act_ref, dtp_ref, acst_ref, acsc_ref, st_ref) = refs
    else:
        (xbc_ref, z_ref, dt_ref, conv0_ref, cw_ref, cb_ref, alog_ref, dskip_ref, ng_ref, e64_ref,
         y_ref, sfin_ref, cnew_ref, xp_ref, act_ref, dtp_ref, acst_ref, acsc_ref, st_ref) = refs
        s0_ref = None
    c = pl.program_id(1)
    q = q_rows
    halo = SUBLANES
    n_heads = dt_ref.shape[-1]
    d_inner = n_heads * SSD_HEAD_DIM
    gw = d_inner // SSD_GROUPS
    hpg = n_heads // SSD_GROUPS
    conv_dim = xbc_ref.shape[-1]

    @pl.when(c == 0)
    def _():
        xp_ref[0:halo, :] = conv0_ref[...]
        if t_rows < q:
            xp_ref[halo:, :] = jnp.zeros((q, conv_dim), F32)
            dtp_ref[...] = jnp.zeros(dtp_ref.shape, F32)
        for g in range(SSD_GROUPS):
            if has_state:
                st_ref[g] = s0_ref[g * hpg:(g + 1) * hpg].reshape(gw, D_STATE).T
            else:
                st_ref[g] = jnp.zeros((D_STATE, gw), F32)

    xp_ref[halo:halo + t_rows, :] = xbc_ref[...]
    dtp_ref[0:t_rows, :] = dt_ref[...]

    cblk = CONV_LANES
    for cbi in range(conv_dim // cblk):
        cs = slice(cbi * cblk, (cbi + 1) * cblk)
        acc = jnp.broadcast_to(cb_ref[:, cs], (q, cblk))
        for tap in range(CONV_W):
            lo = halo - (CONV_W - 1) + tap
            acc = acc + xp_ref[lo:lo + q, cs] * cw_ref[tap:tap + 1, cs]
        act_ref[:, cs] = _silu(acc)

    dt = dtp_ref[...]
    a = -jnp.exp(alog_ref[...])
    ri = lax.broadcasted_iota(I32, (q, q), 0)
    ci = lax.broadcasted_iota(I32, (q, q), 1)
    tril = ri >= ci
    a_cs = _dot_exact_lhs(jnp.where(tril, 1.0, 0.0).astype(BF16), dt * a)
    a_last = a_cs[q - 1:q, :]
    acst_ref[...] = a_cs.T
    for h in range(n_heads):
        acsc_ref[h] = jnp.broadcast_to(a_cs[:, h:h + 1], (q, LANES))
    dt_b = dt.astype(BF16)
    dte_b = jnp.exp(a_last - a_cs).astype(BF16)
    ea_hi, ea_mid, _ = _split3(jnp.exp(a_cs))
    pad = jnp.zeros((SUBLANES - 2, n_heads), F32)
    row_pieces = _split3(jnp.concatenate([jnp.exp(a_last), dskip_ref[...], pad], axis=0))
    mxu = functools.partial(jnp.dot, preferred_element_type=F32)
    lane = lax.broadcasted_iota(I32, (1, LANES), 1)
    lo_half = lane < SSD_HEAD_DIM

    def group_body(g, carry):
        xs = act_ref[:, pl.ds(pl.multiple_of(g * gw, gw), gw)]
        bm = act_ref[:, pl.ds(pl.multiple_of(d_inner + g * D_STATE, D_STATE), D_STATE)]
        cm = act_ref[:, pl.ds(pl.multiple_of(d_inner + SSD_GROUPS * D_STATE + g * D_STATE, D_STATE), D_STATE)]
        e64g = e64_ref[:, pl.ds(pl.multiple_of(g * gw, gw), gw)]
        dt_x, dte_x = mxu(dt_b, e64g), mxu(dte_b, e64g)
        ea_x = mxu(ea_hi, e64g) + mxu(ea_mid, e64g)
        rows_x = mxu(row_pieces[0], e64g) + mxu(row_pieces[1], e64g) + mxu(row_pieces[2], e64g)
        cdec_x, dskip_x = rows_x[0:1], rows_x[1:2]
        xdt = xs * dt_x
        xdt_b = xdt.astype(BF16)
        bm_b, cm_b = bm.astype(BF16), cm.astype(BF16)
        cb_causal = jnp.where(tril, _nt_dot(cm_b, bm_b), 0.0)
        st = st_ref[g]
        y_off = jnp.dot(cm_b, st.astype(BF16), preferred_element_type=F32) * ea_x
        y_parts = []
        for pr in range(hpg // 2):
            xpair = xdt_b[:, pr * LANES:(pr + 1) * LANES]
            halves = (jnp.where(lo_half, xpair, jnp.zeros_like(xpair)), jnp.where(lo_half, jnp.zeros_like(xpair), xpair))
            yp = jnp.zeros((q, LANES), F32)
            for s in range(2):
                hl = 2 * pr + s
                seg = acsc_ref[g * hpg + hl][:, :q] - acst_ref[pl.ds(g * hpg + hl, 1), :]
                decay = jnp.exp(jnp.minimum(seg, 0.0))
                yp = yp + jnp.dot((cb_causal * decay).astype(BF16), halves[s], preferred_element_type=F32)
            y_parts.append(yp)
        y = jnp.concatenate(y_parts, axis=1) + y_off + dskip_x * xs
        st_ref[g] = cdec_x * st + jnp.dot(bm.T.astype(BF16), (xdt * dte_x).astype(BF16), preferred_element_type=F32)
        gsl = pl.ds(pl.multiple_of(g * gw, gw), gw)
        yz = y[0:t_rows] * z_ref[:, gsl].astype(F32)
        ms = jnp.mean(yz * yz, axis=-1, keepdims=True)
        y_ref[:, gsl] = (yz * lax.rsqrt(ms + EPS) * ng_ref[:, gsl]).astype(y_ref.dtype)
        return carry

    def group_batch(i, carry):
        for u in range(GROUPS_PER_TRIP):
            group_body(GROUPS_PER_TRIP * i + u, carry)
        return carry

    lax.fori_loop(0, SSD_GROUPS // GROUPS_PER_TRIP, group_batch, 0)

    @pl.when(c == pl.num_programs(1) - 1)
    def _():
        cnew_ref[...] = xp_ref[halo + t_rows - (CONV_W - 1):halo + t_rows, :]
        for g in range(SSD_GROUPS):
            sfin_ref[g * hpg:(g + 1) * hpg] = st_ref[g].T.reshape(hpg, SSD_HEAD_DIM, D_STATE)

    if t_rows == q:
        @pl.when(c < pl.num_programs(1) - 1)
        def _():
            xp_ref[0:halo, :] = xp_ref[q:q + halo, :]


def _ssd(xbc, z, dt, conv0, s0, conv_w, conv_b, a_log, d_skip, norm_g, bsz, seq):
    conv_dim = xbc.shape[-1]
    n_heads = dt.shape[-1]
    d_inner = n_heads * SSD_HEAD_DIM
    t_rows = min(CHUNK, seq)
    n_chunks = seq // t_rows
    q_rows = CHUNK if t_rows == CHUNK else _round_up(t_rows, BF16_ROWS)
    has_state = s0 is not None
    hpg = n_heads // SSD_GROUPS
    e64 = (jnp.arange(d_inner)[None, :] // SSD_HEAD_DIM == jnp.arange(n_heads)[:, None]).astype(BF16)
    x3 = lambda a: a.reshape(bsz, seq, a.shape[-1])
    rows = lambda w: pl.BlockSpec((None, t_rows, w), lambda b, c: (b, c, 0))
    const = lambda shape: pl.BlockSpec(shape, lambda b, c: (0,) * len(shape))
    state_spec = pl.BlockSpec((None, n_heads, SSD_HEAD_DIM, D_STATE), lambda b, c: (b, 0, 0, 0))
    args = [x3(xbc), x3(z), x3(dt), conv0]
    in_specs = [rows(conv_dim), rows(d_inner), rows(n_heads),
                pl.BlockSpec((None, SUBLANES, conv_dim), lambda b, c: (b, 0, 0))]
    if has_state:
        args.append(s0)
        in_specs.append(state_spec)
    args += [conv_w, conv_b.reshape(1, conv_dim), a_log.reshape(1, n_heads), d_skip.reshape(1, n_heads),
             norm_g.reshape(1, d_inner), e64]
    in_specs += [const((CONV_W, conv_dim)), const((1, conv_dim)), const((1, n_heads)), const((1, n_heads)),
                 const((1, d_inner)), const(e64.shape)]
    kern = functools.partial(_ssd_kernel, t_rows=t_rows, q_rows=q_rows, has_state=has_state)
    y, sfin, cnew = pl.pallas_call(
        kern, grid=(bsz, n_chunks), in_specs=in_specs,
        out_specs=[rows(d_inner), state_spec,
                   pl.BlockSpec((None, CONV_W - 1, conv_dim), lambda b, c: (b, 0, 0))],
        out_shape=[jax.ShapeDtypeStruct((bsz, seq, d_inner), BF16),
                   jax.ShapeDtypeStruct((bsz, n_heads, SSD_HEAD_DIM, D_STATE), F32),
                   jax.ShapeDtypeStruct((bsz, CONV_W - 1, conv_dim), F32)],
        scratch_shapes=[pltpu.VMEM((SUBLANES + q_rows, conv_dim), F32),
                        pltpu.VMEM((q_rows, conv_dim), F32),
                        pltpu.VMEM((q_rows, n_heads), F32),
                        pltpu.VMEM((n_heads, q_rows), F32),
                        pltpu.VMEM((n_heads, q_rows, LANES), F32),
                        pltpu.VMEM((SSD_GROUPS, D_STATE, hpg * SSD_HEAD_DIM), F32)],
        compiler_params=_cparams(("parallel", "arbitrary")), name="ssd")(*args)
    return y.reshape(bsz * seq, d_inner), sfin, cnew


def _merge_kernel(ya_ref, ys_ref, wa_ref, *rest):
    ws_refs, (ga_ref, gs_ref, o_ref) = rest[:-3], rest[-3:]
    kb = wa_ref.shape[0]
    ya = jnp.dot(ya_ref[...], wa_ref[...], preferred_element_type=F32)
    yb = jnp.dot(ys_ref[:, :kb], ws_refs[0][...], preferred_element_type=F32)
    for i in range(1, len(ws_refs)):
        yb = yb + jnp.dot(ys_ref[:, i * kb:(i + 1) * kb], ws_refs[i][...], preferred_element_type=F32)
    o_ref[...] = (ga_ref[...].astype(F32) * ya + gs_ref[...].astype(F32) * yb).astype(o_ref.dtype)


def _merge(y_att, y_ssd, wb, gates, tm):
    m, att_w = y_att.shape
    d_inner = y_ssd.shape[1]
    d = wb.shape[1]
    tn = MERGE_COLS
    nj = d // tn
    n_ssd = d_inner // att_w
    w_blk = lambda r: pl.BlockSpec((att_w, tn), lambda i, j: (r, j))
    return pl.pallas_call(
        _merge_kernel, grid=(m // tm, nj),
        in_specs=[pl.BlockSpec((tm, att_w), lambda i, j: (i, 0)), pl.BlockSpec((tm, d_inner), lambda i, j: (i, 0))]
                 + [w_blk(r) for r in range(1 + n_ssd)]
                 + [pl.BlockSpec((tm, tn), lambda i, j: (i, j)), pl.BlockSpec((tm, tn), lambda i, j: (i, j + nj))],
        out_specs=pl.BlockSpec((tm, tn), lambda i, j: (i, j)),
        out_shape=jax.ShapeDtypeStruct((m, d), BF16),
        compiler_params=_cparams(("parallel", "parallel")), name="merge",
    )(y_att, y_ssd, *([wb] * (1 + n_ssd)), gates, gates)


def _out_kernel(m_ref, w_ref, x_ref, g_ref, o_ref):
    h = x_ref[...] + jnp.dot(m_ref[...], w_ref[...], preferred_element_type=F32)
    ms = jnp.mean(h * h, axis=-1, keepdims=True)
    o_ref[...] = h * lax.rsqrt(ms + EPS) * g_ref[...]


def _out_proj(merged, w_out, x, final_g, tm):
    m, d = x.shape
    return pl.pallas_call(
        _out_kernel, grid=(m // tm,),
        in_specs=[pl.BlockSpec((tm, d), lambda i: (i, 0)), pl.BlockSpec((d, d), lambda i: (0, 0)),
                  pl.BlockSpec((tm, d), lambda i: (i, 0)), pl.BlockSpec((1, d), lambda i: (0, 0))],
        out_specs=pl.BlockSpec((tm, d), lambda i: (i, 0)),
        out_shape=jax.ShapeDtypeStruct((m, d), F32),
        compiler_params=_cparams(("parallel",)), name="out_proj")(merged, w_out, x, final_g.reshape(1, d))


def _round_up(x, n):
    return (x + n - 1) // n * n


def kernel(x_prompt, x_sample, cache_k, cache_v, cache_idx_k, state_ssm, state_conv, page_table, norm_g, w_in, conv_w,
           conv_b, dt_bias, a_log, d_skip, ssd_norm_g, idx_ln_w, idx_ln_b, w_branch, w_out, final_norm_g):
    assert w_in.shape[0] == 1, "single-layer trunk"
    bp, seq, d = x_prompt.shape
    bs, n_tok, _ = x_sample.shape
    n_phys, page = cache_k.shape[1], cache_k.shape[2]
    past = page_table.shape[1] * page
    att_w, kv_w = ATT_HEADS * HEAD_DIM, KV_HEADS * HEAD_DIM
    d_inner = 2 * d
    conv_dim = d_inner + 2 * SSD_GROUPS * D_STATE
    top_k_p = max(1, min(TOPK_MAX, seq // 4))
    top_k_s = max(1, min(TOPK_MAX, (past + n_tok) // 4))

    wb = w_branch[0].astype(BF16)
    w_out_b = w_out[0].astype(BF16)
    w_in0 = jnp.swapaxes(w_in[0], 0, 1)

    mp = bp * seq
    tm_p = min(PROJ_ROWS, seq)
    xp = x_prompt.reshape(mp, d)
    xn_p = _rmsnorm_bf16(xp, norm_g[0], min(NORM_ROWS, seq))
    pp = _projections(xn_p, w_in0, jnp.arange(seq, dtype=I32), seq // tm_p, tm_p, idx_ln_w[0], idx_ln_b[0], dt_bias[0])
    y_att_p = _prompt_attention(pp, bp, seq, top_k_p)
    y_ssd_p, ssm_p, conv_p = _ssd(pp["xbc"], pp["z_ssd"], pp["dt"], jnp.zeros((bp, SUBLANES, conv_dim), F32), None,
                                  conv_w[0], conv_b[0], a_log[0], d_skip[0], ssd_norm_g[0], bp, seq)
    merged_p = _merge(y_att_p, y_ssd_p, wb, pp["gates"], min(MERGE_ROWS, seq))
    y_p = _out_proj(merged_p, w_out_b, xp, final_norm_g, min(NORM_ROWS, seq))

    ms_rows = bs * n_tok
    rows_pad = _round_up(ms_rows, LANES)
    xs = x_sample.reshape(ms_rows, d)
    xs_pad = jnp.pad(xs, ((0, rows_pad - ms_rows), (0, 0))) if rows_pad != ms_rows else xs
    pos_s = past + (jnp.arange(rows_pad, dtype=I32) % n_tok)
    xn_s = _rmsnorm_bf16(xs_pad, norm_g[0], rows_pad)
    ps = _projections(xn_s, w_in0, pos_s, 1, rows_pad, idx_ln_w[0], idx_ln_b[0], dt_bias[0])
    real = lambda a: a[:ms_rows]

    qi_rows = jnp.transpose(ps["qi"][:, :ms_rows], (1, 0, 2)).reshape(bs, n_tok * IDX_HEADS, IDX_DIM)
    wi_col = real(ps["wi"]).reshape(bs, n_tok * IDX_HEADS, 1)
    scores = _sample_scores(qi_rows, wi_col, jnp.swapaxes(cache_idx_k[0], 1, 2), page_table, n_tok)
    scores = scores.reshape(ms_rows, past)
    if rows_pad != ms_rows:
        scores = jnp.pad(scores, ((0, rows_pad - ms_rows), (0, 0)))
    keep, keep_new = _sample_select(scores, ps["qi"], ps["ki_bf"], ps["wi"], n_tok, ms_rows, top_k_s)
    keep = real(keep).reshape(bs, n_tok, past * KV_HEADS)
    keep_new = real(keep_new).reshape(bs, n_tok, rows_pad * KV_HEADS)
    q_rows = jnp.transpose(ps["q"][:, :ms_rows], (1, 0, 2)).reshape(bs, n_tok * ATT_HEADS, HEAD_DIM)
    z_rows = real(ps["z_att"]).reshape(bs, n_tok * ATT_HEADS, HEAD_DIM)
    k_new, v_new = ps["k"], ps["v"]
    ck = cache_k[0].reshape(n_phys, page * KV_HEADS, HEAD_DIM)
    cv = cache_v[0].reshape(n_phys, page * KV_HEADS, HEAD_DIM)
    y_att_s = _sample_attention(q_rows, z_rows, keep, keep_new, k_new, v_new, ck, cv, page_table, n_tok)
    y_att_s = y_att_s.reshape(ms_rows, att_w)

    conv0_s = jnp.pad(state_conv[0], ((0, 0), (SUBLANES - (CONV_W - 1), 0), (0, 0)))
    y_ssd_s, ssm_s, conv_s = _ssd(real(ps["xbc"]), real(ps["z_ssd"]), real(ps["dt"]), conv0_s, state_ssm[0],
                                  conv_w[0], conv_b[0], a_log[0], d_skip[0], ssd_norm_g[0], bs, n_tok)
    if rows_pad != ms_rows:
        padr = lambda a: jnp.pad(a, ((0, rows_pad - ms_rows), (0, 0)))
        y_att_s, y_ssd_s = padr(y_att_s), padr(y_ssd_s)
    merged_s = _merge(y_att_s, y_ssd_s, wb, ps["gates"], rows_pad)
    y_s = real(_out_proj(merged_s, w_out_b, xs_pad, final_norm_g, rows_pad))

    kv5 = lambda a, b, t: a[:b * t * KV_HEADS].reshape(1, b, t, KV_HEADS, HEAD_DIM)
    return (y_p.reshape(bp, seq, d), y_s.reshape(bs, n_tok, d),
            kv5(pp["k"], bp, seq), kv5(pp["v"], bp, seq), pp["ki"].reshape(1, bp, seq, IDX_DIM),
            ssm_p[None], conv_p[None],
            kv5(ps["k"], bs, n_tok), kv5(ps["v"], bs, n_tok), real(ps["ki"]).reshape(1, bs, n_tok, IDX_DIM),
            ssm_s[None], conv_s[None])
```

```python
import functools

import numpy as np
import jax
import jax.numpy as jnp
from jax import lax
from jax.experimental import pallas as pl
from jax.experimental.pallas import tpu as pltpu

F32, BF16, I32 = jnp.float32, jnp.bfloat16, jnp.int32

ATT_HEADS = 16
KV_HEADS = 4
HEAD_DIM = 128
IDX_HEADS = 16
IDX_DIM = 64
TOPK_MAX = 256
ROPE_THETA = 10000.0
SSD_HEAD_DIM = 64
SSD_GROUPS = 8
D_STATE = 128
CONV_W = 4
CHUNK = 128
EPS = 1e-6

LANES = 128
SUBLANES = 8
BF16_ROWS = 16
VMEM_LIMIT = 56 * 1024 * 1024
INT_MIN = -2 ** 31
KEY_OF_MOST_NEGATIVE_FLOAT = INT_MIN + 0x00800000
MASKED = -1e30
LOG2E = 1.4426950408889634
PAGES_PER_STEP = 32
SCORE_PAGES_PER_STEP = 32
GROUPS_PER_TRIP = 8
IDX_HEADS_PER_DOT = 8
NORM_ROWS = 512
PROJ_ROWS = 1024
PROJ_COLS = 512
PROJ_COLS_WIDE = 1024
ATTN_Q_ROWS = 256
ATTN_SCORE_KEYS = 512
ATTN_SOFTMAX_KEYS = 1024
ATTN_HEADS_PER_STACK = 2
MERGE_ROWS = 1024
MERGE_COLS = 512
CONV_LANES = LANES


def _cparams(sem):
    return pltpu.CompilerParams(dimension_semantics=sem, vmem_limit_bytes=VMEM_LIMIT)


def _nt_dot(a, b):
    return lax.dot_general(a, b, (((1,), (1,)), ((), ())), preferred_element_type=F32)


def _tile_lanes(x, n):
    return x if n == 1 else jnp.concatenate([x] * n, axis=1)


def _split3(x):
    hi = x.astype(BF16)
    r1 = x - hi.astype(F32)
    mid = r1.astype(BF16)
    lo = (r1 - mid.astype(F32)).astype(BF16)
    return hi, mid, lo


def _dot_exact_lhs(e, x):
    hi, mid, lo = _split3(x)
    d = functools.partial(jnp.dot, preferred_element_type=F32)
    return d(e, hi) + d(e, mid) + d(e, lo)


def _norm_kernel(x_ref, g_ref, o_ref):
    x = x_ref[...]
    ms = jnp.mean(x * x, axis=-1, keepdims=True)
    o_ref[...] = (x * lax.rsqrt(ms + EPS) * g_ref[...]).astype(o_ref.dtype)


def _rmsnorm_bf16(x, g, tm):
    m, d = x.shape
    return pl.pallas_call(
        _norm_kernel, grid=(m // tm,),
        in_specs=[pl.BlockSpec((tm, d), lambda i: (i, 0)), pl.BlockSpec((1, d), lambda i: (0, 0))],
        out_specs=pl.BlockSpec((tm, d), lambda i: (i, 0)),
        out_shape=jax.ShapeDtypeStruct((m, d), BF16),
        compiler_params=_cparams(("parallel",)), name="rmsnorm")(x, g.reshape(1, d))


def _mm_kernel(x_ref, w_ref, *rest, epilogue, n_aux):
    acc = _nt_dot(x_ref[...], w_ref[...].astype(BF16))
    epilogue(acc, rest[:n_aux], rest[n_aux:])


def _matmul(x, wt, row0, n, epilogue, aux, aux_specs, out_shapes, out_specs, tm, tn, name):
    m, k = x.shape
    kern = functools.partial(_mm_kernel, epilogue=epilogue, n_aux=len(aux))
    return pl.pallas_call(
        kern, grid=(m // tm, n // tn),
        in_specs=[pl.BlockSpec((tm, k), lambda i, j: (i, 0)),
                  pl.BlockSpec((pl.Element(tn), pl.Element(k)),
                               lambda i, j: (pl.multiple_of(row0 + j * tn, SUBLANES), 0))] + list(aux_specs),
        out_specs=out_specs, out_shape=out_shapes,
        compiler_params=_cparams(("parallel", "parallel")), name=name)(x, wt, *aux)


def _rope_half(x, cos, sin_signed, half):
    if 2 * half == LANES:
        partner = pltpu.roll(x, half, 1)
    else:
        partner = jnp.concatenate([x[:, half:], x[:, :half]], axis=1)
    return x * cos + partner * sin_signed


def _store_heads(outs, c, r, head_major, token_major_first):
    n_heads = outs[0].shape[0] // r.shape[0] if token_major_first else None
    for idx, o in enumerate(outs):
        if token_major_first and idx == 0:
            o[pl.ds(c, r.shape[0], stride=n_heads), :] = r.astype(o.dtype)
        elif head_major:
            o[c] = r.astype(o.dtype)
        else:
            o[:, c * HEAD_DIM:(c + 1) * HEAD_DIM] = r.astype(o.dtype)


def _epi_rope128(acc, aux, outs, post_scale=None, head_major=False, token_major_first=False):
    cos, sin = aux[0][...], aux[1][...]
    for c in range(acc.shape[1] // HEAD_DIM):
        r = _rope_half(acc[:, c * HEAD_DIM:(c + 1) * HEAD_DIM], cos, sin, HEAD_DIM // 2)
        if post_scale is not None:
            r = r * post_scale
        _store_heads(outs, c, r, head_major, token_major_first)


def _epi_heads(acc, aux, outs):
    for c in range(acc.shape[1] // HEAD_DIM):
        _store_heads(outs, c, acc[:, c * HEAD_DIM:(c + 1) * HEAD_DIM], False, True)


def _epi_store(acc, aux, outs, fn=None):
    val = acc if fn is None else fn(acc)
    for o in outs:
        o[...] = val.astype(o.dtype)


def _epi_qi(acc, aux, outs):
    cos, sin = aux[0][...], aux[1][...]
    for hh in range(acc.shape[1] // IDX_DIM):
        x = acc[:, hh * IDX_DIM:(hh + 1) * IDX_DIM]
        outs[0][hh] = _rope_half(x, cos, sin, IDX_DIM // 2).astype(outs[0].dtype)


def _epi_kiwi(acc, aux, outs, idx_scale):
    lnw, lnb, cos, sin = (a[...] for a in aux)
    ki = acc[:, :IDX_DIM]
    mu = jnp.mean(ki, axis=-1, keepdims=True)
    kc = ki - mu
    y = kc * lax.rsqrt(jnp.mean(kc * kc, axis=-1, keepdims=True) + EPS) * lnw + lnb
    r = _rope_half(y, cos, sin, IDX_DIM // 2)
    outs[0][...] = r
    outs[1][...] = r.astype(BF16)
    outs[2][...] = acc[:, IDX_DIM:] * idx_scale


def _epi_softplus(acc, aux, outs):
    x = acc + aux[0][...]
    outs[0][...] = jnp.maximum(x, 0.0) + jnp.log1p(jnp.exp(-jnp.abs(x)))


def _silu(x):
    return x * jax.nn.sigmoid(x)


def _rope_tables(pos, d):
    inv = ROPE_THETA ** (-jnp.arange(0, d, 2, dtype=F32) / d)
    ang = pos.astype(F32)[:, None] * inv[None, :]
    cos, sin = jnp.cos(ang), jnp.sin(ang)
    return jnp.concatenate([cos, cos], axis=-1), jnp.concatenate([-sin, sin], axis=-1)


def _projections(xn, wt, pos_rows, n_pos_blocks, tm, idx_ln_w, idx_ln_b, dt_bias):
    m, d = xn.shape
    att_w, kv_w = ATT_HEADS * HEAD_DIM, KV_HEADS * HEAD_DIM
    d_inner = 2 * d
    conv_dim = d_inner + 2 * SSD_GROUPS * D_STATE
    ssd_heads = d_inner // SSD_HEAD_DIM
    splits = (att_w, kv_w, kv_w, att_w, IDX_HEADS * IDX_DIM, IDX_DIM, IDX_HEADS, d_inner, conv_dim,
              ssd_heads, d, d)
    assert sum(splits) == wt.shape[0]
    off = np.concatenate([[0], np.cumsum(splits)]).astype(int)
    assert all(o % BF16_ROWS == 0 for o in off), "segment rows of the weight must start on packed-row boundaries"
    seg = lambda a, b: (wt, int(off[a]), int(off[b] - off[a]))

    cos128, sin128 = _rope_tables(pos_rows, HEAD_DIM)
    cos64, sin64 = _rope_tables(pos_rows, IDX_DIM)
    pos_map = lambda i, j: (i % n_pos_blocks, 0)
    t128 = [pl.BlockSpec((tm, HEAD_DIM), pos_map)] * 2
    t64 = [pl.BlockSpec((tm, IDX_DIM), pos_map)] * 2
    tile = lambda tn: pl.BlockSpec((tm, tn), lambda i, j: (i, j))
    sds = lambda n, dt: jax.ShapeDtypeStruct((m, n), dt)
    tn, tw = PROJ_COLS, PROJ_COLS_WIDE

    (q,) = _matmul(xn, *seg(0, 1), functools.partial(_epi_rope128, post_scale=HEAD_DIM ** -0.5 * LOG2E, head_major=True),
                   [cos128, sin128], t128, [jax.ShapeDtypeStruct((ATT_HEADS, m, HEAD_DIM), BF16)],
                   [pl.BlockSpec((tw // HEAD_DIM, tm, HEAD_DIM), lambda i, j: (j, i, 0))], tm, tw, "proj_q")
    kv_shapes = [jax.ShapeDtypeStruct((m * KV_HEADS, HEAD_DIM), F32), sds(kv_w, BF16)]
    kv_specs = [pl.BlockSpec((tm * KV_HEADS, HEAD_DIM), lambda i, j: (i, 0)), tile(kv_w)]
    k, k_bf = _matmul(xn, *seg(1, 2), functools.partial(_epi_rope128, token_major_first=True), [cos128, sin128], t128,
                      kv_shapes, kv_specs, tm, kv_w, "proj_k")
    v, v_bf = _matmul(xn, *seg(2, 3), _epi_heads, [], [], kv_shapes, kv_specs, tm, kv_w, "proj_v")
    (z_att,) = _matmul(xn, *seg(3, 4), functools.partial(_epi_store, fn=_silu), [], [],
                       [sds(att_w, BF16)], [tile(tw)], tm, tw, "proj_zatt")
    qi_tn = 4 * IDX_DIM
    (qi,) = _matmul(xn, *seg(4, 5), _epi_qi, [cos64, sin64], t64,
                    [jax.ShapeDtypeStruct((IDX_HEADS, m, IDX_DIM), BF16)],
                    [pl.BlockSpec((4, tm, IDX_DIM), lambda i, j: (j, i, 0))], tm, qi_tn, "proj_qi")
    kiwi_n = IDX_DIM + IDX_HEADS
    idx_scale = IDX_HEADS ** -0.5 * IDX_DIM ** -0.5
    row1 = lambda n: pl.BlockSpec((1, n), lambda i, j: (0, 0))
    ki, ki_bf, wi = _matmul(
        xn, *seg(5, 7), functools.partial(_epi_kiwi, idx_scale=idx_scale),
        [idx_ln_w.reshape(1, IDX_DIM), idx_ln_b.reshape(1, IDX_DIM), cos64, sin64],
        [row1(IDX_DIM), row1(IDX_DIM)] + t64,
        [sds(IDX_DIM, F32), sds(IDX_DIM, BF16), sds(IDX_HEADS, F32)],
        [pl.BlockSpec((tm, IDX_DIM), lambda i, j: (i, 0))] * 2 + [pl.BlockSpec((tm, IDX_HEADS), lambda i, j: (i, 0))],
        tm, kiwi_n, "proj_kiwi")
    (z_ssd,) = _matmul(xn, *seg(7, 8), functools.partial(_epi_store, fn=_silu), [], [],
                       [sds(d_inner, BF16)], [tile(tw)], tm, tw, "proj_zssd")
    (xbc,) = _matmul(xn, *seg(8, 9), _epi_store, [], [], [sds(conv_dim, F32)], [tile(tw)], tm, tw, "proj_xbc")
    (dt,) = _matmul(xn, *seg(9, 10), _epi_softplus, [dt_bias.reshape(1, ssd_heads)], [row1(ssd_heads)],
                    [sds(ssd_heads, F32)], [pl.BlockSpec((tm, ssd_heads), lambda i, j: (i, 0))],
                    tm, ssd_heads, "proj_dt")
    (gates,) = _matmul(xn, *seg(10, 12), functools.partial(_epi_store, fn=jax.nn.sigmoid), [], [],
                       [sds(2 * d, BF16)], [tile(tw)], tm, tw, "proj_gates")
    return dict(q=q, k=k, k_bf=k_bf, v=v, v_bf=v_bf, z_att=z_att, qi=qi, ki=ki, ki_bf=ki_bf, wi=wi,
                z_ssd=z_ssd, xbc=xbc, dt=dt, gates=gates)


def _key_to_float(key):
    return pltpu.bitcast(jnp.where(key < 0, key ^ jnp.int32(0x7FFFFFFF), key), F32)


def _kth_largest(count_ge, shape, top_k):
    kf = jnp.float32(top_k)

    def pick(t, n, cands):
        for cand, c in zip(cands, count_ge([_key_to_float(cand) for cand in cands])):
            c = jnp.broadcast_to(c, shape)
            t, n = jnp.where(c >= kf, cand, t), jnp.where(c >= kf, c, n)
        return t, n

    t, n = pick(jnp.full(shape, INT_MIN, I32), jnp.zeros(shape, F32), [jnp.zeros(shape, I32)])

    def body(i, tn):
        t, n = tn
        return pick(t, n, [t | lax.shift_left(jnp.int32(1), jnp.int32(30) - i)])

    t, n = lax.fori_loop(0, 31, body, (t, n))
    return _key_to_float(jnp.maximum(t, jnp.int32(KEY_OF_MOST_NEGATIVE_FLOAT))), n


def _lane_tile_sum(x):
    part = x[:, :LANES]
    for c in range(1, x.shape[1] // LANES):
        part = part + x[:, c * LANES:(c + 1) * LANES]
    return part


def _attn_kernel(q_ref, qi_ref, wi_ref, z_ref, k_ref, v_ref, ki_ref, o_ref,
                 score_ref, wexp_ref, bias_ref, s_ref, m_ref, l_ref, acc_ref, cand_ref, *, tq, tk, tkc, hps, top_k):
    qt = pl.program_id(1)
    n_kc = ((qt + 1) * tq + tk - 1) // tk
    nl = tk // LANES
    row_pos = qt * tq + lax.broadcasted_iota(I32, (tq, 1), 0)

    wi = wi_ref[...]
    for h in range(IDX_HEADS):
        wexp_ref[h] = jnp.broadcast_to(wi[:, h:h + 1], (tq, LANES))

    def score_chunk(kc, carry):
        off = pl.multiple_of(kc * tk, tk)
        ki = ki_ref[pl.ds(off, tk), :]
        sc = None
        for h0 in range(0, IDX_HEADS, IDX_HEADS_PER_DOT):
            hs = slice(h0, h0 + IDX_HEADS_PER_DOT)
            d = _nt_dot(qi_ref[hs].reshape(IDX_HEADS_PER_DOT * tq, IDX_DIM), ki)
            r = jnp.maximum(d, 0.0) * _tile_lanes(wexp_ref[hs].reshape(IDX_HEADS_PER_DOT * tq, LANES), nl)
            for h in range(IDX_HEADS_PER_DOT):
                sc = r[h * tq:(h + 1) * tq] if sc is None else sc + r[h * tq:(h + 1) * tq]
        kpos = off + lax.broadcasted_iota(I32, (1, tk), 1)
        score_ref[:, pl.ds(off, tk)] = jnp.where(kpos <= row_pos, sc, -jnp.inf)
        return carry

    lax.fori_loop(0, n_kc, score_chunk, 0)

    def count(cands, strict=False):
        for i, c in enumerate(cands):
            cand_ref[i] = c

        def body(kc, cnts):
            sc = score_ref[:, pl.ds(pl.multiple_of(kc * tk, tk), tk)]
            hit = lambda c: jnp.where((sc > c) if strict else (sc >= c), 1.0, 0.0)
            return tuple(cnt + _lane_tile_sum(hit(_tile_lanes(cand_ref[i], nl))) for i, cnt in enumerate(cnts))

        cnts = lax.fori_loop(0, n_kc, body, tuple(jnp.zeros((tq, LANES), F32) for _ in cands))
        return [jnp.sum(cnt, axis=1, keepdims=True) for cnt in cnts]

    thr1, n_ge = _kth_largest(count, (tq, LANES), top_k)
    thr = _tile_lanes(thr1, nl)
    has_ties = jnp.max(n_ge) > jnp.float32(top_k)

    @pl.when(jnp.logical_not(has_ties))
    def _():
        def bias_chunk(kc, carry):
            off = pl.multiple_of(kc * tk, tk)
            bias_ref[:, pl.ds(off, tk)] = jnp.where(score_ref[:, pl.ds(off, tk)] >= thr, 0.0, MASKED)
            return carry

        lax.fori_loop(0, n_kc, bias_chunk, 0)

    @pl.when(has_ties)
    def _():
        room = jnp.float32(top_k) - count([thr1], strict=True)[0]
        ri = lax.broadcasted_iota(I32, (tk, tk), 0)
        ci = lax.broadcasted_iota(I32, (tk, tk), 1)
        prefix = jnp.where(ri <= ci, 1.0, 0.0).astype(BF16)

        def bias_chunk(kc, seen):
            off = pl.multiple_of(kc * tk, tk)
            key = score_ref[:, pl.ds(off, tk)]
            eq = jnp.where(key == thr, 1.0, 0.0)
            rank = seen + jnp.dot(eq.astype(BF16), prefix, preferred_element_type=F32)
            keep = (key > thr) | ((key == thr) & (rank <= room))
            bias_ref[:, pl.ds(off, tk)] = jnp.where(keep, 0.0, MASKED)
            return seen + jnp.sum(eq, axis=1, keepdims=True)

        lax.fori_loop(0, n_kc, bias_chunk, jnp.zeros((tq, 1), F32))

    n_kcc = ((qt + 1) * tq + tkc - 1) // tkc
    nlc = tkc // LANES

    def bias_tail(kc, carry):
        bias_ref[:, pl.ds(pl.multiple_of(kc * tk, tk), tk)] = jnp.full((tq, tk), MASKED, F32)
        return carry

    lax.fori_loop(n_kc, n_kcc * (tkc // tk), bias_tail, 0)
    rep = ATT_HEADS // KV_HEADS

    rows = hps * tq
    n_stacks = ATT_HEADS // hps
    kv_lanes = lambda g: slice((g * hps // rep) * HEAD_DIM, (g * hps // rep + 1) * HEAD_DIM)

    def logits_chunk(g, ks):
        slot = g % 2
        gs = kv_lanes(g)
        qg = q_ref[g * hps:(g + 1) * hps].reshape(rows, HEAD_DIM)
        bias = bias_ref[:, ks]
        s = jnp.concatenate([bias] * hps, axis=0) + _nt_dot(qg, k_ref[ks, gs])
        s_ref[slot, :, ks] = s
        part = s[:, :LANES]
        for c in range(1, nlc):
            part = jnp.maximum(part, s[:, c * LANES:(c + 1) * LANES])
        m_ref[slot] = jnp.maximum(m_ref[slot], part)

    def pv_chunk(g, ks):
        slot = g % 2
        gs = kv_lanes(g)
        p = jnp.exp2(s_ref[slot, :, ks] - _tile_lanes(m_ref[slot], nlc))
        l_ref[...] += _lane_tile_sum(p)
        acc_ref[...] += jnp.dot(p.astype(BF16), v_ref[ks, gs], preferred_element_type=F32)

    for stage in range(n_stacks + 1):
        g_logits = stage if stage < n_stacks else None
        g_pv = stage - 1 if stage > 0 else None
        if g_logits is not None:
            m_ref[g_logits % 2] = jnp.full((rows, LANES), MASKED, F32)
        if g_pv is not None:
            slot = g_pv % 2
            m_ref[slot] = jnp.broadcast_to(jnp.max(m_ref[slot], axis=1, keepdims=True), (rows, LANES))
            l_ref[...] = jnp.zeros((rows, LANES), F32)
            acc_ref[...] = jnp.zeros((rows, HEAD_DIM), F32)

        def stage_chunk(kc, carry, g_logits=g_logits, g_pv=g_pv):
            ks = pl.ds(pl.multiple_of(kc * tkc, tkc), tkc)
            if g_logits is not None:
                logits_chunk(g_logits, ks)
            if g_pv is not None:
                pv_chunk(g_pv, ks)
            return carry

        lax.fori_loop(0, n_kcc, stage_chunk, 0)
        if g_pv is not None:
            o = acc_ref[...] / jnp.sum(l_ref[...], axis=1, keepdims=True)
            for r in range(hps):
                hs = slice((g_pv * hps + r) * HEAD_DIM, (g_pv * hps + r + 1) * HEAD_DIM)
                o_ref[:, hs] = (o[r * tq:(r + 1) * tq] * z_ref[:, hs].astype(F32)).astype(o_ref.dtype)


def _prompt_attention(p, batch, seq, top_k):
    m = batch * seq
    tq = min(ATTN_Q_ROWS, seq)
    tk = min(ATTN_SCORE_KEYS, seq)
    nq = seq // tq
    att_w, kv_w = ATT_HEADS * HEAD_DIM, KV_HEADS * HEAD_DIM
    hps = ATTN_HEADS_PER_STACK
    rows = lambda w: pl.BlockSpec((tq, w), lambda b, t: (b * nq + t, 0))
    whole = lambda w: pl.BlockSpec((seq, w), lambda b, t: (b, 0), pipeline_mode=pl.Buffered(1))
    tkc = min(ATTN_SOFTMAX_KEYS, seq)
    kern = functools.partial(_attn_kernel, tq=tq, tk=tk, tkc=tkc, hps=hps, top_k=top_k)
    return pl.pallas_call(
        kern, grid=(batch, nq),
        in_specs=[pl.BlockSpec((ATT_HEADS, tq, HEAD_DIM), lambda b, t: (0, b * nq + t, 0)),
                  pl.BlockSpec((IDX_HEADS, tq, IDX_DIM), lambda b, t: (0, b * nq + t, 0)),
                  rows(IDX_HEADS), rows(att_w), whole(kv_w), whole(kv_w), whole(IDX_DIM)],
        out_specs=rows(att_w),
        out_shape=jax.ShapeDtypeStruct((m, att_w), BF16),
        scratch_shapes=[pltpu.VMEM((tq, seq), F32),
                        pltpu.VMEM((IDX_HEADS, tq, LANES), F32),
                        pltpu.VMEM((tq, seq), F32),
                        pltpu.VMEM((2, hps * tq, seq), F32),
                        pltpu.VMEM((2, hps * tq, LANES), F32),
                        pltpu.VMEM((hps * tq, LANES), F32),
                        pltpu.VMEM((hps * tq, HEAD_DIM), F32),
                        pltpu.VMEM((1, tq, LANES), F32)],
        compiler_params=_cparams(("parallel", "arbitrary")), name="prompt_attention",
    )(p["q"], p["qi"], p["wi"], p["z_att"], p["k_bf"], p["v_bf"], p["ki_bf"])


def _sample_score_kernel(pt_ref, qi_ref, wi_ref, *rest, n_pages, n_tok):
    page_refs, o_ref = rest[:n_pages], rest[n_pages]
    qi = qi_ref[...]
    w = jnp.broadcast_to(wi_ref[...], (qi.shape[0], LANES))
    for k in range(n_pages):
        kit = page_refs[k][...].astype(BF16)
        page = kit.shape[1]
        r = jnp.maximum(jnp.dot(qi, kit, preferred_element_type=F32), 0.0) * _tile_lanes(w, page // LANES)
        sc = jnp.sum(r.reshape(n_tok, IDX_HEADS, page), axis=1)
        o_ref[:, k * page:(k + 1) * page] = sc


def _sample_scores(qi_rows, wi_col, cache_idx_kt, page_table, n_tok):
    bsz, n_pages_total = page_table.shape
    page = cache_idx_kt.shape[2]
    g = SCORE_PAGES_PER_STEP if n_pages_total % SCORE_PAGES_PER_STEP == 0 else 1
    rows = n_tok * IDX_HEADS
    page_spec = lambda k: pl.BlockSpec((None, IDX_DIM, page), lambda b, s, pt: (pt[b, s * g + k], 0, 0))
    grid_spec = pltpu.PrefetchScalarGridSpec(
        num_scalar_prefetch=1, grid=(bsz, n_pages_total // g),
        in_specs=[pl.BlockSpec((None, rows, IDX_DIM), lambda b, s, pt: (b, 0, 0)),
                  pl.BlockSpec((None, rows, 1), lambda b, s, pt: (b, 0, 0))] + [page_spec(k) for k in range(g)],
        out_specs=pl.BlockSpec((None, n_tok, g * page), lambda b, s, pt: (b, 0, s)))
    kern = functools.partial(_sample_score_kernel, n_pages=g, n_tok=n_tok)
    return pl.pallas_call(
        kern, grid_spec=grid_spec,
        out_shape=jax.ShapeDtypeStruct((bsz, n_tok, n_pages_total * page), F32),
        compiler_params=_cparams(("parallel", "arbitrary")), name="sample_scores",
    )(page_table, qi_rows, wi_col, *([cache_idx_kt] * g))


def _sample_select_kernel(sc_ref, qi_ref, ki_ref, wi_ref, rexp_ref, keep_ref, keep_new_ref, *, n_tok, n_rows, top_k):
    rows, past = sc_ref.shape
    nk = ki_ref.shape[0]
    wi = wi_ref[...]
    ki = ki_ref[...]
    sc_new = jnp.zeros((rows, nk), F32)
    for h in range(IDX_HEADS):
        w = jnp.broadcast_to(wi[:, h:h + 1], (rows, LANES))
        sc_new = sc_new + jnp.maximum(_nt_dot(qi_ref[h], ki), 0.0) * _tile_lanes(w, nk // LANES)
    r = lax.broadcasted_iota(I32, (rows, nk), 0)
    c = lax.broadcasted_iota(I32, (rows, nk), 1)
    ok = (r // n_tok == c // n_tok) & (c % n_tok <= r % n_tok) & (r < n_rows) & (c < n_rows)
    sc_new = jnp.where(ok, sc_new, -jnp.inf)

    def count(cands, strict=False):
        def one(cand):
            hit = lambda sc, n: jnp.where((sc > _tile_lanes(cand, n)) if strict else (sc >= _tile_lanes(cand, n)), 1.0, 0.0)
            return jnp.sum(_lane_tile_sum(hit(sc_ref[...], past // LANES)) + _lane_tile_sum(hit(sc_new, nk // LANES)),
                           axis=1, keepdims=True)

        return [one(cand) for cand in cands]

    thr, _ = _kth_largest(count, (rows, LANES), top_k)
    rexp = rexp_ref[...]
    xw = rexp.shape[1]
    room = jnp.float32(top_k) - count([thr], strict=True)[0]
    ri = lax.broadcasted_iota(I32, (LANES, LANES), 0)
    ci = lax.broadcasted_iota(I32, (LANES, LANES), 1)
    prefix = jnp.where(ri <= ci, 1.0, 0.0).astype(BF16)

    def select(key_tile, seen):
        eq = jnp.where(key_tile == thr, 1.0, 0.0)
        rank = seen + jnp.dot(eq.astype(BF16), prefix, preferred_element_type=F32)
        keep = (key_tile > thr) | ((key_tile == thr) & (rank <= room))
        sel = jnp.where(keep, 1.0, 0.0).astype(BF16)
        return (jnp.dot(sel, rexp, preferred_element_type=F32).astype(BF16),
                seen + jnp.sum(eq, axis=1, keepdims=True))

    def past_tile(ct, seen):
        src = pl.ds(pl.multiple_of(ct * LANES, LANES), LANES)
        kept, seen = select(sc_ref[:, src], seen)
        keep_ref[:, pl.ds(pl.multiple_of(ct * xw, xw), xw)] = kept
        return seen

    seen = lax.fori_loop(0, past // LANES, past_tile, jnp.zeros((rows, 1), F32))
    for ct in range(nk // LANES):
        kept, seen = select(sc_new[:, ct * LANES:(ct + 1) * LANES], seen)
        keep_new_ref[:, ct * xw:(ct + 1) * xw] = kept


def _sample_select(scores, qi, ki_new_bf, wi, n_tok, n_rows, top_k):
    rows, past = scores.shape
    nk = ki_new_bf.shape[0]
    rexp = (jnp.arange(LANES * KV_HEADS)[None, :] // KV_HEADS == jnp.arange(LANES)[:, None]).astype(BF16)
    kern = functools.partial(_sample_select_kernel, n_tok=n_tok, n_rows=n_rows, top_k=top_k)
    full = lambda shape: pl.BlockSpec(shape, lambda i: (0,) * len(shape))
    return pl.pallas_call(
        kern, grid=(1,),
        in_specs=[full((rows, past)), full((IDX_HEADS, rows, IDX_DIM)), full((nk, IDX_DIM)), full((rows, IDX_HEADS)),
                  full(rexp.shape)],
        out_specs=[full((rows, past * KV_HEADS)), full((rows, nk * KV_HEADS))],
        out_shape=[jax.ShapeDtypeStruct((rows, past * KV_HEADS), BF16),
                   jax.ShapeDtypeStruct((rows, nk * KV_HEADS), BF16)],
        compiler_params=_cparams(("arbitrary",)), name="sample_select")(scores, qi, ki_new_bf, wi, rexp)


def _sample_attn_kernel(pt_ref, q_ref, z_ref, keep_ref, keep_new_ref, knew_ref, vnew_ref, *rest, n_pages, n_tok, page):
    k_refs, v_refs = rest[:n_pages], rest[n_pages:2 * n_pages]
    o_ref, m_ref, l_ref, acc_ref, s_ref = rest[2 * n_pages:]
    step = pl.program_id(1)
    rows = n_tok * ATT_HEADS
    cols = page * KV_HEADS
    rep = ATT_HEADS // KV_HEADS
    nl = cols // LANES

    @pl.when(step == 0)
    def _():
        m_ref[...] = jnp.full(m_ref.shape, MASKED, F32)
        l_ref[...] = jnp.zeros(l_ref.shape, F32)
        acc_ref[...] = jnp.zeros(acc_ref.shape, F32)

    q = q_ref[...]
    rr = lax.broadcasted_iota(I32, (rows, cols), 0)
    cc = lax.broadcasted_iota(I32, (rows, cols), 1)
    head_bias = jnp.where((cc % KV_HEADS) == ((rr % ATT_HEADS) // rep), 0.0, MASKED)

    def update(blocks):
        mx = None
        for i, (load_k, _, load_keep) in enumerate(blocks):
            kp = load_k().astype(BF16)
            kb = (load_keep().astype(F32) - 1.0) * (-MASKED)
            kb_rows = jnp.concatenate(
                [jnp.broadcast_to(kb[t:t + 1, :], (ATT_HEADS, cols)) for t in range(n_tok)], axis=0)
            s = _nt_dot(q, kp) + (kb_rows + head_bias)
            s_ref[:, i * cols:(i + 1) * cols] = s
            part = s[:, :LANES]
            for c in range(1, nl):
                part = jnp.maximum(part, s[:, c * LANES:(c + 1) * LANES])
            mx = part if mx is None else jnp.maximum(mx, part)
        m_old = m_ref[...]
        m_new = jnp.maximum(m_old, jnp.max(mx, axis=1, keepdims=True))
        alpha = jnp.exp2(m_old - m_new)
        m_t = _tile_lanes(m_new, nl)
        lsum = jnp.zeros((rows, LANES), F32)
        pv = jnp.zeros((rows, HEAD_DIM), F32)
        for i, (_, load_v, _) in enumerate(blocks):
            p = jnp.exp2(s_ref[:, i * cols:(i + 1) * cols] - m_t)
            lsum = lsum + _lane_tile_sum(p)
            pv = pv + jnp.dot(p.astype(BF16), load_v().astype(BF16), preferred_element_type=F32)
        l_ref[...] = alpha * l_ref[...] + lsum
        acc_ref[...] = alpha * acc_ref[...] + pv
        m_ref[...] = m_new

    update([(lambda k=k: k_refs[k][...], lambda k=k: v_refs[k][...], lambda k=k: keep_ref[:, k * cols:(k + 1) * cols])
            for k in range(n_pages)])

    @pl.when(step == pl.num_programs(1) - 1)
    def _():
        update([(lambda c=c: knew_ref[c * cols:(c + 1) * cols, :], lambda c=c: vnew_ref[c * cols:(c + 1) * cols, :],
                 lambda c=c: keep_new_ref[:, c * cols:(c + 1) * cols]) for c in range(knew_ref.shape[0] // cols)])
        lsum = jnp.sum(l_ref[...], axis=1, keepdims=True)
        o_ref[...] = (acc_ref[...] / lsum * z_ref[...].astype(F32)).astype(o_ref.dtype)


def _sample_attention(q_rows, z_rows, keep, keep_new, k_new, v_new, cache_k, cache_v, page_table, n_tok):
    bsz, n_pages_total = page_table.shape
    cols = cache_k.shape[1]
    page = cols // KV_HEADS
    g = PAGES_PER_STEP if n_pages_total % PAGES_PER_STEP == 0 else 1
    rows = n_tok * ATT_HEADS
    per_b = lambda shape: pl.BlockSpec((None,) + shape, lambda b, s, pt: (b,) + (0,) * len(shape))
    const = lambda shape: pl.BlockSpec(shape, lambda b, s, pt: (0,) * len(shape))
    page_spec = lambda k: pl.BlockSpec((None, cols, HEAD_DIM), lambda b, s, pt: (pt[b, s * g + k], 0, 0))
    grid_spec = pltpu.PrefetchScalarGridSpec(
        num_scalar_prefetch=1, grid=(bsz, n_pages_total // g),
        in_specs=[per_b((rows, HEAD_DIM)), per_b((rows, HEAD_DIM)),
                  pl.BlockSpec((None, n_tok, g * cols), lambda b, s, pt: (b, 0, s)),
                  per_b((n_tok, keep_new.shape[-1])), const(k_new.shape), const(v_new.shape)]
                 + [page_spec(k) for k in range(g)] * 2,
        out_specs=per_b((rows, HEAD_DIM)),
        scratch_shapes=[pltpu.VMEM((rows, LANES), F32), pltpu.VMEM((rows, LANES), F32),
                        pltpu.VMEM((rows, HEAD_DIM), F32), pltpu.VMEM((rows, g * cols), F32)])
    kern = functools.partial(_sample_attn_kernel, n_pages=g, n_tok=n_tok, page=page)
    return pl.pallas_call(
        kern, grid_spec=grid_spec,
        out_shape=jax.ShapeDtypeStruct((bsz, rows, HEAD_DIM), BF16),
        compiler_params=_cparams(("parallel", "arbitrary")), name="sample_attention",
    )(page_table, q_rows, z_rows, keep, keep_new, k_new, v_new, *([cache_k] * g), *([cache_v] * g))


def _ssd_kernel(*refs, t_rows, q_rows, has_state):
    if has_state:
        (xbc_ref, z_ref, dt_ref, conv0_ref, s0_ref, cw_ref, cb_ref, alog_ref, dskip_ref, ng_ref, e64_ref,
         y_ref, sfin_ref, cnew_ref, xp_ref, act_ref, dtp_ref, acst_ref, acsc_ref, st_ref) = refs
    else:
        (xbc_ref, z_ref, dt_ref, conv0_ref, cw_ref, cb_ref, alog_ref, dskip_ref, ng_ref, e64_ref,
         y_ref, sfin_ref, cnew_ref, xp_ref, act_ref, dtp_ref, acst_ref, acsc_ref, st_ref) = refs
        s0_ref = None
    c = pl.program_id(1)
    q = q_rows
    halo = SUBLANES
    n_heads = dt_ref.shape[-1]
    d_inner = n_heads * SSD_HEAD_DIM
    gw = d_inner // SSD_GROUPS
    hpg = n_heads // SSD_GROUPS
    conv_dim = xbc_ref.shape[-1]

    @pl.when(c == 0)
    def _():
        xp_ref[0:halo, :] = conv0_ref[...]
        if t_rows < q:
            xp_ref[halo:, :] = jnp.zeros((q, conv_dim), F32)
            dtp_ref[...] = jnp.zeros(dtp_ref.shape, F32)
        for g in range(SSD_GROUPS):
            if has_state:
                st_ref[g] = s0_ref[g * hpg:(g + 1) * hpg].reshape(gw, D_STATE).T
            else:
                st_ref[g] = jnp.zeros((D_STATE, gw), F32)

    xp_ref[halo:halo + t_rows, :] = xbc_ref[...]
    dtp_ref[0:t_rows, :] = dt_ref[...]

    cblk = CONV_LANES
    for cbi in range(conv_dim // cblk):
        cs = slice(cbi * cblk, (cbi + 1) * cblk)
        acc = jnp.broadcast_to(cb_ref[:, cs], (q, cblk))
        for tap in range(CONV_W):
            lo = halo - (CONV_W - 1) + tap
            acc = acc + xp_ref[lo:lo + q, cs] * cw_ref[tap:tap + 1, cs]
        act_ref[:, cs] = _silu(acc)

    dt = dtp_ref[...]
    a = -jnp.exp(alog_ref[...])
    ri = lax.broadcasted_iota(I32, (q, q), 0)
    ci = lax.broadcasted_iota(I32, (q, q), 1)
    tril = ri >= ci
    a_cs = _dot_exact_lhs(jnp.where(tril, 1.0, 0.0).astype(BF16), dt * a)
    a_last = a_cs[q - 1:q, :]
    acst_ref[...] = a_cs.T
    for h in range(n_heads):
        acsc_ref[h] = jnp.broadcast_to(a_cs[:, h:h + 1], (q, LANES))
    dt_b = dt.astype(BF16)
    dte_b = jnp.exp(a_last - a_cs).astype(BF16)
    ea_hi, ea_mid, _ = _split3(jnp.exp(a_cs))
    pad = jnp.zeros((SUBLANES - 2, n_heads), F32)
    row_pieces = _split3(jnp.concatenate([jnp.exp(a_last), dskip_ref[...], pad], axis=0))
    mxu = functools.partial(jnp.dot, preferred_element_type=F32)
    lane = lax.broadcasted_iota(I32, (1, LANES), 1)
    lo_half = lane < SSD_HEAD_DIM

    def group_body(g, carry):
        xs = act_ref[:, pl.ds(pl.multiple_of(g * gw, gw), gw)]
        bm = act_ref[:, pl.ds(pl.multiple_of(d_inner + g * D_STATE, D_STATE), D_STATE)]
        cm = act_ref[:, pl.ds(pl.multiple_of(d_inner + SSD_GROUPS * D_STATE + g * D_STATE, D_STATE), D_STATE)]
        e64g = e64_ref[:, pl.ds(pl.multiple_of(g * gw, gw), gw)]
        dt_x, dte_x = mxu(dt_b, e64g), mxu(dte_b, e64g)
        ea_x = mxu(ea_hi, e64g) + mxu(ea_mid, e64g)
        rows_x = mxu(row_pieces[0], e64g) + mxu(row_pieces[1], e64g) + mxu(row_pieces[2], e64g)
        cdec_x, dskip_x = rows_x[0:1], rows_x[1:2]
        xdt = xs * dt_x
        xdt_b = xdt.astype(BF16)
        bm_b, cm_b = bm.astype(BF16), cm.astype(BF16)
        cb = _nt_dot(cm_b, bm_b)
        st = st_ref[g]
        y_off = jnp.dot(cm_b, st.astype(BF16), preferred_element_type=F32) * ea_x
        y_parts = []
        for pr in range(hpg // 2):
            xpair = xdt_b[:, pr * LANES:(pr + 1) * LANES]
            halves = (jnp.where(lo_half, xpair, jnp.zeros_like(xpair)), jnp.where(lo_half, jnp.zeros_like(xpair), xpair))
            yp = jnp.zeros((q, LANES), F32)
            for s in range(2):
                hl = 2 * pr + s
                seg = acsc_ref[g * hpg + hl][:, :q] - acst_ref[pl.ds(g * hpg + hl, 1), :]
                lmat = jnp.where(tril, jnp.exp(jnp.where(tril, seg, 0.0)), 0.0)
                yp = yp + jnp.dot((cb * lmat).astype(BF16), halves[s], preferred_element_type=F32)
            y_parts.append(yp)
        y = jnp.concatenate(y_parts, axis=1) + y_off + dskip_x * xs
        st_ref[g] = cdec_x * st + jnp.dot(bm.T.astype(BF16), (xdt * dte_x).astype(BF16), preferred_element_type=F32)
        gsl = pl.ds(pl.multiple_of(g * gw, gw), gw)
        yz = y[0:t_rows] * z_ref[:, gsl].astype(F32)
        ms = jnp.mean(yz * yz, axis=-1, keepdims=True)
        y_ref[:, gsl] = (yz * lax.rsqrt(ms + EPS) * ng_ref[:, gsl]).astype(y_ref.dtype)
        return carry

    def group_batch(i, carry):
        for u in range(GROUPS_PER_TRIP):
            group_body(GROUPS_PER_TRIP * i + u, carry)
        return carry

    lax.fori_loop(0, SSD_GROUPS // GROUPS_PER_TRIP, group_batch, 0)

    @pl.when(c == pl.num_programs(1) - 1)
    def _():
        cnew_ref[...] = xp_ref[halo + t_rows - (CONV_W - 1):halo + t_rows, :]
        for g in range(SSD_GROUPS):
            sfin_ref[g * hpg:(g + 1) * hpg] = st_ref[g].T.reshape(hpg, SSD_HEAD_DIM, D_STATE)

    if t_rows == q:
        @pl.when(c < pl.num_programs(1) - 1)
        def _():
            xp_ref[0:halo, :] = xp_ref[q:q + halo, :]


def _ssd(xbc, z, dt, conv0, s0, conv_w, conv_b, a_log, d_skip, norm_g, bsz, seq):
    conv_dim = xbc.shape[-1]
    n_heads = dt.shape[-1]
    d_inner = n_heads * SSD_HEAD_DIM
    t_rows = min(CHUNK, seq)
    n_chunks = seq // t_rows
    q_rows = CHUNK if t_rows == CHUNK else _round_up(t_rows, BF16_ROWS)
    has_state = s0 is not None
    hpg = n_heads // SSD_GROUPS
    e64 = (jnp.arange(d_inner)[None, :] // SSD_HEAD_DIM == jnp.arange(n_heads)[:, None]).astype(BF16)
    x3 = lambda a: a.reshape(bsz, seq, a.shape[-1])
    rows = lambda w: pl.BlockSpec((None, t_rows, w), lambda b, c: (b, c, 0))
    const = lambda shape: pl.BlockSpec(shape, lambda b, c: (0,) * len(shape))
    state_spec = pl.BlockSpec((None, n_heads, SSD_HEAD_DIM, D_STATE), lambda b, c: (b, 0, 0, 0))
    args = [x3(xbc), x3(z), x3(dt), conv0]
    in_specs = [rows(conv_dim), rows(d_inner), rows(n_heads),
                pl.BlockSpec((None, SUBLANES, conv_dim), lambda b, c: (b, 0, 0))]
    if has_state:
        args.append(s0)
        in_specs.append(state_spec)
    args += [conv_w, conv_b.reshape(1, conv_dim), a_log.reshape(1, n_heads), d_skip.reshape(1, n_heads),
             norm_g.reshape(1, d_inner), e64]
    in_specs += [const((CONV_W, conv_dim)), const((1, conv_dim)), const((1, n_heads)), const((1, n_heads)),
                 const((1, d_inner)), const(e64.shape)]
    kern = functools.partial(_ssd_kernel, t_rows=t_rows, q_rows=q_rows, has_state=has_state)
    y, sfin, cnew = pl.pallas_call(
        kern, grid=(bsz, n_chunks), in_specs=in_specs,
        out_specs=[rows(d_inner), state_spec,
                   pl.BlockSpec((None, CONV_W - 1, conv_dim), lambda b, c: (b, 0, 0))],
        out_shape=[jax.ShapeDtypeStruct((bsz, seq, d_inner), BF16),
                   jax.ShapeDtypeStruct((bsz, n_heads, SSD_HEAD_DIM, D_STATE), F32),
                   jax.ShapeDtypeStruct((bsz, CONV_W - 1, conv_dim), F32)],
        scratch_shapes=[pltpu.VMEM((SUBLANES + q_rows, conv_dim), F32),
                        pltpu.VMEM((q_rows, conv_dim), F32),
                        pltpu.VMEM((q_rows, n_heads), F32),
                        pltpu.VMEM((n_heads, q_rows), F32),
                        pltpu.VMEM((n_heads, q_rows, LANES), F32),
                        pltpu.VMEM((SSD_GROUPS, D_STATE, hpg * SSD_HEAD_DIM), F32)],
        compiler_params=_cparams(("parallel", "arbitrary")), name="ssd")(*args)
    return y.reshape(bsz * seq, d_inner), sfin, cnew


def _merge_kernel(ya_ref, ys_ref, wa_ref, *rest):
    ws_refs, (ga_ref, gs_ref, o_ref) = rest[:-3], rest[-3:]
    kb = wa_ref.shape[0]
    ya = jnp.dot(ya_ref[...], wa_ref[...], preferred_element_type=F32)
    yb = jnp.dot(ys_ref[:, :kb], ws_refs[0][...], preferred_element_type=F32)
    for i in range(1, len(ws_refs)):
        yb = yb + jnp.dot(ys_ref[:, i * kb:(i + 1) * kb], ws_refs[i][...], preferred_element_type=F32)
    o_ref[...] = (ga_ref[...].astype(F32) * ya + gs_ref[...].astype(F32) * yb).astype(o_ref.dtype)


def _merge(y_att, y_ssd, wb, gates, tm):
    m, att_w = y_att.shape
    d_inner = y_ssd.shape[1]
    d = wb.shape[1]
    tn = MERGE_COLS
    nj = d // tn
    n_ssd = d_inner // att_w
    w_blk = lambda r: pl.BlockSpec((att_w, tn), lambda i, j: (r, j))
    return pl.pallas_call(
        _merge_kernel, grid=(m // tm, nj),
        in_specs=[pl.BlockSpec((tm, att_w), lambda i, j: (i, 0)), pl.BlockSpec((tm, d_inner), lambda i, j: (i, 0))]
                 + [w_blk(r) for r in range(1 + n_ssd)]
                 + [pl.BlockSpec((tm, tn), lambda i, j: (i, j)), pl.BlockSpec((tm, tn), lambda i, j: (i, j + nj))],
        out_specs=pl.BlockSpec((tm, tn), lambda i, j: (i, j)),
        out_shape=jax.ShapeDtypeStruct((m, d), BF16),
        compiler_params=pltpu.CompilerParams(
            dimension_semantics=("parallel", "parallel"), vmem_limit_bytes=VMEM_LIMIT,
            allow_input_fusion=[False, False] + [True] * (1 + n_ssd) + [False, False]), name="merge",
    )(y_att, y_ssd, *([wb] * (1 + n_ssd)), gates, gates)


def _out_kernel(m_ref, w_ref, x_ref, g_ref, o_ref):
    h = x_ref[...] + jnp.dot(m_ref[...], w_ref[...], preferred_element_type=F32)
    ms = jnp.mean(h * h, axis=-1, keepdims=True)
    o_ref[...] = h * lax.rsqrt(ms + EPS) * g_ref[...]


def _out_proj(merged, w_out, x, final_g, tm):
    m, d = x.shape
    return pl.pallas_call(
        _out_kernel, grid=(m // tm,),
        in_specs=[pl.BlockSpec((tm, d), lambda i: (i, 0)), pl.BlockSpec((d, d), lambda i: (0, 0)),
                  pl.BlockSpec((tm, d), lambda i: (i, 0)), pl.BlockSpec((1, d), lambda i: (0, 0))],
        out_specs=pl.BlockSpec((tm, d), lambda i: (i, 0)),
        out_shape=jax.ShapeDtypeStruct((m, d), F32),
        compiler_params=pltpu.CompilerParams(
            dimension_semantics=("parallel",), vmem_limit_bytes=VMEM_LIMIT,
            allow_input_fusion=[False, True, False, False]), name="out_proj")(merged, w_out, x, final_g.reshape(1, d))


def _round_up(x, n):
    return (x + n - 1) // n * n


def kernel(x_prompt, x_sample, cache_k, cache_v, cache_idx_k, state_ssm, state_conv, page_table, norm_g, w_in, conv_w,
           conv_b, dt_bias, a_log, d_skip, ssd_norm_g, idx_ln_w, idx_ln_b, w_branch, w_out, final_norm_g):
    assert w_in.shape[0] == 1, "single-layer trunk"
    bp, seq, d = x_prompt.shape
    bs, n_tok, _ = x_sample.shape
    n_phys, page = cache_k.shape[1], cache_k.shape[2]
    past = page_table.shape[1] * page
    att_w, kv_w = ATT_HEADS * HEAD_DIM, KV_HEADS * HEAD_DIM
    d_inner = 2 * d
    conv_dim = d_inner + 2 * SSD_GROUPS * D_STATE
    top_k_p = max(1, min(TOPK_MAX, seq // 4))
    top_k_s = max(1, min(TOPK_MAX, (past + n_tok) // 4))

    wb = w_branch[0].astype(BF16)
    w_out_b = w_out[0].astype(BF16)
    w_in0 = jnp.swapaxes(w_in[0], 0, 1)

    mp = bp * seq
    tm_p = min(PROJ_ROWS, seq)
    xp = x_prompt.reshape(mp, d)
    xn_p = _rmsnorm_bf16(xp, norm_g[0], min(NORM_ROWS, seq))
    pp = _projections(xn_p, w_in0, jnp.arange(seq, dtype=I32), seq // tm_p, tm_p, idx_ln_w[0], idx_ln_b[0], dt_bias[0])
    y_att_p = _prompt_attention(pp, bp, seq, top_k_p)
    y_ssd_p, ssm_p, conv_p = _ssd(pp["xbc"], pp["z_ssd"], pp["dt"], jnp.zeros((bp, SUBLANES, conv_dim), F32), None,
                                  conv_w[0], conv_b[0], a_log[0], d_skip[0], ssd_norm_g[0], bp, seq)
    merged_p = _merge(y_att_p, y_ssd_p, wb, pp["gates"], min(MERGE_ROWS, seq))
    y_p = _out_proj(merged_p, w_out_b, xp, final_norm_g, min(NORM_ROWS, seq))

    ms_rows = bs * n_tok
    rows_pad = _round_up(ms_rows, LANES)
    xs = x_sample.reshape(ms_rows, d)
    xs_pad = jnp.pad(xs, ((0, rows_pad - ms_rows), (0, 0))) if rows_pad != ms_rows else xs
    pos_s = past + (jnp.arange(rows_pad, dtype=I32) % n_tok)
    xn_s = _rmsnorm_bf16(xs_pad, norm_g[0], rows_pad)
    ps = _projections(xn_s, w_in0, pos_s, 1, rows_pad, idx_ln_w[0], idx_ln_b[0], dt_bias[0])
    real = lambda a: a[:ms_rows]

    qi_rows = jnp.transpose(ps["qi"][:, :ms_rows], (1, 0, 2)).reshape(bs, n_tok * IDX_HEADS, IDX_DIM)
    wi_col = real(ps["wi"]).reshape(bs, n_tok * IDX_HEADS, 1)
    scores = _sample_scores(qi_rows, wi_col, jnp.swapaxes(cache_idx_k[0], 1, 2), page_table, n_tok)
    scores = scores.reshape(ms_rows, past)
    if rows_pad != ms_rows:
        scores = jnp.pad(scores, ((0, rows_pad - ms_rows), (0, 0)))
    keep, keep_new = _sample_select(scores, ps["qi"], ps["ki_bf"], ps["wi"], n_tok, ms_rows, top_k_s)
    keep = real(keep).reshape(bs, n_tok, past * KV_HEADS)
    keep_new = real(keep_new).reshape(bs, n_tok, rows_pad * KV_HEADS)
    q_rows = jnp.transpose(ps["q"][:, :ms_rows], (1, 0, 2)).reshape(bs, n_tok * ATT_HEADS, HEAD_DIM)
    z_rows = real(ps["z_att"]).reshape(bs, n_tok * ATT_HEADS, HEAD_DIM)
    k_new, v_new = ps["k"], ps["v"]
    ck = cache_k[0].reshape(n_phys, page * KV_HEADS, HEAD_DIM)
    cv = cache_v[0].reshape(n_phys, page * KV_HEADS, HEAD_DIM)
    y_att_s = _sample_attention(q_rows, z_rows, keep, keep_new, k_new, v_new, ck, cv, page_table, n_tok)
    y_att_s = y_att_s.reshape(ms_rows, att_w)

    conv0_s = jnp.pad(state_conv[0], ((0, 0), (SUBLANES - (CONV_W - 1), 0), (0, 0)))
    y_ssd_s, ssm_s, conv_s = _ssd(real(ps["xbc"]), real(ps["z_ssd"]), real(ps["dt"]), conv0_s, state_ssm[0],
                                  conv_w[0], conv_b[0], a_log[0], d_skip[0], ssd_norm_g[0], bs, n_tok)
    if rows_pad != ms_rows:
        padr = lambda a: jnp.pad(a, ((0, rows_pad - ms_rows), (0, 0)))
        y_att_s, y_ssd_s = padr(y_att_s), padr(y_ssd_s)
    merged_s = _merge(y_att_s, y_ssd_s, wb, ps["gates"], rows_pad)
    y_s = real(_out_proj(merged_s, w_out_b, xs_pad, final_norm_g, rows_pad))

    kv5 = lambda a, b, t: a[:b * t * KV_HEADS].reshape(1, b, t, KV_HEADS, HEAD_DIM)
    return (y_p.reshape(bp, seq, d), y_s.reshape(bs, n_tok, d),
            kv5(pp["k"], bp, seq), kv5(pp["v"], bp, seq), pp["ki"].reshape(1, bp, seq, IDX_DIM),
            ssm_p[None], conv_p[None],
            kv5(ps["k"], bs, n_tok), kv5(ps["v"], bs, n_tok), real(ps["ki"]).reshape(1, bs, n_tok, IDX_DIM),
            ssm_s[None], conv_s[None])
```
